```python
import jax, jax.numpy as jnp
from jax import lax
import numpy as np

D_MODEL = 1024
BATCH = 16
SEQ = 2048
DEPTH = 2
DEC_BATCH = 128
DEC_SEQ = 4
PAST_LEN = 16384
PAGE_SIZE = 128

N_BRANCH = 4
BRANCH_WIDTH = D_MODEL // N_BRANCH
EPS = 1e-6

GLA_HEADS = 4
GLA_HEAD_K = BRANCH_WIDTH // GLA_HEADS // 2
GLA_HEAD_V = BRANCH_WIDTH // GLA_HEADS
GLA_GATE_RANK = 16
GLA_GATE_TAU = 16.0
GLA_CHUNK = 32

MLA_HEADS = 4
MLA_NOPE_DIM = 64
MLA_ROPE_DIM = 32
MLA_V_DIM = BRANCH_WIDTH // MLA_HEADS
MLA_Q_LORA = 256
MLA_KV_LORA = 128
ROPE_THETA = 10000.0
Q_BLOCK = 128

RWKV_HEADS = 4
RWKV_HEAD = BRANCH_WIDTH // RWKV_HEADS
RWKV_DECAY_LORA = 64
RWKV_A_LORA = 64
RWKV_GATE_LORA = 128
RWKV_LN_EPS = 64e-5

MLSTM_HEADS = 4
MLSTM_HEAD_QK = BRANCH_WIDTH // MLSTM_HEADS
MLSTM_HEAD_V = BRANCH_WIDTH // MLSTM_HEADS
MLSTM_CHUNK = 64

MEM_LEN = 256
CA_HEADS = 4
CA_HEAD_DIM = D_MODEL // CA_HEADS

D_FF = 2816
CONV_W = 3

GLA_SIZES = (GLA_HEADS * GLA_HEAD_K, GLA_HEADS * GLA_HEAD_K, GLA_HEADS * GLA_HEAD_V, GLA_GATE_RANK, GLA_HEADS * GLA_HEAD_V)
MLA_SIZES = (MLA_Q_LORA, MLA_KV_LORA, MLA_ROPE_DIM)
RWKV_SIZES = (BRANCH_WIDTH, BRANCH_WIDTH, BRANCH_WIDTH, RWKV_DECAY_LORA, RWKV_A_LORA, RWKV_GATE_LORA)
MLSTM_SIZES = (MLSTM_HEADS * MLSTM_HEAD_QK, MLSTM_HEADS * MLSTM_HEAD_QK, MLSTM_HEADS * MLSTM_HEAD_V, MLSTM_HEADS, MLSTM_HEADS, MLSTM_HEADS * MLSTM_HEAD_V)
RWKV_PROJ = sum(RWKV_SIZES)
PROJ_SIZES = (sum(GLA_SIZES), sum(MLA_SIZES), RWKV_PROJ, sum(MLSTM_SIZES), N_BRANCH * D_MODEL)
PROJ_TOTAL = sum(PROJ_SIZES)

kernel_name = 'hybrid_gated_branch_decoder_step'


def _rms_norm(x, g):
    xf = x.astype(jnp.float32)
    y = xf * lax.rsqrt(jnp.mean(xf * xf, axis=-1, keepdims=True) + EPS)
    return (y * g.astype(jnp.float32)).astype(x.dtype)


def _head_rms_norm(x, g):
    return x * lax.rsqrt(jnp.mean(x * x, axis=-1, keepdims=True) + EPS) * g


def _head_layer_norm(x, w, b, eps):
    xc = x - jnp.mean(x, axis=-1, keepdims=True)
    y = xc * lax.rsqrt(jnp.mean(xc * xc, axis=-1, keepdims=True) + eps) * w
    return y if b is None else y + b


def _split_cols(x, sizes):
    return jnp.split(x, np.cumsum(sizes)[:-1].tolist(), axis=-1)


def _heads(t, n):
    return t.reshape(t.shape[:-1] + (t.shape[-1] // n, n))


def _rope(x, pos):
    half = MLA_ROPE_DIM // 2
    inv_freq = ROPE_THETA ** (-jnp.arange(half, dtype=jnp.float32) / half)
    ang = pos[:, None] * inv_freq[None, :]
    shape = (1, x.shape[1]) + (1,) * (x.ndim - 3) + (half,)
    cos, sin = jnp.cos(ang).reshape(shape), jnp.sin(ang).reshape(shape)
    xf = x.astype(jnp.float32)
    x1, x2 = xf[..., :half], xf[..., half:]
    return jnp.concatenate([x1 * cos - x2 * sin, x1 * sin + x2 * cos], axis=-1).astype(x.dtype)


def _to_blocks(t, n, L):
    B, T, H, d = t.shape
    return t.reshape(B, n, L, H, d).transpose(1, 0, 3, 2, 4)


def _from_blocks(t):
    n, B, H, L, d = t.shape
    return t.transpose(1, 0, 3, 2, 4).reshape(B, n * L, H, d)


def _gla_chunked(q, k, v, log_a, s0):
    T = q.shape[1]
    L = GLA_CHUNK if T % GLA_CHUNK == 0 else T
    n = T // L
    causal = jnp.tril(jnp.ones((L, L), dtype=bool))

    def step(S, blk):
        qb, kb, vb, gb = blk
        b = jnp.cumsum(gb, axis=-2)
        q_dec = qb * jnp.exp(b)
        k_inv = kb * jnp.exp(-b)
        k_end = kb * jnp.exp(b[:, :, -1:, :] - b)
        att = jnp.where(causal, jnp.einsum('bhlk,bhsk->bhls', q_dec, k_inv), 0.0)
        out = jnp.einsum('bhlk,bhkv->bhlv', q_dec, S) + jnp.einsum('bhls,bhsv->bhlv', att, vb)
        S = S * jnp.exp(b[:, :, -1, :])[..., None] + jnp.einsum('bhlk,bhlv->bhkv', k_end, vb)
        return S, out

    S, out = lax.scan(step, s0, tuple(_to_blocks(t, n, L) for t in (q, k, v, log_a)))
    return _from_blocks(out), S


def _mlstm_chunked(q, k, v, i_pre, log_f, c0, n0, m0):
    B, T, H, _ = q.shape
    L = MLSTM_CHUNK if T % MLSTM_CHUNK == 0 else T
    n = T // L
    causal = jnp.tril(jnp.ones((L, L), dtype=bool))
    gate_blocks = lambda g: g.reshape(B, n, L, H).transpose(1, 0, 3, 2)

    def step(carry, blk):
        C, nv, m = carry
        qb, kb, vb, ib, fb = blk
        b = jnp.cumsum(fb, axis=-1)
        d = jnp.where(causal, b[..., :, None] - b[..., None, :] + ib[..., None, :], -jnp.inf)
        inter = b + m[..., None]
        m_t = jnp.maximum(inter, jnp.max(d, axis=-1))
        w_inter = jnp.exp(inter - m_t)
        w_intra = jnp.exp(d - m_t[..., None]) * jnp.einsum('bhlk,bhsk->bhls', qb, kb)
        num = w_inter[..., None] * jnp.einsum('bhlk,bhkv->bhlv', qb, C) + jnp.einsum('bhls,bhsv->bhlv', w_intra, vb)
        den = w_inter * jnp.einsum('bhlk,bhk->bhl', qb, nv) + jnp.sum(w_intra, axis=-1)
        h = num / jnp.maximum(jnp.abs(den), jnp.exp(-m_t))[..., None]
        m_new = m_t[..., -1]
        scale_old = jnp.exp(b[..., -1] + m - m_new)
        w_end = jnp.exp(b[..., -1:] - b + ib - m_new[..., None])
        C = scale_old[..., None, None] * C + jnp.einsum('bhl,bhlk,bhlv->bhkv', w_end, kb, vb)
        nv = scale_old[..., None] * nv + jnp.einsum('bhl,bhlk->bhk', w_end, kb)
        return (C, nv, m_new), h

    blks = (_to_blocks(q, n, L), _to_blocks(k, n, L), _to_blocks(v, n, L), gate_blocks(i_pre), gate_blocks(log_f))
    (C, nv, m), h = lax.scan(step, (c0, n0, m0), blks)
    return _from_blocks(h), C, nv, m


def _rwkv7_scan(r, decay, k, v, kk, a, s0):
    def step(S, inp):
        r_t, w_t, k_t, v_t, kk_t, a_t = inp
        S = (S * w_t[:, :, None, :]
             - jnp.einsum('bhvk,bhk->bhv', S, kk_t)[..., None] * (kk_t * a_t)[:, :, None, :]
             + v_t[..., None] * k_t[:, :, None, :])
        return S, jnp.einsum('bhvk,bhk->bhv', S, r_t)

    S, y = lax.scan(step, s0, tuple(jnp.swapaxes(t, 0, 1) for t in (r, decay, k, v, kk, a)))
    return jnp.swapaxes(y, 0, 1), S


def _latent_attend(q_lat, q_pe, key_sets):
    scale = (MLA_NOPE_DIM + MLA_ROPE_DIM) ** -0.5
    scores = []
    for ckv, kpe, mask in key_sets:
        s = (jnp.einsum('bthc,bsc->bhts', q_lat, ckv) + jnp.einsum('bthr,bsr->bhts', q_pe, kpe)).astype(jnp.float32) * scale
        if mask is not None:
            s = jnp.where(mask, s, -jnp.inf)
        scores.append(s)
    probs = jax.nn.softmax(jnp.concatenate(scores, axis=-1), axis=-1)
    out, start = None, 0
    for ckv, _, _ in key_sets:
        stop = start + ckv.shape[1]
        part = jnp.einsum('bhts,bsc->bthc', probs[..., start:stop].astype(ckv.dtype), ckv)
        out = part if out is None else out + part
        start = stop
    return out


def _mla_prompt(q_lat, q_pe, ckv, kpe):
    T = q_lat.shape[1]
    outs = []
    for start in range(0, T, Q_BLOCK):
        stop = min(start + Q_BLOCK, T)
        mask = jnp.arange(stop)[None, :] <= jnp.arange(start, stop)[:, None]
        outs.append(_latent_attend(q_lat[:, start:stop], q_pe[:, start:stop], ((ckv[:, :stop], kpe[:, :stop], mask),)))
    return jnp.concatenate(outs, axis=1)


def _layer(x, lp, st, pos0, mla_past, mem_k, mem_v):
    bsz, T, _ = x.shape
    dt = x.dtype
    f32 = jnp.float32
    gla_s, rwkv_s, shift_s, mc_s, mn_s, mm_s, conv_s = st
    pos = pos0 + jnp.arange(T, dtype=f32)

    h = _rms_norm(x, lp['norm_pre_mix'])
    proj = h @ lp['w_in']
    p_gla, p_mla, p_rwkv, p_mlstm, p_gate = _split_cols(proj, PROJ_SIZES)

    gq, gk, gv, glo, gr = _split_cols(p_gla.astype(f32), GLA_SIZES)
    log_a = jax.nn.log_sigmoid(glo @ lp['gla_w_gate2'] + lp['gla_b_gate']) / GLA_GATE_TAU
    o_gla, gla_new = _gla_chunked(_heads(gq, GLA_HEAD_K) * GLA_HEAD_K ** -0.5, _heads(gk, GLA_HEAD_K),
                                  _heads(gv, GLA_HEAD_V), _heads(log_a, GLA_HEAD_K), gla_s.astype(f32))
    o_gla = (_head_rms_norm(o_gla, lp['gla_norm']) * jax.nn.silu(_heads(gr, GLA_HEAD_V))).reshape(bsz, T, BRANCH_WIDTH)

    cq, ckv_raw, kr = _split_cols(p_mla, MLA_SIZES)
    q = (_rms_norm(cq, lp['mla_norm_q']) @ lp['mla_w_uq']).reshape(bsz, T, MLA_HEADS, MLA_NOPE_DIM + MLA_ROPE_DIM)
    q_pe = _rope(q[..., MLA_NOPE_DIM:], pos)
    q_lat = jnp.einsum('bthd,hcd->bthc', q[..., :MLA_NOPE_DIM], lp['mla_w_uk'])
    ckv = _rms_norm(ckv_raw, lp['mla_norm_kv'])
    kpe = _rope(kr, pos)
    if mla_past is None:
        o_lat = _mla_prompt(q_lat, q_pe, ckv, kpe)
    else:
        causal = jnp.tril(jnp.ones((T, T), dtype=bool))
        o_lat = _latent_attend(q_lat, q_pe, ((mla_past[0], mla_past[1], None), (ckv, kpe, causal)))
    o_mla = jnp.einsum('bthc,hcd->bthd', o_lat, lp['mla_w_uv']).reshape(bsz, T, BRANCH_WIDTH)

    shifted = jnp.concatenate([shift_s[:, None, :].astype(dt), p_rwkv[:, :-1]], axis=1)
    pm = (p_rwkv + (shifted - p_rwkv) * lp['rwkv_mu']).astype(f32)
    rr, rk, rv, wlo, alo, rglo = _split_cols(pm, RWKV_SIZES)
    decay = jnp.exp(-jnp.exp(-jax.nn.softplus(-(lp['rwkv_w0'] + jnp.tanh(wlo) @ lp['rwkv_w2'])) - 0.5))
    a = jax.nn.sigmoid(lp['rwkv_a0'] + alo @ lp['rwkv_a2'])
    g = jax.nn.sigmoid(rglo) @ lp['rwkv_g2']
    kk = _heads(rk * lp['rwkv_k_k'], RWKV_HEAD)
    kk = kk / jnp.maximum(jnp.sqrt(jnp.sum(kk * kk, axis=-1, keepdims=True)), 1e-12)
    rk = rk * (1.0 + (a - 1.0) * lp['rwkv_k_a'])
    r_h, k_h, v_h = _heads(rr, RWKV_HEAD), _heads(rk, RWKV_HEAD), _heads(rv, RWKV_HEAD)
    y, rwkv_new = _rwkv7_scan(r_h, _heads(decay, RWKV_HEAD), k_h, v_h, kk, _heads(a, RWKV_HEAD), rwkv_s.astype(f32))
    y = _head_layer_norm(y, _heads(lp['rwkv_ln_w'], RWKV_HEAD), _heads(lp['rwkv_ln_b'], RWKV_HEAD), RWKV_LN_EPS)
    y = y + jnp.sum(r_h * k_h * lp['rwkv_r_k'], axis=-1, keepdims=True) * v_h
    o_rwkv = y.reshape(bsz, T, BRANCH_WIDTH) * g
    shift_new = p_rwkv[:, -1]

    mlq, mlk, mlv, mli, mlf, mlo = _split_cols(p_mlstm.astype(f32), MLSTM_SIZES)
    h_m, mc_new, mn_new, mm_new = _mlstm_chunked(
        _heads(mlq, MLSTM_HEAD_QK) * MLSTM_HEAD_QK ** -0.5, _heads(mlk, MLSTM_HEAD_QK), _heads(mlv, MLSTM_HEAD_V),
        mli + lp['mlstm_i_bias'], jax.nn.log_sigmoid(mlf + lp['mlstm_f_bias']),
        mc_s.astype(f32), mn_s.astype(f32), mm_s.astype(f32))
    h_m = _head_layer_norm(h_m, _heads(lp['mlstm_norm'], MLSTM_HEAD_V), None, EPS)
    o_mlstm = h_m.reshape(bsz, T, BRANCH_WIDTH) * jax.nn.sigmoid(mlo)

    branches = jnp.stack([o_gla, o_mla.astype(f32), o_rwkv, o_mlstm], axis=2).astype(dt)
    up = jnp.einsum('btnc,ncd->btnd', branches, lp['w_branch'])
    gates = jax.nn.sigmoid(p_gate.reshape(bsz, T, N_BRANCH, D_MODEL))
    mixed = jnp.sum(gates * up, axis=2) @ lp['w_out']
    x = x + _rms_norm(mixed, lp['norm_post_mix'])

    h = _rms_norm(x, lp['norm_pre_ca'])
    qc = (h @ lp['ca_w_q']).reshape(bsz, T, CA_HEADS, CA_HEAD_DIM)
    s = jnp.einsum('bthd,bshd->bhts', qc, mem_k).astype(f32) * CA_HEAD_DIM ** -0.5
    pa = jax.nn.softmax(s, axis=-1).astype(mem_v.dtype)
    ca = jnp.einsum('bhts,bshd->bthd', pa, mem_v).reshape(bsz, T, D_MODEL) @ lp['ca_w_o']
    x = x + _rms_norm(ca, lp['norm_post_ca'])

    h = _rms_norm(x, lp['norm_pre_ffn'])
    a_up, g_up = jnp.split(h @ lp['ffn_w_up'], 2, axis=-1)
    a_full = jnp.concatenate([conv_s.astype(a_up.dtype), a_up], axis=1)
    conv = lp['ffn_conv_b'] + a_full[:, 0:T] * lp['ffn_conv_w'][0]
    for j in range(1, CONV_W):
        conv = conv + a_full[:, j:j + T] * lp['ffn_conv_w'][j]
    f = (jax.nn.gelu(conv, approximate=False) * g_up) @ lp['ffn_w_down']
    x = x + _rms_norm(f, lp['norm_post_ffn'])
    conv_new = a_full[:, T:]

    new_state = (ckv.astype(dt), kpe.astype(dt), gla_new.astype(dt), rwkv_new.astype(dt), shift_new.astype(dt),
                 mc_new.astype(dt), mn_new.astype(dt), mm_new.astype(dt), conv_new.astype(dt))
    return x, new_state


def _stack_field(per_layer, i):
    return jnp.stack([st[i] for st in per_layer], axis=0)


def setup_inputs(seed: int = 0) -> dict:
    key = jax.random.key(seed)
    ks = iter(jax.random.split(key, 64))
    f32 = jnp.float32

    def nrm(shape, scale=1.0):
        return scale * jax.random.normal(next(ks), shape, f32)

    def gain(shape):
        return 1.0 + 0.02 * jax.random.normal(next(ks), shape, f32)

    L, D, BW = DEPTH, D_MODEL, BRANCH_WIDTH
    n_pages = PAST_LEN // PAGE_SIZE
    n_used = DEC_BATCH * n_pages
    n_pool = n_used + n_used // 4
    x_prompt = nrm((BATCH, SEQ, D))
    x_sample = nrm((DEC_BATCH, DEC_SEQ, D))
    mem_prompt = nrm((BATCH, MEM_LEN, D))
    cache_ckv = nrm((L, n_pool, PAGE_SIZE, MLA_KV_LORA))
    cache_kpe = nrm((L, n_pool, PAGE_SIZE, MLA_ROPE_DIM))
    page_table = jax.random.permutation(next(ks), n_pool)[:n_used].reshape(DEC_BATCH, n_pages).astype(jnp.int32)
    return {
        'x_prompt': x_prompt,
        'x_sample': x_sample,
        'mem_prompt': mem_prompt,
        'cache_ckv': cache_ckv,
        'cache_kpe': cache_kpe,
        'page_table': page_table,
        'cache_mem_k': nrm((L, DEC_BATCH, MEM_LEN, CA_HEADS, CA_HEAD_DIM)),
        'cache_mem_v': nrm((L, DEC_BATCH, MEM_LEN, CA_HEADS, CA_HEAD_DIM)),
        'state_gla': nrm((L, DEC_BATCH, GLA_HEADS, GLA_HEAD_K, GLA_HEAD_V), 0.3),
        'state_rwkv': nrm((L, DEC_BATCH, RWKV_HEADS, RWKV_HEAD, RWKV_HEAD), 0.3),
        'state_rwkv_shift': nrm((L, DEC_BATCH, RWKV_PROJ)),
        'state_mlstm_c': nrm((L, DEC_BATCH, MLSTM_HEADS, MLSTM_HEAD_QK, MLSTM_HEAD_V), 0.3),
        'state_mlstm_n': nrm((L, DEC_BATCH, MLSTM_HEADS, MLSTM_HEAD_QK), 0.3),
        'state_mlstm_m': nrm((L, DEC_BATCH, MLSTM_HEADS)),
        'state_conv': nrm((L, DEC_BATCH, CONV_W - 1, D_FF)),
        'norm_pre_mix': gain((L, D)),
        'norm_post_mix': gain((L, D)),
        'norm_pre_ca': gain((L, D)),
        'norm_post_ca': gain((L, D)),
        'norm_pre_ffn': gain((L, D)),
        'norm_post_ffn': gain((L, D)),
        'w_in': nrm((L, D, PROJ_TOTAL), D ** -0.5),
        'gla_w_gate2': nrm((L, GLA_GATE_RANK, GLA_HEADS * GLA_HEAD_K), GLA_GATE_RANK ** -0.5),
        'gla_b_gate': nrm((L, GLA_HEADS * GLA_HEAD_K), 0.1),
        'gla_norm': gain((L, GLA_HEAD_V)),
        'mla_norm_q': gain((L, MLA_Q_LORA)),
        'mla_norm_kv': gain((L, MLA_KV_LORA)),
        'mla_w_uq': nrm((L, MLA_Q_LORA, MLA_HEADS * (MLA_NOPE_DIM + MLA_ROPE_DIM)), MLA_Q_LORA ** -0.5),
        'mla_w_uk': nrm((L, MLA_HEADS, MLA_KV_LORA, MLA_NOPE_DIM), MLA_KV_LORA ** -0.5),
        'mla_w_uv': nrm((L, MLA_HEADS, MLA_KV_LORA, MLA_V_DIM), MLA_KV_LORA ** -0.5),
        'rwkv_mu': jax.random.uniform(next(ks), (L, RWKV_PROJ), f32),
        'rwkv_w0': nrm((L, BW), 2.0) - 2.0,
        'rwkv_w2': nrm((L, RWKV_DECAY_LORA, BW), 0.1),
        'rwkv_a0': nrm((L, BW), 0.5),
        'rwkv_a2': nrm((L, RWKV_A_LORA, BW), 0.5 * RWKV_A_LORA ** -0.5),
        'rwkv_g2': nrm((L, RWKV_GATE_LORA, BW), RWKV_GATE_LORA ** -0.5),
        'rwkv_k_k': 0.85 + nrm((L, BW), 0.05),
        'rwkv_k_a': 1.0 + nrm((L, BW), 0.05),
        'rwkv_r_k': nrm((L, RWKV_HEADS, RWKV_HEAD), 0.1),
        'rwkv_ln_w': gain((L, BW)),
        'rwkv_ln_b': nrm((L, BW), 0.01),
        'mlstm_i_bias': nrm((L, MLSTM_HEADS), 0.1),
        'mlstm_f_bias': 4.0 + nrm((L, MLSTM_HEADS), 0.5),
        'mlstm_norm': gain((L, MLSTM_HEADS * MLSTM_HEAD_V)),
        'w_branch': nrm((L, N_BRANCH, BW, D), BW ** -0.5),
        'w_out': nrm((L, D, D), D ** -0.5),
        'ca_norm_mem': gain((L, D)),
        'ca_w_q': nrm((L, D, D), D ** -0.5),
        'ca_w_k': nrm((L, D, D), D ** -0.5),
        'ca_w_v': nrm((L, D, D), D ** -0.5),
        'ca_w_o': nrm((L, D, D), D ** -0.5),
        'ffn_w_up': nrm((L, D, 2 * D_FF), D ** -0.5),
        'ffn_conv_w': nrm((L, CONV_W, D_FF), CONV_W ** -0.5),
        'ffn_conv_b': nrm((L, D_FF), 0.02),
        'ffn_w_down': nrm((L, D_FF, D), D_FF ** -0.5),
    }


def reference(x_prompt, x_sample, mem_prompt, cache_ckv, cache_kpe, page_table, cache_mem_k, cache_mem_v,
              state_gla, state_rwkv, state_rwkv_shift, state_mlstm_c, state_mlstm_n, state_mlstm_m, state_conv,
              norm_pre_mix, norm_post_mix, norm_pre_ca, norm_post_ca, norm_pre_ffn, norm_post_ffn,
              w_in, gla_w_gate2, gla_b_gate, gla_norm,
              mla_norm_q, mla_norm_kv, mla_w_uq, mla_w_uk, mla_w_uv,
              rwkv_mu, rwkv_w0, rwkv_w2, rwkv_a0, rwkv_a2, rwkv_g2, rwkv_k_k, rwkv_k_a, rwkv_r_k, rwkv_ln_w, rwkv_ln_b,
              mlstm_i_bias, mlstm_f_bias, mlstm_norm,
              w_branch, w_out,
              ca_norm_mem, ca_w_q, ca_w_k, ca_w_v, ca_w_o,
              ffn_w_up, ffn_conv_w, ffn_conv_b, ffn_w_down):
    dt = x_prompt.dtype
    b_p = x_prompt.shape[0]
    b_s = x_sample.shape[0]
    past_len = page_table.shape[1] * cache_ckv.shape[2]

    st_p = (jnp.zeros((b_p, GLA_HEADS, GLA_HEAD_K, GLA_HEAD_V), dt),
            jnp.zeros((b_p, RWKV_HEADS, RWKV_HEAD, RWKV_HEAD), dt),
            jnp.zeros((b_p, RWKV_PROJ), dt),
            jnp.zeros((b_p, MLSTM_HEADS, MLSTM_HEAD_QK, MLSTM_HEAD_V), dt),
            jnp.zeros((b_p, MLSTM_HEADS, MLSTM_HEAD_QK), dt),
            jnp.zeros((b_p, MLSTM_HEADS), dt),
            jnp.zeros((b_p, CONV_W - 1, D_FF), dt))

    xp, xs = x_prompt, x_sample
    new_p, new_s, mem_k_new, mem_v_new = [], [], [], []
    for l in range(DEPTH):
        lp = {
            'norm_pre_mix': norm_pre_mix[l], 'norm_post_mix': norm_post_mix[l],
            'norm_pre_ca': norm_pre_ca[l], 'norm_post_ca': norm_post_ca[l],
            'norm_pre_ffn': norm_pre_ffn[l], 'norm_post_ffn': norm_post_ffn[l],
            'w_in': w_in[l], 'gla_w_gate2': gla_w_gate2[l], 'gla_b_gate': gla_b_gate[l], 'gla_norm': gla_norm[l],
            'mla_norm_q': mla_norm_q[l], 'mla_norm_kv': mla_norm_kv[l], 'mla_w_uq': mla_w_uq[l],
            'mla_w_uk': mla_w_uk[l], 'mla_w_uv': mla_w_uv[l],
            'rwkv_mu': rwkv_mu[l], 'rwkv_w0': rwkv_w0[l], 'rwkv_w2': rwkv_w2[l], 'rwkv_a0': rwkv_a0[l],
            'rwkv_a2': rwkv_a2[l], 'rwkv_g2': rwkv_g2[l], 'rwkv_k_k': rwkv_k_k[l], 'rwkv_k_a': rwkv_k_a[l],
            'rwkv_r_k': rwkv_r_k[l], 'rwkv_ln_w': rwkv_ln_w[l], 'rwkv_ln_b': rwkv_ln_b[l],
            'mlstm_i_bias': mlstm_i_bias[l], 'mlstm_f_bias': mlstm_f_bias[l], 'mlstm_norm': mlstm_norm[l],
            'w_branch': w_branch[l], 'w_out': w_out[l], 'ca_w_q': ca_w_q[l], 'ca_w_o': ca_w_o[l],
            'ffn_w_up': ffn_w_up[l], 'ffn_conv_w': ffn_conv_w[l], 'ffn_conv_b': ffn_conv_b[l],
            'ffn_w_down': ffn_w_down[l],
        }
        mem_n = _rms_norm(mem_prompt, ca_norm_mem[l])
        mk = (mem_n @ ca_w_k[l]).reshape(b_p, -1, CA_HEADS, CA_HEAD_DIM)
        mv = (mem_n @ ca_w_v[l]).reshape(b_p, -1, CA_HEADS, CA_HEAD_DIM)
        xp, st_new_p = _layer(xp, lp, st_p, 0.0, None, mk, mv)
        new_p.append(st_new_p)
        mem_k_new.append(mk)
        mem_v_new.append(mv)

        ckv_past = cache_ckv[l][page_table].reshape(b_s, past_len, MLA_KV_LORA)
        kpe_past = cache_kpe[l][page_table].reshape(b_s, past_len, MLA_ROPE_DIM)
        st_s = (state_gla[l], state_rwkv[l], state_rwkv_shift[l], state_mlstm_c[l], state_mlstm_n[l],
                state_mlstm_m[l], state_conv[l])
        xs, st_new_s = _layer(xs, lp, st_s, float(past_len), (ckv_past, kpe_past), cache_mem_k[l], cache_mem_v[l])
        new_s.append(st_new_s)

    return (xp, xs,
            _stack_field(new_p, 0), _stack_field(new_p, 1), _stack_field(new_s, 0), _stack_field(new_s, 1),
            jnp.stack(mem_k_new, axis=0), jnp.stack(mem_v_new, axis=0),
            _stack_field(new_p, 2), _stack_field(new_s, 2),
            _stack_field(new_p, 3), _stack_field(new_s, 3),
            _stack_field(new_p, 4), _stack_field(new_s, 4),
            _stack_field(new_p, 5), _stack_field(new_s, 5),
            _stack_field(new_p, 6), _stack_field(new_s, 6),
            _stack_field(new_p, 7), _stack_field(new_s, 7),
            _stack_field(new_p, 8), _stack_field(new_s, 8))
```

```python
import functools
import math

import jax
import jax.numpy as jnp
from jax import lax
from jax.experimental import pallas as pl
from jax.experimental.pallas import tpu as pltpu

F32 = jnp.float32
BF16 = jnp.bfloat16

D_MODEL = 1024
BRANCH = 256
N_HEADS = 4
EPS = 1e-6
GLA_K = 32
GLA_TAU = 16.0
MLA_NOPE = 64
MLA_ROPE = 32
MLA_LORA = 128
ROPE_THETA = 10000.0
RWKV_LN_EPS = 64e-5
HEAD = 64
CA_DIM = 256
D_FF = 2816
FF_CHUNKS = 2

LANES = 128
SUBLANES = 8
VMEM_BUDGET = 56 * 1024 * 1024

W_GLA, W_MLA, W_RWKV, W_MLSTM, W_GATE = 896, 512, 1024, 1152, 4096
PIECES = (W_GLA, W_MLA, W_RWKV, W_MLSTM, W_GATE)


def _params(sem, vmem_bytes):
    return pltpu.CompilerParams(dimension_semantics=sem, vmem_limit_bytes=int(min(max(vmem_bytes, 16 << 20), VMEM_BUDGET)))


def _nbytes(shape, dtype=F32):
    return math.prod(shape) * jnp.dtype(dtype).itemsize


def _rms(x, g):
    return x * lax.rsqrt(jnp.mean(x * x, axis=-1, keepdims=True) + EPS) * g


def _bdot(a, b):
    return jnp.dot(a.astype(BF16), b.astype(BF16), preferred_element_type=F32)


def _bdot_t(a, b):
    return lax.dot_general(a.astype(BF16), b.astype(BF16), (((1,), (1,)), ((), ())), preferred_element_type=F32)


def _group_sum(x, ones_bd):
    hi = x.astype(BF16)
    lo = (x - hi.astype(F32)).astype(BF16)
    return jnp.dot(hi, ones_bd, preferred_element_type=F32) + jnp.dot(lo, ones_bd, preferred_element_type=F32)


def _softplus(x):
    return jnp.maximum(x, 0.0) + jnp.log1p(jnp.exp(-jnp.abs(x)))


def _log_sigmoid(x):
    return -_softplus(-x)


def _shift_rows(x, first, s):
    n = x.shape[0]
    if s % SUBLANES == 0:
        return jnp.concatenate([first, x[: n - s]], axis=0)
    rolled = pltpu.roll(x, s, 0)
    row = lax.broadcasted_iota(jnp.int32, x.shape, 0)
    for r in range(s):
        rolled = jnp.where(row == r, first[r : r + 1, :], rolled)
    return rolled


def _mix_in_kernel(x_ref, g_ref, w_ref, *outs):
    h = _rms(x_ref[...], g_ref[...]).astype(BF16)
    off = 0
    for o_ref, width in zip(outs, PIECES):
        o_ref[...] = jnp.dot(h, w_ref[:, off : off + width], preferred_element_type=F32)
        off += width


def _mix_in(x, g, w_all, tm):
    n = x.shape[0]
    total = sum(PIECES)
    vmem = 2 * _nbytes((tm, D_MODEL)) + 2 * _nbytes((D_MODEL, total), BF16) + 2 * _nbytes((tm, total)) + (4 << 20)
    return pl.pallas_call(
        _mix_in_kernel,
        grid=(n // tm,),
        in_specs=[
            pl.BlockSpec((tm, D_MODEL), lambda i: (i, 0)),
            pl.BlockSpec((1, D_MODEL), lambda i: (0, 0)),
            pl.BlockSpec((D_MODEL, total), lambda i: (0, 0)),
        ],
        out_specs=[pl.BlockSpec((tm, w), lambda i: (i, 0)) for w in PIECES],
        out_shape=[jax.ShapeDtypeStruct((n, w), F32) for w in PIECES],
        compiler_params=_params(("parallel",), vmem),
    )(x, g, w_all)


def _prep_kernel(pg_ref, pr_ref, pm_ref, shift_ref,
                 gate2_ref, bgate_ref, mu_ref, w0_ref, w2_ref, a0_ref, a2_ref, g2_ref, kk_ref, ka_ref, rk_ref,
                 ones_ref, ibias_ref, fbias_ref,
                 gla_ref, rw_ref, aux_ref, ml_ref, carry, *, stride):
    i = pl.program_id(1)
    tm = pr_ref.shape[0]

    @pl.when(i == 0)
    def _():
        carry[...] = shift_ref[0]

    pr = pr_ref[...]
    prev = _shift_rows(pr, carry[...], stride)
    carry[...] = pr[tm - stride :, :]
    log_w, kk, a, k_mod, rr, rv, g, bonus = _rwkv_mix(
        pr, prev, mu_ref[...], w0_ref[...], w2_ref[...], a0_ref[...], a2_ref[...], g2_ref[...], kk_ref[...],
        ka_ref[...], rk_ref[...], ones_ref[...])
    rw_ref[:, 0:256] = jnp.exp(log_w)
    rw_ref[:, 256:512] = kk
    rw_ref[:, 512:768] = kk * a
    rw_ref[:, 768:1024] = k_mod
    rw_ref[:, 1024:1280] = rr
    rw_ref[:, 1280:1536] = rv
    aux_ref[:, 0:256] = g
    aux_ref[:, 256:512] = bonus

    pg = pg_ref[...]
    z = _bdot(pg[:, 768:896], gate2_ref[...]) + bgate_ref[...]
    gla_ref[:, 0:128] = pg[:, 0:128] * (GLA_K ** -0.5)
    gla_ref[:, 128:256] = jnp.exp(_log_sigmoid(z) / GLA_TAU)

    pml = pm_ref[...]
    ifp = pml[:, 1024:1152]
    lane = lax.broadcasted_iota(jnp.int32, ifp.shape, 1)
    ml_ref[:, 0:256] = pml[:, 0:256] * (HEAD ** -0.5)
    ml_ref[:, 256:384] = jnp.where(lane < N_HEADS, ifp + ibias_ref[...], _log_sigmoid(ifp + fbias_ref[...]))


def _prep(p_gla, p_rwkv, p_mlstm, shift0, wts, n_seq, tm, stride):
    n = p_rwkv.shape[0]
    n_tiles = n // n_seq // tm
    row = lambda w: pl.BlockSpec((tm, w), lambda q, i: (q * n_tiles + i, 0))
    const = lambda a: pl.BlockSpec(a.shape, lambda q, i: (0,) * a.ndim)
    out_w = (256, 1536, 512, 384)
    vmem = 2 * _nbytes((tm, W_GLA + W_RWKV + W_MLSTM + sum(out_w))) + 8 * _nbytes((tm, W_RWKV)) + (6 << 20)
    return pl.pallas_call(
        functools.partial(_prep_kernel, stride=stride),
        grid=(n_seq, n_tiles),
        in_specs=[row(W_GLA), row(W_RWKV), row(W_MLSTM),
                  pl.BlockSpec((1, stride, W_RWKV), lambda q, i: (q, 0, 0))] + [const(a) for a in wts],
        out_specs=[row(w) for w in out_w],
        out_shape=[jax.ShapeDtypeStruct((n, w), F32) for w in out_w],
        scratch_shapes=[pltpu.VMEM((stride, W_RWKV), F32)],
        compiler_params=_params(("arbitrary", "arbitrary"), vmem),
    )(p_gla, p_rwkv, p_mlstm, shift0, *wts)


def _rope32(x, cos2, sin2):
    half = MLA_ROPE // 2
    swapped = jnp.concatenate([x[:, half:], x[:, :half]], axis=1)
    return x * cos2 + swapped * sin2


def _mla_prep_kernel(p_ref, cos_ref, sin_ref, nq_ref, nkv_ref, wn_ref, wp_ref, wuk_ref,
                     qlat_ref, qpe_ref, ckv_ref, kpe_ref):
    p = p_ref[...]
    cos2, sin2 = cos_ref[...], sin_ref[...]
    cq = _rms(p[:, 0:256], nq_ref[...])
    q_nope = _bdot(cq, wn_ref[...])
    q_pe = _bdot(cq, wp_ref[...])
    qlat_ref[...] = _bdot(q_nope, wuk_ref[...])
    qpe_ref[...] = jnp.concatenate(
        [_rope32(q_pe[:, h * MLA_ROPE : (h + 1) * MLA_ROPE], cos2, sin2) for h in range(N_HEADS)], axis=1)
    ckv_ref[...] = _rms(p[:, 256:384], nkv_ref[...])
    kpe_ref[...] = _rope32(p[:, 384:416], cos2, sin2)


def _mla_prep(p_mla, cos2, sin2, wts, n_seq, tm):
    n = p_mla.shape[0]
    n_tiles = n // n_seq // tm
    row = lambda w: pl.BlockSpec((tm, w), lambda q, i: (q * n_tiles + i, 0))
    tab = pl.BlockSpec((tm, MLA_ROPE), lambda q, i: (i, 0))
    const = lambda a: pl.BlockSpec(a.shape, lambda q, i: (0,) * a.ndim)
    out_w = (N_HEADS * MLA_LORA, N_HEADS * MLA_ROPE, MLA_LORA, MLA_ROPE)
    vmem = 2 * _nbytes((tm, W_MLA + sum(out_w) + 2 * LANES)) + 6 * _nbytes((tm, 512)) + (4 << 20)
    return pl.pallas_call(
        _mla_prep_kernel,
        grid=(n_seq, n_tiles),
        in_specs=[row(W_MLA), tab, tab] + [const(a) for a in wts],
        out_specs=[row(w) for w in out_w],
        out_shape=[jax.ShapeDtypeStruct((n, w), F32) for w in out_w],
        compiler_params=_params(("parallel", "parallel"), vmem),
    )(p_mla, cos2, sin2, *wts)


MLA_SCALE = (MLA_NOPE + MLA_ROPE) ** -0.5


def _stack_heads(x, width):
    return jnp.concatenate([x[:, h * width : (h + 1) * width] for h in range(N_HEADS)], axis=0)


def _mla_prompt_kernel(ql_ref, qp_ref, ckv_ref, kpe_ref, o_ref, *, tq, tk):
    qi = pl.program_id(1)
    q = _stack_heads(ql_ref[...], MLA_LORA).astype(BF16)
    qp = _stack_heads(qp_ref[...], MLA_ROPE).astype(BF16)
    rows = N_HEADS * tq
    tpos = lax.broadcasted_iota(jnp.int32, (tq, tk), 0)
    qpos = qi * tq + jnp.concatenate([tpos] * N_HEADS, axis=0)
    kidx = lax.broadcasted_iota(jnp.int32, (rows, tk), 1)

    def body(kb, carry):
        m, l, acc = carry
        k0 = pl.multiple_of(kb * tk, tk)
        kc = ckv_ref[pl.ds(k0, tk), :].astype(BF16)
        kp = kpe_ref[pl.ds(k0, tk), :].astype(BF16)
        s = (_bdot_t(q, kc) + _bdot_t(qp, kp)) * MLA_SCALE
        s = jnp.where(k0 + kidx <= qpos, s, -jnp.inf)
        m_new = jnp.maximum(m, jnp.max(s, axis=1, keepdims=True))
        p = jnp.exp(s - m_new)
        alpha = jnp.exp(m - m_new)
        l = alpha * l + jnp.sum(p, axis=1, keepdims=True)
        acc = alpha * acc + jnp.dot(p.astype(BF16), kc, preferred_element_type=F32)
        return m_new, l, acc

    init = (jnp.full((rows, 1), -jnp.inf, F32), jnp.zeros((rows, 1), F32), jnp.zeros((rows, MLA_LORA), F32))
    n_kb = qi + 1 if tq == tk else (qi * tq + tq + tk - 1) // tk
    _, l, acc = lax.fori_loop(0, n_kb, body, init)
    out = acc / l
    o_ref[...] = jnp.concatenate([out[h * tq : (h + 1) * tq, :] for h in range(N_HEADS)], axis=1)


def _mla_prompt(q_lat, q_pe, ckv, kpe, n_b, t_len):
    tq = tk = min(256, t_len)
    nq = t_len // tq
    vmem = 2 * _nbytes((tq, 1024 + 128)) + 2 * _nbytes((t_len, 256)) + 12 * _nbytes((N_HEADS * tq, tk)) + (4 << 20)
    return pl.pallas_call(
        functools.partial(_mla_prompt_kernel, tq=tq, tk=tk),
        grid=(n_b, nq),
        in_specs=[
            pl.BlockSpec((tq, N_HEADS * MLA_LORA), lambda b, i: (b * nq + i, 0)),
            pl.BlockSpec((tq, N_HEADS * MLA_ROPE), lambda b, i: (b * nq + i, 0)),
            pl.BlockSpec((t_len, MLA_LORA), lambda b, i: (b, 0)),
            pl.BlockSpec((t_len, MLA_ROPE), lambda b, i: (b, 0)),
        ],
        out_specs=pl.BlockSpec((tq, N_HEADS * MLA_LORA), lambda b, i: (b * nq + i, 0)),
        out_shape=jax.ShapeDtypeStruct((n_b * t_len, N_HEADS * MLA_LORA), F32),
        compiler_params=_params(("parallel", "parallel"), vmem),
    )(q_lat, q_pe, ckv, kpe)


def _mla_sample_kernel(pt_ref, ql_ref, qp_ref, cnew_ref, pnew_ref, *rest, n_pg, page):
    ckv_refs = rest[:n_pg]
    kpe_refs = rest[n_pg : 2 * n_pg]
    o_ref = rest[2 * n_pg]
    m_s, l_s, acc_s = rest[2 * n_pg + 1 :]
    g = pl.program_id(1)
    rows = ql_ref.shape[1]

    @pl.when(g == 0)
    def _():
        m_s[...] = jnp.full(m_s.shape, -jnp.inf, F32)
        l_s[...] = jnp.zeros(l_s.shape, F32)
        acc_s[...] = jnp.zeros(acc_s.shape, F32)

    q = ql_ref[0].astype(BF16)
    qp = qp_ref[0].astype(BF16)

    def update(scores, values):
        m = m_s[...]
        m_new = m
        for s in scores:
            m_new = jnp.maximum(m_new, jnp.max(s, axis=1, keepdims=True))
        alpha = jnp.exp(m - m_new)
        l = alpha * l_s[...]
        acc = alpha * acc_s[...]
        for s, v in zip(scores, values):
            p = jnp.exp(s - m_new)
            l = l + jnp.sum(p, axis=1, keepdims=True)
            acc = acc + jnp.dot(p.astype(BF16), v, preferred_element_type=F32)
        m_s[...] = m_new
        l_s[...] = l
        acc_s[...] = acc

    scores, values = [], []
    for c_ref, p_ref in zip(ckv_refs, kpe_refs):
        kc = c_ref[0, 0].astype(BF16)
        kp_t = p_ref[0, 0].astype(BF16)
        scores.append((_bdot_t(q, kc) + jnp.dot(qp, kp_t, preferred_element_type=F32)) * MLA_SCALE)
        values.append(kc)
    update(scores, values)

    @pl.when(g == pl.num_programs(1) - 1)
    def _():
        kc = cnew_ref[0].astype(BF16)
        kp = pnew_ref[0].astype(BF16)
        s = (_bdot_t(q, kc) + _bdot_t(qp, kp)) * MLA_SCALE
        row = lax.broadcasted_iota(jnp.int32, (rows, page), 0)
        tk = lax.broadcasted_iota(jnp.int32, (rows, page), 1)
        s = jnp.where(tk * N_HEADS <= row, s, -jnp.inf)
        update([s], [kc])
        o_ref[0] = acc_s[...] / l_s[...]


def _mla_sample(page_table, q_lat, q_pe, ckv_new, kpe_new, cache_ckv, cache_kpe_t, layer):
    n_b, n_pages = page_table.shape
    page = cache_ckv.shape[2]
    rows = q_lat.shape[1]
    n_pg = math.gcd(n_pages, 16)
    n_groups = n_pages // n_pg
    pt_flat = page_table.reshape(-1)

    def page_spec(shape, p):
        return pl.BlockSpec((1, 1) + shape, lambda b, g, pt: (layer, pt[b * n_pages + g * n_pg + p], 0, 0))

    per_b = lambda shape: pl.BlockSpec((1,) + shape, lambda b, g, pt: (b, 0, 0))
    grid_spec = pltpu.PrefetchScalarGridSpec(
        num_scalar_prefetch=1,
        grid=(n_b, n_groups),
        in_specs=[per_b((rows, MLA_LORA)), per_b((rows, MLA_ROPE)), per_b((page, MLA_LORA)), per_b((page, MLA_ROPE))]
        + [page_spec((page, MLA_LORA), p) for p in range(n_pg)] + [page_spec((MLA_ROPE, page), p) for p in range(n_pg)],
        out_specs=per_b((rows, MLA_LORA)),
        scratch_shapes=[pltpu.VMEM((rows, 1), F32), pltpu.VMEM((rows, 1), F32), pltpu.VMEM((rows, MLA_LORA), F32)],
    )
    vmem = 2 * n_pg * 2 * _nbytes((page, LANES)) + 4 * _nbytes((page, 2 * LANES)) + (8 << 20)
    return pl.pallas_call(
        functools.partial(_mla_sample_kernel, n_pg=n_pg, page=page),
        grid_spec=grid_spec,
        out_shape=jax.ShapeDtypeStruct((n_b, rows, MLA_LORA), F32),
        compiler_params=_params(("parallel", "arbitrary"), vmem),
    )(pt_flat, q_lat, q_pe, ckv_new, kpe_new, *([cache_ckv] * n_pg), *([cache_kpe_t] * n_pg))


V_TILES = HEAD // SUBLANES


def _scan_kernel(gk_ref, rk_ref, mk_ref, v_ref, sg0, sr0, sc0, sn0, sm0,
                 o_ref, sg_o, sr_o, sc_o, sn_o, sm_o, sg, sr, sc, sn, sm, *, tb, ksg, ks, groups):
    i = pl.program_id(1)

    @pl.when(i == 0)
    def _():
        sg[...] = sg0[...]
        sr[...] = sr0[...]
        sc[...] = sc0[...]
        sn[...] = sn0[...]
        sm[...] = sm0[...]

    def fold(x):
        return x + pltpu.roll(x, LANES // 2, 1) if groups == 2 else x

    def vt_slice(vt):
        return slice(vt * SUBLANES, (vt + 1) * SUBLANES)

    def rwkv_sa(t):
        acc = [jnp.zeros((SUBLANES, LANES), F32)] * V_TILES
        for k in range(ks):
            kk = rk_ref[t, ks + k : ks + k + 1, :]
            for vt in range(V_TILES):
                acc[vt] = acc[vt] + sr[k, vt_slice(vt), :] * kk
        return tuple(fold(a) for a in acc)

    def step(t, sa):
        tn = jnp.minimum(t + 1, tb - 1)
        zeros = [jnp.zeros((SUBLANES, LANES), F32)] * V_TILES

        gv = [v_ref[t, vt * SUBLANES : (vt + 1) * SUBLANES, :] for vt in range(V_TILES)]
        o = list(zeros)
        for k in range(ksg):
            q = gk_ref[t, k : k + 1, :]
            kr = gk_ref[t, ksg + k : ksg + k + 1, :]
            a = gk_ref[t, 2 * ksg + k : 2 * ksg + k + 1, :]
            for vt in range(V_TILES):
                new = sg[k, vt_slice(vt), :] * a + gv[vt] * kr
                sg[k, vt_slice(vt), :] = new
                o[vt] = o[vt] + new * q
        for vt in range(V_TILES):
            o_ref[t, vt_slice(vt), :] = fold(o[vt])

        rv = [v_ref[t, HEAD + vt * SUBLANES : HEAD + (vt + 1) * SUBLANES, :] for vt in range(V_TILES)]
        y = list(zeros)
        san = list(zeros)
        for k in range(ks):
            w = rk_ref[t, k : k + 1, :]
            b = rk_ref[t, 2 * ks + k : 2 * ks + k + 1, :]
            kr = rk_ref[t, 3 * ks + k : 3 * ks + k + 1, :]
            r = rk_ref[t, 4 * ks + k : 4 * ks + k + 1, :]
            kkn = rk_ref[tn, ks + k : ks + k + 1, :]
            for vt in range(V_TILES):
                new = sr[k, vt_slice(vt), :] * w - sa[vt] * b + rv[vt] * kr
                sr[k, vt_slice(vt), :] = new
                y[vt] = y[vt] + new * r
                san[vt] = san[vt] + new * kkn
        for vt in range(V_TILES):
            o_ref[t, HEAD + vt * SUBLANES : HEAD + (vt + 1) * SUBLANES, :] = fold(y[vt])

        mv = [v_ref[t, 2 * HEAD + vt * SUBLANES : 2 * HEAD + (vt + 1) * SUBLANES, :] for vt in range(V_TILES)]
        i_t = v_ref[t, 3 * HEAD : 3 * HEAD + 1, :]
        lf = v_ref[t, 3 * HEAD + 1 : 3 * HEAD + 2, :]
        m_old = sm[0:1, :]
        m_new = jnp.maximum(lf + m_old, i_t)
        fs = jnp.exp(lf + m_old - m_new)
        isc = jnp.exp(i_t - m_new)
        sm[0:1, :] = m_new
        q_tile = mk_ref[t, 0:ks, :]
        n_new = fs * sn[...] + isc * mk_ref[t, ks : 2 * ks, :]
        sn[...] = n_new
        den = fold(jnp.sum(n_new * q_tile, axis=0, keepdims=True))
        fs_b = jnp.broadcast_to(fs, (SUBLANES, LANES))
        num = list(zeros)
        for k in range(ks):
            q = mk_ref[t, k : k + 1, :]
            ik = isc * mk_ref[t, ks + k : ks + k + 1, :]
            for vt in range(V_TILES):
                new = sc[k, vt_slice(vt), :] * fs_b + mv[vt] * ik
                sc[k, vt_slice(vt), :] = new
                num[vt] = num[vt] + new * q
        scale = 1.0 / jnp.maximum(jnp.abs(den), jnp.exp(-m_new))
        for vt in range(V_TILES):
            o_ref[t, 2 * HEAD + vt * SUBLANES : 2 * HEAD + (vt + 1) * SUBLANES, :] = fold(num[vt]) * scale
        return tuple(fold(x) for x in san)

    lax.fori_loop(0, tb, step, rwkv_sa(0))

    @pl.when(i == pl.num_programs(1) - 1)
    def _():
        sg_o[...] = sg[...]
        sr_o[...] = sr[...]
        sc_o[...] = sc[...]
        sn_o[...] = sn[...]
        sm_o[...] = sm[...]


def _scan(gk, rk, mk, vops, states, t_len, groups):
    ks = HEAD // groups
    ksg = GLA_K // groups
    n_lanes = gk.shape[2]
    n_lt = n_lanes // LANES
    tb = min(32, t_len)
    op = lambda rows: pl.BlockSpec((tb, rows, LANES), lambda j, i: (i, 0, j))
    st3 = lambda k: pl.BlockSpec((k, HEAD, LANES), lambda j, i: (0, 0, j))
    st2 = lambda k: pl.BlockSpec((k, LANES), lambda j, i: (0, j))
    v_rows = vops.shape[1]
    st_specs = [st3(ksg), st3(ks), st3(ks), st2(ks), st2(SUBLANES)]
    st_shapes = [(ksg, HEAD, n_lanes), (ks, HEAD, n_lanes), (ks, HEAD, n_lanes), (ks, n_lanes), (SUBLANES, n_lanes)]
    blk = _nbytes((tb, 3 * ksg + 5 * ks + 2 * ks + v_rows + 3 * HEAD, LANES))
    st_bytes = _nbytes(((ksg + 2 * ks) * HEAD + ks + SUBLANES, LANES))
    return pl.pallas_call(
        functools.partial(_scan_kernel, tb=tb, ksg=ksg, ks=ks, groups=groups),
        grid=(n_lt, t_len // tb),
        in_specs=[op(3 * ksg), op(5 * ks), op(2 * ks), op(v_rows)] + st_specs,
        out_specs=[op(3 * HEAD)] + st_specs,
        out_shape=[jax.ShapeDtypeStruct((t_len, 3 * HEAD, n_lanes), F32)]
        + [jax.ShapeDtypeStruct(s, F32) for s in st_shapes],
        scratch_shapes=[pltpu.VMEM((ksg, HEAD, LANES), F32), pltpu.VMEM((ks, HEAD, LANES), F32),
                        pltpu.VMEM((ks, HEAD, LANES), F32), pltpu.VMEM((ks, LANES), F32),
                        pltpu.VMEM((SUBLANES, LANES), F32)],
        compiler_params=_params(("parallel", "arbitrary"), 2 * blk + 5 * st_bytes + (4 << 20)),
    )(gk, rk, mk, vops, *states)


def _head_ln(x, ones_bd, eps):
    mean = _group_sum(x, ones_bd) * (1.0 / HEAD)
    xc = x - mean
    var = _group_sum(xc * xc, ones_bd) * (1.0 / HEAD)
    return xc * lax.rsqrt(var + eps)


def _gla_out(o, gate_r, ones_bd, gnorm):
    ms = _group_sum(o * o, ones_bd) * (1.0 / HEAD)
    return o * lax.rsqrt(ms + EPS) * gnorm * jax.nn.silu(gate_r)


def _rwkv_out(y, bonus, g, ones_bd, lnw, lnb):
    return (_head_ln(y, ones_bd, RWKV_LN_EPS) * lnw + lnb + bonus) * g


def _mlstm_out(h, gate_o, ones_bd, mnorm):
    return _head_ln(h, ones_bd, EPS) * mnorm * jax.nn.sigmoid(gate_o)


def _post_kernel(scan_ref, pg_ref, aux_ref, pm_ref, ones_ref, gnorm_ref, lnw_ref, lnb_ref, mnorm_ref,
                 og_ref, or_ref, om_ref):
    ones_bd = ones_ref[...]
    sc = scan_ref[...]
    aux = aux_ref[...]
    og_ref[...] = _gla_out(sc[:, 0:256], pg_ref[:, 512:768], ones_bd, gnorm_ref[...])
    or_ref[...] = _rwkv_out(sc[:, 256:512], aux[:, 256:512], aux[:, 0:256], ones_bd, lnw_ref[...], lnb_ref[...])
    om_ref[...] = _mlstm_out(sc[:, 512:768], pm_ref[:, 768:1024], ones_bd, mnorm_ref[...])


def _post(scan_rows, p_gla, aux, p_mlstm, wts, tm):
    n = scan_rows.shape[0]
    row = lambda w: pl.BlockSpec((tm, w), lambda i: (i, 0))
    const = lambda a: pl.BlockSpec(a.shape, lambda i: (0,) * a.ndim)
    widths = (768, W_GLA, 512, W_MLSTM)
    vmem = 2 * _nbytes((tm, sum(widths) + 768)) + 12 * _nbytes((tm, BRANCH)) + (4 << 20)
    return pl.pallas_call(
        _post_kernel,
        grid=(n // tm,),
        in_specs=[row(w) for w in widths] + [const(a) for a in wts],
        out_specs=[row(BRANCH)] * 3,
        out_shape=[jax.ShapeDtypeStruct((n, BRANCH), F32)] * 3,
        compiler_params=_params(("parallel",), vmem),
    )(scan_rows, p_gla, aux, p_mlstm, *wts)


def _merge_kernel(x_ref, og_ref, olat_ref, or_ref, om_ref, gate_ref, wuv_ref, wbr_ref, wout_ref, gpost_ref, o_ref):
    o_mla = _bdot(olat_ref[...], wuv_ref[...])
    mixed = jnp.zeros((x_ref.shape[0], D_MODEL), F32)
    for n, br in enumerate((og_ref[...], o_mla, or_ref[...], om_ref[...])):
        up = _bdot(br, wbr_ref[n])
        mixed = mixed + jax.nn.sigmoid(gate_ref[:, n * D_MODEL : (n + 1) * D_MODEL]) * up
    out = _bdot(mixed, wout_ref[...])
    o_ref[...] = x_ref[...] + _rms(out, gpost_ref[...])


def _merge(x, o_gla, o_lat, o_rwkv, o_mlstm, p_gate, wts, tm):
    n = x.shape[0]
    row = lambda w: pl.BlockSpec((tm, w), lambda i: (i, 0))
    const = lambda a: pl.BlockSpec(a.shape, lambda i: (0,) * a.ndim)
    widths = (D_MODEL, BRANCH, 512, BRANCH, BRANCH, W_GATE)
    vmem = 2 * _nbytes((tm, sum(widths) + D_MODEL)) + 10 * _nbytes((tm, D_MODEL)) + (12 << 20)
    return pl.pallas_call(
        _merge_kernel,
        grid=(n // tm,),
        in_specs=[row(w) for w in widths] + [const(a) for a in wts],
        out_specs=row(D_MODEL),
        out_shape=jax.ShapeDtypeStruct((n, D_MODEL), F32),
        compiler_params=_params(("parallel",), vmem),
    )(x, o_gla, o_lat, o_rwkv, o_mlstm, p_gate, *wts)


GLA_CHUNK = 32
MIX_CHUNK = 64


def _bdot_tn(a, b):
    return lax.dot_general(a.astype(BF16), b.astype(BF16), (((0,), (0,)), ((), ())), preferred_element_type=F32)


def _cumsum_rows(tri, x):
    hi = x.astype(BF16)
    lo = (x - hi.astype(F32)).astype(BF16)
    return jnp.dot(tri, hi, preferred_element_type=F32) + jnp.dot(tri, lo, preferred_element_type=F32)


def _rep_rows(x, mask):
    return jnp.concatenate([x] * N_HEADS, axis=0) * mask


def _gla_chunk_kernel(pg_ref, gate2_ref, bgate_ref, gnorm_ref, ones_ref, tri_ref, causal_ref, kmask_ref, vmask_ref,
                      smask_ref, o_ref, st_ref, st, *, tc):
    i = pl.program_id(1)

    @pl.when(i == 0)
    def _():
        st[...] = jnp.zeros(st.shape, F32)

    L = GLA_CHUNK
    pg = pg_ref[...]
    log_a = _log_sigmoid(_bdot(pg[:, 768:896], gate2_ref[...]) + bgate_ref[...]) / GLA_TAU
    tri, causal = tri_ref[...], causal_ref[...]
    kmask, vmask, smask = kmask_ref[...], vmask_ref[...], smask_ref[...]
    outs = []
    for c in range(tc // L):
        rows = slice(c * L, (c + 1) * L)
        b = _cumsum_rows(tri, log_a[rows])
        b_last = b[L - 1 : L, :]
        q_dec = pg[rows, 0:128] * (GLA_K ** -0.5) * jnp.exp(b)
        k = pg[rows, 128:256]
        v = pg[rows, 256:512]
        k_rep = _rep_rows(k * jnp.exp(-b), kmask)
        v_rep = _rep_rows(v, vmask)
        att = _bdot_t(q_dec, k_rep) * causal
        s_t = st[...]
        outs.append(_bdot_t(q_dec, s_t) + _bdot(att, v_rep))
        st[...] = s_t * jnp.exp(b_last) + _bdot_tn(v, k * jnp.exp(b_last - b)) * smask
    o = jnp.concatenate(outs, axis=0)
    o_ref[...] = _gla_out(o, pg[:, 512:768], ones_ref[...], gnorm_ref[...])

    @pl.when(i == pl.num_programs(1) - 1)
    def _():
        st_ref[0] = st[...]


def _gla_chunk(p_gla, wts, n_b, t_len):
    tc = min(256, t_len)
    nt = t_len // tc
    const = lambda a: pl.BlockSpec(a.shape, lambda b, i: (0,) * a.ndim)
    vmem = 2 * _nbytes((tc, W_GLA + BRANCH)) + 12 * _nbytes((tc, BRANCH)) + (6 << 20)
    return pl.pallas_call(
        functools.partial(_gla_chunk_kernel, tc=tc),
        grid=(n_b, nt),
        in_specs=[pl.BlockSpec((tc, W_GLA), lambda b, i: (b * nt + i, 0))] + [const(a) for a in wts],
        out_specs=[pl.BlockSpec((tc, BRANCH), lambda b, i: (b * nt + i, 0)),
                   pl.BlockSpec((1, BRANCH, N_HEADS * GLA_K), lambda b, i: (b, 0, 0))],
        out_shape=[jax.ShapeDtypeStruct((n_b * t_len, BRANCH), F32),
                   jax.ShapeDtypeStruct((n_b, BRANCH, N_HEADS * GLA_K), F32)],
        scratch_shapes=[pltpu.VMEM((BRANCH, N_HEADS * GLA_K), F32)],
        compiler_params=_params(("parallel", "arbitrary"), vmem),
    )(p_gla, *wts)


def _mlstm_chunk_kernel(pm_ref, ibias_ref, fbias_ref, mnorm_ref, ones_ref, tri_ref,
                        o_ref, c_out, n_out, m_out, cs, ns, ms, *, tc):
    i = pl.program_id(1)

    @pl.when(i == 0)
    def _():
        cs[...] = jnp.zeros(cs.shape, F32)
        ns[...] = jnp.zeros(ns.shape, F32)
        ms[...] = jnp.zeros(ms.shape, F32)

    L = MIX_CHUNK
    pm = pm_ref[...]
    ifp = pm[:, 1024:1152]
    lane = lax.broadcasted_iota(jnp.int32, ifp.shape, 1)
    gates = jnp.where(lane < N_HEADS, ifp + ibias_ref[...], _log_sigmoid(ifp + fbias_ref[...]))
    tri = tri_ref[...]
    causal = lax.broadcasted_iota(jnp.int32, (L, L), 1) <= lax.broadcasted_iota(jnp.int32, (L, L), 0)
    lane_c = lax.broadcasted_iota(jnp.int32, (L, LANES), 1)
    row_t = lax.broadcasted_iota(jnp.int32, (LANES, L), 0)
    last_lane = lax.broadcasted_iota(jnp.int32, (1, L), 1) == L - 1

    def pick_col(x, j):
        return jnp.sum(jnp.where(lane_c == j, x, 0.0), axis=1, keepdims=True)

    def pick_row(x_t, j):
        return jnp.sum(jnp.where(row_t == j, x_t, 0.0), axis=0, keepdims=True)

    chunks = []
    for c in range(tc // L):
        rows = slice(c * L, (c + 1) * L)
        g_c = gates[rows]
        z = jnp.where(lane_c < N_HEADS, g_c, _cumsum_rows(tri, g_c))
        z_t = z.T
        heads = []
        for h in range(N_HEADS):
            cols = slice(h * HEAD, (h + 1) * HEAD)
            q = pm[rows, cols] * (HEAD ** -0.5)
            k = pm[rows, 256 + h * HEAD : 256 + (h + 1) * HEAD]
            v = pm[rows, 512 + h * HEAD : 512 + (h + 1) * HEAD]
            i_col, b_col = pick_col(z, h), pick_col(z, N_HEADS + h)
            i_row, b_row = pick_row(z_t, h), pick_row(z_t, N_HEADS + h)
            m0 = jnp.max(ms[h : h + 1, :], axis=1, keepdims=True)
            d = jnp.where(causal, b_col - b_row + i_row, -jnp.inf)
            inter = b_col + m0
            m_t = jnp.maximum(inter, jnp.max(d, axis=1, keepdims=True))
            w_inter = jnp.exp(inter - m_t)
            w_intra = jnp.exp(d - m_t) * _bdot_t(q, k)
            c_h, n_h = cs[h], ns[h : h + 1, :]
            num = w_inter * _bdot(q, c_h) + _bdot(w_intra, v)
            den = w_inter * jnp.sum(q * n_h, axis=1, keepdims=True) + jnp.sum(w_intra, axis=1, keepdims=True)
            heads.append(num / jnp.maximum(jnp.abs(den), jnp.exp(-m_t)))
            b_last = jnp.sum(jnp.where(last_lane, b_row, 0.0), axis=1, keepdims=True)
            m_new = jnp.maximum(b_last + m0, jnp.max(b_last - b_row + i_row, axis=1, keepdims=True))
            scale_old = jnp.exp(b_last + m0 - m_new)
            kw = k * jnp.exp(b_last - b_col + i_col - m_new)
            cs[h] = scale_old * c_h + _bdot_tn(kw, v)
            ns[h : h + 1, :] = scale_old * n_h + jnp.sum(kw, axis=0, keepdims=True)
            ms[h : h + 1, :] = jnp.broadcast_to(m_new, (1, LANES))
        chunks.append(jnp.concatenate(heads, axis=1))
    hm = jnp.concatenate(chunks, axis=0)
    o_ref[...] = _mlstm_out(hm, pm[:, 768:1024], ones_ref[...], mnorm_ref[...])

    @pl.when(i == pl.num_programs(1) - 1)
    def _():
        c_out[0] = cs[...]
        n_out[0] = ns[...]
        m_out[0] = ms[...]


def _mlstm_chunk(p_mlstm, wts, n_b, t_len):
    tc = min(256, t_len)
    nt = t_len // tc
    const = lambda a: pl.BlockSpec(a.shape, lambda b, i: (0,) * a.ndim)
    vmem = 2 * _nbytes((tc, W_MLSTM + BRANCH)) + 12 * _nbytes((tc, BRANCH)) + (6 << 20)
    return pl.pallas_call(
        functools.partial(_mlstm_chunk_kernel, tc=tc),
        grid=(n_b, nt),
        in_specs=[pl.BlockSpec((tc, W_MLSTM), lambda b, i: (b * nt + i, 0))] + [const(a) for a in wts],
        out_specs=[pl.BlockSpec((tc, BRANCH), lambda b, i: (b * nt + i, 0)),
                   pl.BlockSpec((1, N_HEADS, HEAD, HEAD), lambda b, i: (b, 0, 0, 0)),
                   pl.BlockSpec((1, SUBLANES, HEAD), lambda b, i: (b, 0, 0)),
                   pl.BlockSpec((1, SUBLANES, LANES), lambda b, i: (b, 0, 0))],
        out_shape=[jax.ShapeDtypeStruct((n_b * t_len, BRANCH), F32),
                   jax.ShapeDtypeStruct((n_b, N_HEADS, HEAD, HEAD), F32),
                   jax.ShapeDtypeStruct((n_b, SUBLANES, HEAD), F32),
                   jax.ShapeDtypeStruct((n_b, SUBLANES, LANES), F32)],
        scratch_shapes=[pltpu.VMEM((N_HEADS, HEAD, HEAD), F32), pltpu.VMEM((SUBLANES, HEAD), F32),
                        pltpu.VMEM((SUBLANES, LANES), F32)],
        compiler_params=_params(("parallel", "arbitrary"), vmem),
    )(p_mlstm, *wts)


def _rwkv_mix(pr, prev, mu, w0, w2, a0, a2, g2, kkw, ka, rkw, ones_bd):
    pm = pr + (prev - pr) * mu
    rr, rk, rv = pm[:, 0:256], pm[:, 256:512], pm[:, 512:768]
    wlo, alo, rglo = pm[:, 768:832], pm[:, 832:896], pm[:, 896:1024]
    log_w = -jnp.exp(-_softplus(-(w0 + _bdot(jnp.tanh(wlo), w2))) - 0.5)
    a = jax.nn.sigmoid(a0 + _bdot(alo, a2))
    g = _bdot(jax.nn.sigmoid(rglo), g2)
    kkr = rk * kkw
    kk = kkr / jnp.maximum(jnp.sqrt(_group_sum(kkr * kkr, ones_bd)), 1e-12)
    k_mod = rk * (1.0 + (a - 1.0) * ka)
    bonus = _group_sum(rr * k_mod * rkw, ones_bd) * rv
    return log_w, kk, a, k_mod, rr, rv, g, bonus


def _rwkv_chunk_kernel(pr_ref, mu_ref, w0_ref, w2_ref, a0_ref, a2_ref, g2_ref, kk_ref, ka_ref, rk_ref, ones_ref,
                       lnw_ref, lnb_ref, tri_ref, strict_ref, incl_ref, eye_ref, o_ref, s_out, carry, sv, *, tc):
    i = pl.program_id(1)

    @pl.when(i == 0)
    def _():
        carry[...] = jnp.zeros(carry.shape, F32)
        sv[...] = jnp.zeros(sv.shape, F32)

    L = MIX_CHUNK
    pr = pr_ref[...]
    prev = _shift_rows(pr, carry[...], 1)
    carry[...] = pr[tc - 1 :, :]
    ones_bd = ones_ref[...]
    log_w, kk, a, k_mod, rr, rv, g, bonus = _rwkv_mix(
        pr, prev, mu_ref[...], w0_ref[...], w2_ref[...], a0_ref[...], a2_ref[...], g2_ref[...], kk_ref[...],
        ka_ref[...], rk_ref[...], ones_bd)
    bd = ones_bd.astype(F32)
    tri, strict, incl, eye = tri_ref[...], strict_ref[...], incl_ref[...], eye_ref[...]
    outs = []
    for c in range(tc // L):
        rows = slice(c * L, (c + 1) * L)
        lw = log_w[rows]
        cum = _cumsum_rows(tri, lw)
        end = jnp.exp(cum[L - 1 : L, :] - cum)
        w_inc, w_inv = jnp.exp(cum), jnp.exp(-cum)
        kk_c, v_c = kk[rows], rv[rows]
        ka_c = kk_c * a[rows]
        al = -kk_c * jnp.exp(cum - lw)
        rt = rr[rows] * w_inc
        be, kt = ka_c * w_inv, k_mod[rows] * w_inv
        gm = _bdot_t(jnp.concatenate([al, rt], axis=0),
                     jnp.concatenate([_rep_rows(be, bd), _rep_rows(kt, bd)], axis=0))
        a_ab = jnp.where(strict > 0, gm[0:L, 0:256], 0.0)
        a_ak = jnp.where(strict > 0, gm[0:L, 256:512], 0.0)
        r_b = jnp.where(incl > 0, gm[L : 2 * L, 0:256], 0.0)
        r_k = jnp.where(incl > 0, gm[L : 2 * L, 256:512], 0.0)
        p = _rep_rows(a_ab, bd)
        t = eye + p
        for _ in range(5):
            p = _bdot(p, p)
            t = t + _bdot(t, p)
        t_cols = t[0:L] + t[L : 2 * L] + t[2 * L : 3 * L] + t[3 * L : 4 * L]
        s_vk = sv[...]
        v_rep = _rep_rows(v_c, bd)
        x = _bdot_t(al, s_vk) + _bdot(a_ak, v_rep)
        u = _bdot(t_cols, _rep_rows(x, bd))
        outs.append(_bdot_t(rt, s_vk) + _bdot(r_b, _rep_rows(u, bd)) + _bdot(r_k, v_rep))
        sv[...] = s_vk * jnp.exp(cum[L - 1 : L, :]) + _bdot_tn(
            jnp.concatenate([u, v_c], axis=0), jnp.concatenate([ka_c * end, k_mod[rows] * end], axis=0)) * bd
    y = jnp.concatenate(outs, axis=0)
    o_ref[...] = _rwkv_out(y, bonus, g, ones_bd, lnw_ref[...], lnb_ref[...])

    @pl.when(i == pl.num_programs(1) - 1)
    def _():
        s_out[0] = sv[...]


def _rwkv_chunk(p_rwkv, wts, n_b, t_len):
    tc = min(256, t_len)
    nt = t_len // tc
    const = lambda a: pl.BlockSpec(a.shape, lambda b, i: (0,) * a.ndim)
    vmem = 2 * _nbytes((tc, W_RWKV + BRANCH)) + 24 * _nbytes((tc, BRANCH)) + 16 * _nbytes((BRANCH, BRANCH)) + (6 << 20)
    return pl.pallas_call(
        functools.partial(_rwkv_chunk_kernel, tc=tc),
        grid=(n_b, nt),
        in_specs=[pl.BlockSpec((tc, W_RWKV), lambda b, i: (b * nt + i, 0))] + [const(a) for a in wts],
        out_specs=[pl.BlockSpec((tc, BRANCH), lambda b, i: (b * nt + i, 0)),
                   pl.BlockSpec((1, BRANCH, BRANCH), lambda b, i: (b, 0, 0))],
        out_shape=[jax.ShapeDtypeStruct((n_b * t_len, BRANCH), F32), jax.ShapeDtypeStruct((n_b, BRANCH, BRANCH), F32)],
        scratch_shapes=[pltpu.VMEM((1, W_RWKV), F32), pltpu.VMEM((BRANCH, BRANCH), F32)],
        compiler_params=_params(("parallel", "arbitrary"), vmem),
    )(p_rwkv, *wts)


def _norm_matmul_kernel(x_ref, g_ref, w_ref, *outs):
    h = _rms(x_ref[...], g_ref[...]).astype(BF16)
    off = 0
    for o_ref in outs:
        width = o_ref.shape[1]
        o_ref[...] = jnp.dot(h, w_ref[:, off : off + width], preferred_element_type=F32)
        off += width


def _norm_matmul(x, g, w, widths, tm):
    n = x.shape[0]
    total = sum(widths)
    vmem = 2 * _nbytes((tm, D_MODEL + total)) + 2 * _nbytes((D_MODEL, total), BF16) + (6 << 20)
    return pl.pallas_call(
        _norm_matmul_kernel,
        grid=(n // tm,),
        in_specs=[pl.BlockSpec((tm, D_MODEL), lambda i: (i, 0)), pl.BlockSpec((1, D_MODEL), lambda i: (0, 0)),
                  pl.BlockSpec((D_MODEL, total), lambda i: (0, 0))],
        out_specs=[pl.BlockSpec((tm, w_), lambda i: (i, 0)) for w_ in widths],
        out_shape=[jax.ShapeDtypeStruct((n, w_), F32) for w_ in widths],
        compiler_params=_params(("parallel",), vmem),
    )(x, g, w)


def _proj_res_kernel(x_ref, a_ref, w_ref, g_ref, o_ref):
    o_ref[...] = x_ref[...] + _rms(_bdot(a_ref[...], w_ref[...]), g_ref[...])


def _proj_res(x, a, w, g, tm):
    n = x.shape[0]
    row = pl.BlockSpec((tm, D_MODEL), lambda i: (i, 0))
    vmem = 6 * _nbytes((tm, D_MODEL)) + 2 * _nbytes((D_MODEL, D_MODEL), BF16) + (6 << 20)
    return pl.pallas_call(
        _proj_res_kernel,
        grid=(n // tm,),
        in_specs=[row, row, pl.BlockSpec((D_MODEL, D_MODEL), lambda i: (0, 0)), pl.BlockSpec((1, D_MODEL), lambda i: (0, 0))],
        out_specs=row,
        out_shape=jax.ShapeDtypeStruct((n, D_MODEL), F32),
        compiler_params=_params(("parallel",), vmem),
    )(x, a, w, g)


def _cross_attn_kernel(q_ref, k_ref, v_ref, o_ref):
    q = q_ref[...]
    outs = []
    for h in range(N_HEADS):
        cols = slice(h * CA_DIM, (h + 1) * CA_DIM)
        s = _bdot_t(q[:, cols], k_ref[:, cols]) * (CA_DIM ** -0.5)
        s = s - jnp.max(s, axis=1, keepdims=True)
        p = jnp.exp(s)
        p = p / jnp.sum(p, axis=1, keepdims=True)
        outs.append(_bdot(p, v_ref[:, cols]))
    o_ref[...] = jnp.concatenate(outs, axis=1)


def _cross_attn(q, mem_k, mem_v, layer, tq):
    n_b, t_len, _ = q.shape
    mem = mem_k.shape[2]
    qspec = pl.BlockSpec((None, tq, D_MODEL), lambda b, i: (b, i, 0))
    mspec = pl.BlockSpec((None, None, mem, D_MODEL), lambda b, i: (layer, b, 0, 0))
    vmem = 4 * _nbytes((tq, D_MODEL)) + 4 * _nbytes((mem, D_MODEL)) + 8 * _nbytes((tq, D_MODEL)) + (6 << 20)
    return pl.pallas_call(
        _cross_attn_kernel,
        grid=(n_b, t_len // tq),
        in_specs=[qspec, mspec, mspec],
        out_specs=qspec,
        out_shape=jax.ShapeDtypeStruct(q.shape, F32),
        compiler_params=_params(("parallel", "parallel"), vmem),
    )(q, mem_k, mem_v)


def _gelu(x):
    return 0.5 * x * (1.0 + lax.erf(x * (2.0 ** -0.5)))


def _ffn_kernel(x_ref, gpre_ref, wa_ref, wg_ref, cw_ref, cb_ref, wd_ref, st_ref, gpost_ref,
                o_ref, cnew_ref, hn, acc, carry, *, stride):
    i = pl.program_id(1)
    c = pl.program_id(2)
    tm = x_ref.shape[0]

    @pl.when(c == 0)
    def _():
        hn[...] = _rms(x_ref[...], gpre_ref[...]).astype(BF16)
        acc[...] = jnp.zeros(acc.shape, F32)

    @pl.when(i == 0)
    def _():
        carry[c] = st_ref[0]

    h = hn[...]
    a = jnp.dot(h, wa_ref[...], preferred_element_type=F32)
    gate = jnp.dot(h, wg_ref[...], preferred_element_type=F32)
    st = carry[c]
    prev1 = _shift_rows(a, st[stride:, :], stride)
    prev2 = _shift_rows(a, st, 2 * stride)
    cw = cw_ref[...]
    conv = cb_ref[...] + prev2 * cw[0:1, :] + prev1 * cw[1:2, :] + a * cw[2:3, :]
    tail = a[tm - 2 * stride :, :]
    carry[c] = tail
    fc = a.shape[1]
    for cc in range(FF_CHUNKS):
        @pl.when((c == cc) & (i == pl.num_programs(1) - 1))
        def _(cc=cc):
            cnew_ref[0, :, cc * fc : (cc + 1) * fc] = tail
    acc[...] += _bdot(_gelu(conv) * gate, wd_ref[...])

    @pl.when(c == pl.num_programs(2) - 1)
    def _():
        o_ref[...] = x_ref[...] + _rms(acc[...], gpost_ref[...])


def _ffn(x, gpre, wa, wg, cw, cb, wd, state, gpost, n_seq, tm, stride):
    n = x.shape[0]
    n_tiles = n // n_seq // tm
    fc = D_FF // FF_CHUNKS
    row = pl.BlockSpec((tm, D_MODEL), lambda q, i, c: (q * n_tiles + i, 0))
    vec = pl.BlockSpec((1, D_MODEL), lambda q, i, c: (0, 0))
    st = pl.BlockSpec((1, 2 * stride, fc), lambda q, i, c: (q, 0, c))
    vmem = (4 * _nbytes((tm, D_MODEL)) + 4 * _nbytes((D_MODEL, fc), BF16) + 2 * _nbytes((fc, D_MODEL), BF16)
            + 2 * _nbytes((tm, D_MODEL)) + 8 * _nbytes((tm, fc)) + 6 * _nbytes((2 * stride, fc)) + (4 << 20))
    return pl.pallas_call(
        functools.partial(_ffn_kernel, stride=stride),
        grid=(n_seq, n_tiles, FF_CHUNKS),
        in_specs=[row, vec,
                  pl.BlockSpec((D_MODEL, fc), lambda q, i, c: (0, c)),
                  pl.BlockSpec((D_MODEL, fc), lambda q, i, c: (0, c)),
                  pl.BlockSpec((3, fc), lambda q, i, c: (0, c)),
                  pl.BlockSpec((1, fc), lambda q, i, c: (0, c)),
                  pl.BlockSpec((fc, D_MODEL), lambda q, i, c: (c, 0)),
                  st, vec],
        out_specs=[row, pl.BlockSpec((1, 2 * stride, D_FF), lambda q, i, c: (q, 0, 0))],
        out_shape=[jax.ShapeDtypeStruct((n, D_MODEL), F32), jax.ShapeDtypeStruct((n_seq, 2 * stride, D_FF), F32)],
        scratch_shapes=[pltpu.VMEM((tm, D_MODEL), BF16), pltpu.VMEM((tm, D_MODEL), F32),
                        pltpu.VMEM((FF_CHUNKS, 2 * stride, fc), F32)],
        compiler_params=_params(("arbitrary", "arbitrary", "arbitrary"), vmem),
    )(x, gpre, wa, wg, cw, cb, wd, state, gpost)


def _pad_cols(w, width):
    return jnp.pad(w, ((0, 0), (0, width - w.shape[1])))


def _row(v):
    return v.reshape(1, -1).astype(F32)


def _block_diag(blocks):
    rows = sum(b.shape[0] for b in blocks)
    cols = sum(b.shape[1] for b in blocks)
    out = jnp.zeros((rows, cols), blocks[0].dtype)
    r = c = 0
    for b in blocks:
        out = lax.dynamic_update_slice(out, b, (r, c))
        r += b.shape[0]
        c += b.shape[1]
    return out


def _layer_weights(p, l):
    w = p["w_in"][l]
    gla, mla, rw, ml = w[:, 0:784], w[:, 784:1200], w[:, 1200:2224], w[:, 2224:3256]
    gate = w[:, 3256:]
    w_gla = jnp.concatenate([gla[:, 0:512], gla[:, 528:784], _pad_cols(gla[:, 512:528], LANES)], axis=1)
    w_mla = _pad_cols(mla, W_MLA)
    w_ml = jnp.concatenate([ml[:, 0:768], ml[:, 776:1032], _pad_cols(ml[:, 768:776], LANES)], axis=1)
    w_all = jnp.concatenate([w_gla, w_mla, rw, w_ml, gate], axis=1).astype(BF16)

    ones_bd = _block_diag([jnp.ones((HEAD, HEAD), BF16)] * N_HEADS)
    ibias = jnp.zeros((LANES,), F32).at[0:N_HEADS].set(p["mlstm_i_bias"][l])
    fbias = jnp.zeros((LANES,), F32).at[N_HEADS : 2 * N_HEADS].set(p["mlstm_f_bias"][l])
    prep = [
        jnp.pad(p["gla_w_gate2"][l], ((0, LANES - 16), (0, 0))).astype(BF16), _row(p["gla_b_gate"][l]),
        _row(p["rwkv_mu"][l]), _row(p["rwkv_w0"][l]), p["rwkv_w2"][l].astype(BF16), _row(p["rwkv_a0"][l]),
        p["rwkv_a2"][l].astype(BF16), p["rwkv_g2"][l].astype(BF16), _row(p["rwkv_k_k"][l]), _row(p["rwkv_k_a"][l]),
        _row(p["rwkv_r_k"][l]), ones_bd, _row(ibias), _row(fbias),
    ]
    wuq = p["mla_w_uq"][l].reshape(256, N_HEADS, MLA_NOPE + MLA_ROPE)
    mla_w = [
        _row(p["mla_norm_q"][l]), _row(p["mla_norm_kv"][l]),
        wuq[:, :, :MLA_NOPE].reshape(256, N_HEADS * MLA_NOPE).astype(BF16),
        wuq[:, :, MLA_NOPE:].reshape(256, N_HEADS * MLA_ROPE).astype(BF16),
        _block_diag([p["mla_w_uk"][l][h].T for h in range(N_HEADS)]).astype(BF16),
    ]
    gnorm = _row(jnp.tile(p["gla_norm"][l], N_HEADS))
    post = [ones_bd, gnorm, _row(p["rwkv_ln_w"][l]), _row(p["rwkv_ln_b"][l]), _row(p["mlstm_norm"][l])]
    merge = [
        _block_diag([p["mla_w_uv"][l][h] for h in range(N_HEADS)]).astype(BF16),
        p["w_branch"][l].astype(BF16), p["w_out"][l].astype(BF16), _row(p["norm_post_mix"][l]),
    ]
    tri = lambda n: jnp.tril(jnp.ones((n, n), F32))
    blocks = lambda r, c: _block_diag([jnp.ones((r, c), F32)] * N_HEADS)
    gla_c = prep[0:2] + [gnorm, ones_bd, tri(GLA_CHUNK).astype(BF16), jnp.tile(tri(GLA_CHUNK), (1, N_HEADS)),
                         blocks(GLA_CHUNK, GLA_K), blocks(GLA_CHUNK, HEAD), blocks(HEAD, GLA_K)]
    mlstm_c = [_row(ibias), _row(fbias), _row(p["mlstm_norm"][l]), ones_bd, tri(MIX_CHUNK).astype(BF16)]
    rwkv_c = prep[2:12] + [_row(p["rwkv_ln_w"][l]), _row(p["rwkv_ln_b"][l]), tri(MIX_CHUNK).astype(BF16),
                           jnp.tile(jnp.tril(jnp.ones((MIX_CHUNK, MIX_CHUNK), F32), -1), (1, N_HEADS)),
                           jnp.tile(tri(MIX_CHUNK), (1, N_HEADS)), jnp.eye(N_HEADS * MIX_CHUNK, dtype=F32)]
    up = p["ffn_w_up"][l]
    return dict(
        w_all=w_all, prep=prep, mla=mla_w, post=post, merge=merge, gla_c=gla_c, mlstm_c=mlstm_c, rwkv_c=rwkv_c,
        g_pre_mix=_row(p["norm_pre_mix"][l]),
        g_pre_ca=_row(p["norm_pre_ca"][l]), wq=p["ca_w_q"][l].astype(BF16), wo=p["ca_w_o"][l].astype(BF16),
        g_post_ca=_row(p["norm_post_ca"][l]),
        g_mem=_row(p["ca_norm_mem"][l]),
        wkv=jnp.concatenate([p["ca_w_k"][l], p["ca_w_v"][l]], axis=1).astype(BF16),
        g_pre_ffn=_row(p["norm_pre_ffn"][l]), wa=up[:, :D_FF].astype(BF16), wg=up[:, D_FF:].astype(BF16),
        cw=p["ffn_conv_w"][l].astype(F32), cb=_row(p["ffn_conv_b"][l]), wd=p["ffn_w_down"][l].astype(BF16),
        g_post_ffn=_row(p["norm_post_ffn"][l]),
    )


def _rope_tables(pos):
    half = MLA_ROPE // 2
    inv_freq = ROPE_THETA ** (-jnp.arange(half, dtype=F32) / half)
    ang = pos[:, None] * inv_freq[None, :]
    cos, sin = jnp.cos(ang), jnp.sin(ang)
    return jnp.concatenate([cos, cos], axis=1), jnp.concatenate([-sin, sin], axis=1)


class _Group:
    def __init__(self, n_b, t_len, time_major):
        self.n_b, self.t_len, self.time_major = n_b, t_len, time_major
        self.n_seq = 1 if time_major else n_b
        self.stride = n_b if time_major else 1
        self.rows = n_b * t_len
        self.tm = min(512, self.rows // self.n_seq)
        pairs = n_b * N_HEADS
        self.groups = max(1, LANES // pairs)
        assert not time_major or (self.groups in (1, 2) and (pairs * self.groups) % LANES == 0)

    def to_btc(self, x):
        c = x.shape[-1]
        if self.time_major:
            return x.reshape(self.t_len, self.n_b, c).transpose(1, 0, 2)
        return x.reshape(self.n_b, self.t_len, c)

    def from_btc(self, x):
        c = x.shape[-1]
        if self.time_major:
            return x.transpose(1, 0, 2).reshape(self.rows, c)
        return x.reshape(self.rows, c)

    def key_operand(self, x, dim):
        g = self.groups
        x = self.to_btc(x).reshape(self.n_b, self.t_len, N_HEADS, g, dim // g)
        return x.transpose(1, 4, 3, 0, 2).reshape(self.t_len, dim // g, g * self.n_b * N_HEADS)

    def val_operand(self, x, dim):
        x = self.to_btc(x).reshape(self.n_b, self.t_len, N_HEADS, dim)
        x = x.transpose(1, 3, 0, 2).reshape(self.t_len, dim, self.n_b * N_HEADS)
        return jnp.concatenate([x] * self.groups, axis=2)

    def scan_to_rows(self, o):
        pairs = self.n_b * N_HEADS
        o = o[:, :, :pairs].reshape(self.t_len, 3, HEAD, self.n_b, N_HEADS)
        return self.from_btc(o.transpose(3, 0, 1, 4, 2).reshape(self.n_b, self.t_len, 3 * BRANCH))

    def state_kv_in(self, s, key_axis_last):
        g = self.groups
        if key_axis_last:
            s = jnp.swapaxes(s, 2, 3)
        k, v = s.shape[2], s.shape[3]
        s = s.reshape(self.n_b, N_HEADS, g, k // g, v)
        return s.transpose(3, 4, 2, 0, 1).reshape(k // g, v, g * self.n_b * N_HEADS)

    def state_kv_out(self, s, key_axis_last):
        g = self.groups
        kg, v, _ = s.shape
        s = s.reshape(kg, v, g, self.n_b, N_HEADS).transpose(3, 4, 2, 0, 1).reshape(self.n_b, N_HEADS, g * kg, v)
        return jnp.swapaxes(s, 2, 3) if key_axis_last else s

    def state_k_in(self, s):
        g = self.groups
        k = s.shape[2]
        s = s.reshape(self.n_b, N_HEADS, g, k // g)
        return s.transpose(3, 2, 0, 1).reshape(k // g, g * self.n_b * N_HEADS)

    def state_k_out(self, s):
        g = self.groups
        kg = s.shape[0]
        return s.reshape(kg, g, self.n_b, N_HEADS).transpose(2, 3, 1, 0).reshape(self.n_b, N_HEADS, g * kg)

    def state_m_in(self, s):
        row = jnp.concatenate([s.reshape(1, self.n_b * N_HEADS)] * self.groups, axis=1)
        return jnp.concatenate([row, jnp.zeros((SUBLANES - 1, row.shape[1]), F32)], axis=0)

    def state_m_out(self, s):
        return s[0, : self.n_b * N_HEADS].reshape(self.n_b, N_HEADS)


def _diag_blocks(s, rows, cols):
    n_b = s.shape[0]
    s = s.reshape(n_b, N_HEADS, rows, N_HEADS, cols)
    return jnp.stack([s[:, h, :, h, :] for h in range(N_HEADS)], axis=1)


def _mixer(grp, x, lw, states, pos0, mla_attend):
    p_gla, p_mla, p_rwkv, p_mlstm, p_gate = _mix_in(x, lw["g_pre_mix"], lw["w_all"], min(256, grp.tm))

    pos = pos0 + jnp.arange(grp.t_len, dtype=F32)
    cos2, sin2 = _rope_tables(pos)
    if grp.time_major:
        cos2, sin2 = jnp.repeat(cos2, grp.n_b, axis=0), jnp.repeat(sin2, grp.n_b, axis=0)
    q_lat, q_pe, ckv, kpe = _mla_prep(p_mla, cos2, sin2, lw["mla"], grp.n_seq, grp.tm)
    o_lat = mla_attend(q_lat, q_pe, ckv, kpe)

    recur = _recurrent_scan if grp.time_major else _recurrent_chunked
    o_gla, o_rwkv, o_mlstm, rec_state = recur(grp, p_gla, p_rwkv, p_mlstm, lw, states)
    x_new = _merge(x, o_gla, o_lat, o_rwkv, o_mlstm, p_gate, lw["merge"], min(256, grp.tm))
    gla_new, rwkv_new, mc_new, mn_new, mm_new = rec_state
    shift_new = grp.to_btc(p_rwkv)[:, -1]
    return x_new, (ckv, kpe, gla_new, rwkv_new, shift_new, mc_new, mn_new, mm_new)


def _recurrent_chunked(grp, p_gla, p_rwkv, p_mlstm, lw, states):
    del states
    o_gla, s_gla = _gla_chunk(p_gla, lw["gla_c"], grp.n_b, grp.t_len)
    o_ml, c_new, n_new, m_new = _mlstm_chunk(p_mlstm, lw["mlstm_c"], grp.n_b, grp.t_len)
    o_rw, s_rw = _rwkv_chunk(p_rwkv, lw["rwkv_c"], grp.n_b, grp.t_len)
    gla_new = jnp.swapaxes(_diag_blocks(s_gla, HEAD, GLA_K), 2, 3)
    rwkv_new = _diag_blocks(s_rw, HEAD, HEAD)
    return o_gla, o_rw, o_ml, (gla_new, rwkv_new, c_new, n_new[:, :N_HEADS, :], m_new[:, :N_HEADS, 0])


def _recurrent_scan(grp, p_gla, p_rwkv, p_mlstm, lw, states):
    gla_s, rwkv_s, shift_s, mc_s, mn_s, mm_s = states
    shift0 = shift_s.reshape(grp.n_seq, grp.stride, W_RWKV)
    gla_ops, rw_ops, aux, ml_ops = _prep(p_gla, p_rwkv, p_mlstm, shift0, lw["prep"], grp.n_seq, grp.tm, grp.stride)
    g = grp.groups
    gk = jnp.concatenate([grp.key_operand(gla_ops[:, 0:128], GLA_K), grp.key_operand(p_gla[:, 128:256], GLA_K),
                          grp.key_operand(gla_ops[:, 128:256], GLA_K)], axis=1)
    rk = jnp.concatenate([grp.key_operand(rw_ops[:, c * 256 : (c + 1) * 256], HEAD) for c in range(5)], axis=1)
    mk = jnp.concatenate([grp.key_operand(ml_ops[:, 0:256], HEAD), grp.key_operand(p_mlstm[:, 256:512], HEAD)], axis=1)
    gates = grp.to_btc(ml_ops[:, 256 : 256 + 2 * N_HEADS]).reshape(grp.n_b, grp.t_len, 2, N_HEADS)
    gates = gates.transpose(1, 2, 0, 3).reshape(grp.t_len, 2, grp.n_b * N_HEADS)
    gates = jnp.concatenate([gates] * g, axis=2)
    vops = jnp.concatenate([grp.val_operand(p_gla[:, 256:512], HEAD), grp.val_operand(rw_ops[:, 1280:1536], HEAD),
                            grp.val_operand(p_mlstm[:, 512:768], HEAD), gates,
                            jnp.zeros((grp.t_len, SUBLANES - 2, gates.shape[2]), F32)], axis=1)
    st_in = [grp.state_kv_in(gla_s, False), grp.state_kv_in(rwkv_s, True), grp.state_kv_in(mc_s, False),
             grp.state_k_in(mn_s), grp.state_m_in(mm_s)]
    o_scan, sg, sr, sc, sn, sm = _scan(gk, rk, mk, vops, st_in, grp.t_len, g)
    scan_rows = grp.scan_to_rows(o_scan)
    o_gla, o_rwkv, o_mlstm = _post(scan_rows, p_gla, aux, p_mlstm, lw["post"], min(256, grp.tm))
    return o_gla, o_rwkv, o_mlstm, (grp.state_kv_out(sg, False), grp.state_kv_out(sr, True),
                                    grp.state_kv_out(sc, False), grp.state_k_out(sn), grp.state_m_out(sm))


def _cross_and_ffn(grp, x, lw, mem_k, mem_v, layer, conv_state):
    (q,) = _norm_matmul(x, lw["g_pre_ca"], lw["wq"], (D_MODEL,), grp.tm)
    q = grp.to_btc(q)
    ca = _cross_attn(q, mem_k, mem_v, layer, min(512, grp.t_len))
    x = _proj_res(x, grp.from_btc(ca), lw["wo"], lw["g_post_ca"], grp.tm)
    x, conv_new = _ffn(x, lw["g_pre_ffn"], lw["wa"], lw["wg"], lw["cw"], lw["cb"], lw["wd"], conv_state,
                       lw["g_post_ffn"], grp.n_seq, grp.tm, grp.stride)
    return x, conv_new


def kernel(x_prompt, x_sample, mem_prompt, cache_ckv, cache_kpe, page_table, cache_mem_k, cache_mem_v, state_gla, state_rwkv, state_rwkv_shift, state_mlstm_c, state_mlstm_n, state_mlstm_m, state_conv, norm_pre_mix, norm_post_mix, norm_pre_ca, norm_post_ca, norm_pre_ffn, norm_post_ffn, w_in, gla_w_gate2, gla_b_gate, gla_norm, mla_norm_q, mla_norm_kv, mla_w_uq, mla_w_uk, mla_w_uv, rwkv_mu, rwkv_w0, rwkv_w2, rwkv_a0, rwkv_a2, rwkv_g2, rwkv_k_k, rwkv_k_a, rwkv_r_k, rwkv_ln_w, rwkv_ln_b, mlstm_i_bias, mlstm_f_bias, mlstm_norm, w_branch, w_out, ca_norm_mem, ca_w_q, ca_w_k, ca_w_v, ca_w_o, ffn_w_up, ffn_conv_w, ffn_conv_b, ffn_w_down):
    p = dict(norm_pre_mix=norm_pre_mix, norm_post_mix=norm_post_mix, norm_pre_ca=norm_pre_ca, norm_post_ca=norm_post_ca,
             norm_pre_ffn=norm_pre_ffn, norm_post_ffn=norm_post_ffn, w_in=w_in, gla_w_gate2=gla_w_gate2,
             gla_b_gate=gla_b_gate, gla_norm=gla_norm, mla_norm_q=mla_norm_q, mla_norm_kv=mla_norm_kv,
             mla_w_uq=mla_w_uq, mla_w_uk=mla_w_uk, mla_w_uv=mla_w_uv, rwkv_mu=rwkv_mu, rwkv_w0=rwkv_w0,
             rwkv_w2=rwkv_w2, rwkv_a0=rwkv_a0, rwkv_a2=rwkv_a2, rwkv_g2=rwkv_g2, rwkv_k_k=rwkv_k_k,
             rwkv_k_a=rwkv_k_a, rwkv_r_k=rwkv_r_k, rwkv_ln_w=rwkv_ln_w, rwkv_ln_b=rwkv_ln_b,
             mlstm_i_bias=mlstm_i_bias, mlstm_f_bias=mlstm_f_bias, mlstm_norm=mlstm_norm, w_branch=w_branch,
             w_out=w_out, ca_norm_mem=ca_norm_mem, ca_w_q=ca_w_q, ca_w_k=ca_w_k, ca_w_v=ca_w_v, ca_w_o=ca_w_o,
             ffn_w_up=ffn_w_up, ffn_conv_w=ffn_conv_w, ffn_conv_b=ffn_conv_b, ffn_w_down=ffn_w_down)
    depth = w_in.shape[0]
    b_p, t_p, _ = x_prompt.shape
    b_s, t_s, _ = x_sample.shape
    n_pages, page = page_table.shape[1], cache_ckv.shape[2]
    past_len = n_pages * page
    mem_len = mem_prompt.shape[1]
    gp = _Group(b_p, t_p, time_major=False)
    gs = _Group(b_s, t_s, time_major=True)

    xp = gp.from_btc(x_prompt)
    xs = gs.from_btc(x_sample)
    mem_rows = mem_prompt.reshape(b_p * mem_len, D_MODEL)
    mem_k_c = cache_mem_k.reshape(depth, b_s, mem_len, D_MODEL)
    mem_v_c = cache_mem_v.reshape(depth, b_s, mem_len, D_MODEL)
    cache_kpe_t = jnp.swapaxes(cache_kpe, 2, 3)

    conv0_p = jnp.zeros((b_p, 2, D_FF), F32)

    new_p, new_s, mem_k_new, mem_v_new = [], [], [], []
    for l in range(depth):
        lw = _layer_weights(p, l)

        mk, mv = _norm_matmul(mem_rows, lw["g_mem"], lw["wkv"], (D_MODEL, D_MODEL), min(512, mem_rows.shape[0]))
        attend_p = lambda q_lat, q_pe, ckv, kpe: _mla_prompt(q_lat, q_pe, ckv, kpe, b_p, t_p)
        xp, st = _mixer(gp, xp, lw, None, 0.0, attend_p)
        xp, conv_new = _cross_and_ffn(gp, xp, lw, mk.reshape(1, b_p, mem_len, D_MODEL),
                                      mv.reshape(1, b_p, mem_len, D_MODEL), 0, conv0_p)
        new_p.append(st + (conv_new,))
        mem_k_new.append(mk.reshape(b_p, mem_len, N_HEADS, CA_DIM))
        mem_v_new.append(mv.reshape(b_p, mem_len, N_HEADS, CA_DIM))

        def attend_s(q_lat, q_pe, ckv, kpe, l=l):
            rows = t_s * N_HEADS
            ql = gs.to_btc(q_lat).reshape(b_s, rows, MLA_LORA)
            qp = gs.to_btc(q_pe).reshape(b_s, rows, MLA_ROPE)
            cn = jnp.pad(gs.to_btc(ckv), ((0, 0), (0, page - t_s), (0, 0)))
            pn = jnp.pad(gs.to_btc(kpe), ((0, 0), (0, page - t_s), (0, 0)))
            o = _mla_sample(page_table, ql, qp, cn, pn, cache_ckv, cache_kpe_t, l)
            return gs.from_btc(o.reshape(b_s, t_s, N_HEADS * MLA_LORA))

        st_s = (state_gla[l], state_rwkv[l], state_rwkv_shift[l], state_mlstm_c[l], state_mlstm_n[l], state_mlstm_m[l])
        xs, st = _mixer(gs, xs, lw, st_s, float(past_len), attend_s)
        conv_s = state_conv[l].transpose(1, 0, 2).reshape(1, 2 * b_s, D_FF)
        xs, conv_new = _cross_and_ffn(gs, xs, lw, mem_k_c, mem_v_c, l, conv_s)
        new_s.append(st + (conv_new.reshape(2, b_s, D_FF).transpose(1, 0, 2),))

    def stack(per_layer, i, grp):
        vals = [st[i] for st in per_layer]
        if i in (0, 1):
            vals = [grp.to_btc(v) for v in vals]
        return jnp.stack(vals, axis=0)

    outs = [gp.to_btc(xp), gs.to_btc(xs), stack(new_p, 0, gp), stack(new_p, 1, gp), stack(new_s, 0, gs), stack(new_s, 1, gs),
            jnp.stack(mem_k_new, axis=0), jnp.stack(mem_v_new, axis=0)]
    for i in range(2, 9):
        outs.append(stack(new_p, i, gp))
        outs.append(stack(new_s, i, gs))
    return tuple(outs)
```

```python
import functools
import math

import jax
import jax.numpy as jnp
from jax import lax
from jax.experimental import pallas as pl
from jax.experimental.pallas import tpu as pltpu

F32 = jnp.float32
BF16 = jnp.bfloat16

D_MODEL = 1024
BRANCH = 256
N_HEADS = 4
EPS = 1e-6
GLA_K = 32
GLA_TAU = 16.0
MLA_NOPE = 64
MLA_ROPE = 32
MLA_LORA = 128
ROPE_THETA = 10000.0
RWKV_LN_EPS = 64e-5
HEAD = 64
CA_DIM = 256
D_FF = 2816
FF_CHUNKS = 2

LANES = 128
SUBLANES = 8
VMEM_BUDGET = 56 * 1024 * 1024

W_GLA, W_MLA, W_RWKV, W_MLSTM, W_GATE = 896, 512, 1024, 1152, 4096
PIECES = (W_GLA, W_MLA, W_RWKV, W_MLSTM, W_GATE)


def _params(sem, vmem_bytes):
    return pltpu.CompilerParams(dimension_semantics=sem, vmem_limit_bytes=int(min(max(vmem_bytes, 16 << 20), VMEM_BUDGET)))


def _nbytes(shape, dtype=F32):
    return math.prod(shape) * jnp.dtype(dtype).itemsize


def _rms(x, g):
    return x * lax.rsqrt(jnp.mean(x * x, axis=-1, keepdims=True) + EPS) * g


def _bdot(a, b):
    return jnp.dot(a.astype(BF16), b.astype(BF16), preferred_element_type=F32)


def _bdot_t(a, b):
    return lax.dot_general(a.astype(BF16), b.astype(BF16), (((1,), (1,)), ((), ())), preferred_element_type=F32)


def _group_sum(x, ones_bd):
    hi = x.astype(BF16)
    lo = (x - hi.astype(F32)).astype(BF16)
    return jnp.dot(hi, ones_bd, preferred_element_type=F32) + jnp.dot(lo, ones_bd, preferred_element_type=F32)


def _softplus(x):
    return jnp.maximum(x, 0.0) + jnp.log1p(jnp.exp(-jnp.abs(x)))


def _log_sigmoid(x):
    return -_softplus(-x)


def _shift_rows(x, first, s):
    n = x.shape[0]
    if s % SUBLANES == 0:
        return jnp.concatenate([first, x[: n - s]], axis=0)
    rolled = pltpu.roll(x, s, 0)
    row = lax.broadcasted_iota(jnp.int32, x.shape, 0)
    for r in range(s):
        rolled = jnp.where(row == r, first[r : r + 1, :], rolled)
    return rolled


def _mix_in_kernel(x_ref, g_ref, w_ref, *outs):
    h = _rms(x_ref[...], g_ref[...]).astype(BF16)
    off = 0
    for o_ref, width in zip(outs, PIECES):
        piece = jnp.dot(h, w_ref[:, off : off + width], preferred_element_type=F32)
        if o_ref.dtype == BF16:
            piece = jax.nn.sigmoid(piece)
        o_ref[...] = piece.astype(o_ref.dtype)
        off += width


def _mix_in(x, g, w_all, tm):
    n = x.shape[0]
    total = sum(PIECES)
    vmem = 2 * _nbytes((tm, D_MODEL)) + 2 * _nbytes((D_MODEL, total), BF16) + 2 * _nbytes((tm, total)) + (4 << 20)
    return pl.pallas_call(
        _mix_in_kernel,
        grid=(n // tm,),
        in_specs=[
            pl.BlockSpec((tm, D_MODEL), lambda i: (i, 0)),
            pl.BlockSpec((1, D_MODEL), lambda i: (0, 0)),
            pl.BlockSpec((D_MODEL, total), lambda i: (0, 0)),
        ],
        out_specs=[pl.BlockSpec((tm, w), lambda i: (i, 0)) for w in PIECES],
        out_shape=[jax.ShapeDtypeStruct((n, w), BF16 if w == W_GATE else F32) for w in PIECES],
        compiler_params=_params(("parallel",), vmem),
    )(x, g, w_all)


def _prep_kernel(pg_ref, pr_ref, pm_ref, shift_ref,
                 gate2_ref, bgate_ref, mu_ref, w0_ref, w2_ref, a0_ref, a2_ref, g2_ref, kk_ref, ka_ref, rk_ref,
                 ones_ref, ibias_ref, fbias_ref,
                 gla_ref, rw_ref, aux_ref, ml_ref, carry, *, stride):
    i = pl.program_id(1)
    tm = pr_ref.shape[0]

    @pl.when(i == 0)
    def _():
        carry[...] = shift_ref[0]

    pr = pr_ref[...]
    prev = _shift_rows(pr, carry[...], stride)
    carry[...] = pr[tm - stride :, :]
    log_w, kk, a, k_mod, rr, rv, g, bonus = _rwkv_mix(
        pr, prev, mu_ref[...], w0_ref[...], w2_ref[...], a0_ref[...], a2_ref[...], g2_ref[...], kk_ref[...],
        ka_ref[...], rk_ref[...], ones_ref[...])
    rw_ref[:, 0:256] = jnp.exp(log_w)
    rw_ref[:, 256:512] = kk
    rw_ref[:, 512:768] = kk * a
    rw_ref[:, 768:1024] = k_mod
    rw_ref[:, 1024:1280] = rr
    rw_ref[:, 1280:1536] = rv
    aux_ref[:, 0:256] = g
    aux_ref[:, 256:512] = bonus

    pg = pg_ref[...]
    z = _bdot(pg[:, 768:896], gate2_ref[...]) + bgate_ref[...]
    gla_ref[:, 0:128] = pg[:, 0:128] * (GLA_K ** -0.5)
    gla_ref[:, 128:256] = jnp.exp(_log_sigmoid(z) / GLA_TAU)

    pml = pm_ref[...]
    ifp = pml[:, 1024:1152]
    lane = lax.broadcasted_iota(jnp.int32, ifp.shape, 1)
    ml_ref[:, 0:256] = pml[:, 0:256] * (HEAD ** -0.5)
    ml_ref[:, 256:384] = jnp.where(lane < N_HEADS, ifp + ibias_ref[...], _log_sigmoid(ifp + fbias_ref[...]))


def _prep(p_gla, p_rwkv, p_mlstm, shift0, wts, n_seq, tm, stride):
    n = p_rwkv.shape[0]
    n_tiles = n // n_seq // tm
    row = lambda w: pl.BlockSpec((tm, w), lambda q, i: (q * n_tiles + i, 0))
    const = lambda a: pl.BlockSpec(a.shape, lambda q, i: (0,) * a.ndim)
    out_w = (256, 1536, 512, 384)
    vmem = 2 * _nbytes((tm, W_GLA + W_RWKV + W_MLSTM + sum(out_w))) + 8 * _nbytes((tm, W_RWKV)) + (6 << 20)
    return pl.pallas_call(
        functools.partial(_prep_kernel, stride=stride),
        grid=(n_seq, n_tiles),
        in_specs=[row(W_GLA), row(W_RWKV), row(W_MLSTM),
                  pl.BlockSpec((1, stride, W_RWKV), lambda q, i: (q, 0, 0))] + [const(a) for a in wts],
        out_specs=[row(w) for w in out_w],
        out_shape=[jax.ShapeDtypeStruct((n, w), F32) for w in out_w],
        scratch_shapes=[pltpu.VMEM((stride, W_RWKV), F32)],
        compiler_params=_params(("arbitrary", "arbitrary"), vmem),
    )(p_gla, p_rwkv, p_mlstm, shift0, *wts)


def _rope32(x, cos2, sin2):
    half = MLA_ROPE // 2
    swapped = jnp.concatenate([x[:, half:], x[:, :half]], axis=1)
    return x * cos2 + swapped * sin2


def _mla_prep_kernel(p_ref, cos_ref, sin_ref, nq_ref, nkv_ref, wn_ref, wp_ref, wuk_ref,
                     qlat_ref, qpe_ref, ckv_ref, kpe_ref):
    p = p_ref[...]
    cos2, sin2 = cos_ref[...], sin_ref[...]
    cq = _rms(p[:, 0:256], nq_ref[...])
    q_nope = _bdot(cq, wn_ref[...])
    q_pe = _bdot(cq, wp_ref[...])
    qlat_ref[...] = _bdot(q_nope, wuk_ref[...])
    qpe_ref[...] = jnp.concatenate(
        [_rope32(q_pe[:, h * MLA_ROPE : (h + 1) * MLA_ROPE], cos2, sin2) for h in range(N_HEADS)], axis=1)
    ckv_ref[...] = _rms(p[:, 256:384], nkv_ref[...])
    kpe_ref[...] = _rope32(p[:, 384:416], cos2, sin2)


def _mla_prep(p_mla, cos2, sin2, wts, n_seq, tm):
    n = p_mla.shape[0]
    n_tiles = n // n_seq // tm
    row = lambda w: pl.BlockSpec((tm, w), lambda q, i: (q * n_tiles + i, 0))
    tab = pl.BlockSpec((tm, MLA_ROPE), lambda q, i: (i, 0))
    const = lambda a: pl.BlockSpec(a.shape, lambda q, i: (0,) * a.ndim)
    out_w = (N_HEADS * MLA_LORA, N_HEADS * MLA_ROPE, MLA_LORA, MLA_ROPE)
    vmem = 2 * _nbytes((tm, W_MLA + sum(out_w) + 2 * LANES)) + 6 * _nbytes((tm, 512)) + (4 << 20)
    return pl.pallas_call(
        _mla_prep_kernel,
        grid=(n_seq, n_tiles),
        in_specs=[row(W_MLA), tab, tab] + [const(a) for a in wts],
        out_specs=[row(w) for w in out_w],
        out_shape=[jax.ShapeDtypeStruct((n, w), F32) for w in out_w],
        compiler_params=_params(("parallel", "parallel"), vmem),
    )(p_mla, cos2, sin2, *wts)


MLA_SCALE = (MLA_NOPE + MLA_ROPE) ** -0.5


def _stack_heads(x, width):
    return jnp.concatenate([x[:, h * width : (h + 1) * width] for h in range(N_HEADS)], axis=0)


def _mla_prompt_kernel(ql_ref, qp_ref, ckv_ref, kpe_ref, o_ref, *, tq, tk):
    qi = pl.program_id(1)
    q = (_stack_heads(ql_ref[...], MLA_LORA) * MLA_SCALE).astype(BF16)
    qp = (_stack_heads(qp_ref[...], MLA_ROPE) * MLA_SCALE).astype(BF16)
    rows = N_HEADS * tq

    def block(kb, carry, causal):
        m, l, acc = carry
        k0 = pl.multiple_of(kb * tk, tk)
        kc = ckv_ref[pl.ds(k0, tk), :].astype(BF16)
        kp = kpe_ref[pl.ds(k0, tk), :].astype(BF16)
        s = _bdot_t(q, kc) + _bdot_t(qp, kp)
        if causal:
            tpos = lax.broadcasted_iota(jnp.int32, (tq, tk), 0)
            qpos = jnp.concatenate([tpos] * N_HEADS, axis=0)
            s = jnp.where(lax.broadcasted_iota(jnp.int32, (rows, tk), 1) <= qpos, s, -jnp.inf)
        m_new = jnp.maximum(m, jnp.max(s, axis=1, keepdims=True))
        p = jnp.exp(s - m_new)
        alpha = jnp.exp(m - m_new)
        l = alpha * l + jnp.sum(p, axis=1, keepdims=True)
        acc = alpha * acc + jnp.dot(p.astype(BF16), kc, preferred_element_type=F32)
        return m_new, l, acc

    init = (jnp.full((rows, 1), -jnp.inf, F32), jnp.zeros((rows, 1), F32), jnp.zeros((rows, MLA_LORA), F32))
    carry = lax.fori_loop(0, qi, functools.partial(block, causal=False), init)
    _, l, acc = block(qi, carry, causal=True)
    out = acc / l
    o_ref[...] = jnp.concatenate([out[h * tq : (h + 1) * tq, :] for h in range(N_HEADS)], axis=1)


def _mla_prompt(q_lat, q_pe, ckv, kpe, n_b, t_len):
    tq = tk = min(256, t_len)
    nq = t_len // tq
    vmem = 2 * _nbytes((tq, 1024 + 128)) + 2 * _nbytes((t_len, 256)) + 12 * _nbytes((N_HEADS * tq, tk)) + (4 << 20)
    return pl.pallas_call(
        functools.partial(_mla_prompt_kernel, tq=tq, tk=tk),
        grid=(n_b, nq),
        in_specs=[
            pl.BlockSpec((tq, N_HEADS * MLA_LORA), lambda b, i: (b * nq + i, 0)),
            pl.BlockSpec((tq, N_HEADS * MLA_ROPE), lambda b, i: (b * nq + i, 0)),
            pl.BlockSpec((t_len, MLA_LORA), lambda b, i: (b, 0)),
            pl.BlockSpec((t_len, MLA_ROPE), lambda b, i: (b, 0)),
        ],
        out_specs=pl.BlockSpec((tq, N_HEADS * MLA_LORA), lambda b, i: (b * nq + i, 0)),
        out_shape=jax.ShapeDtypeStruct((n_b * t_len, N_HEADS * MLA_LORA), F32),
        compiler_params=_params(("parallel", "parallel"), vmem),
    )(q_lat, q_pe, ckv, kpe)


def _mla_sample_kernel(pt_ref, ql_ref, qp_ref, cnew_ref, pnew_ref, *rest, n_pg, page):
    ckv_refs = rest[:n_pg]
    kpe_refs = rest[n_pg : 2 * n_pg]
    o_ref = rest[2 * n_pg]
    m_s, l_s, acc_s = rest[2 * n_pg + 1 :]
    g = pl.program_id(1)
    rows = ql_ref.shape[1]

    @pl.when(g == 0)
    def _():
        m_s[...] = jnp.full(m_s.shape, -jnp.inf, F32)
        l_s[...] = jnp.zeros(l_s.shape, F32)
        acc_s[...] = jnp.zeros(acc_s.shape, F32)

    q = (ql_ref[0] * MLA_SCALE).astype(BF16)
    qp = (qp_ref[0] * MLA_SCALE).astype(BF16)

    def partial(s, v):
        m = jnp.max(s, axis=1, keepdims=True)
        p = jnp.exp(s - m)
        return m, jnp.sum(p, axis=1, keepdims=True), jnp.dot(p.astype(BF16), v, preferred_element_type=F32)

    def merge(parts):
        m_old = m_s[...]
        m_new = m_old
        for m, _, _ in parts:
            m_new = jnp.maximum(m_new, m)
        alpha = jnp.exp(m_old - m_new)
        l = alpha * l_s[...]
        acc = alpha * acc_s[...]
        for m, lp, op in parts:
            w = jnp.exp(m - m_new)
            l = l + w * lp
            acc = acc + w * op
        m_s[...] = m_new
        l_s[...] = l
        acc_s[...] = acc

    parts = []
    for c_ref, p_ref in zip(ckv_refs, kpe_refs):
        kc = c_ref[0, 0].astype(BF16)
        kp_t = p_ref[0, 0].astype(BF16)
        parts.append(partial(_bdot_t(q, kc) + jnp.dot(qp, kp_t, preferred_element_type=F32), kc))
    merge(parts)

    @pl.when(g == pl.num_programs(1) - 1)
    def _():
        kc = cnew_ref[0].astype(BF16)
        kp = pnew_ref[0].astype(BF16)
        s = _bdot_t(q, kc) + _bdot_t(qp, kp)
        row = lax.broadcasted_iota(jnp.int32, (rows, page), 0)
        tk = lax.broadcasted_iota(jnp.int32, (rows, page), 1)
        s = jnp.where(tk * N_HEADS <= row, s, -jnp.inf)
        merge([partial(s, kc)])
        o_ref[0] = acc_s[...] / l_s[...]


def _mla_sample(page_table, q_lat, q_pe, ckv_new, kpe_new, cache_ckv, cache_kpe_t, layer):
    n_b, n_pages = page_table.shape
    page = cache_ckv.shape[2]
    rows = q_lat.shape[1]
    n_pg = math.gcd(n_pages, 32)
    n_groups = n_pages // n_pg
    pt_flat = page_table.reshape(-1)

    def page_spec(shape, p):
        return pl.BlockSpec((1, 1) + shape, lambda b, g, pt: (layer, pt[b * n_pages + g * n_pg + p], 0, 0))

    per_b = lambda shape: pl.BlockSpec((1,) + shape, lambda b, g, pt: (b, 0, 0))
    grid_spec = pltpu.PrefetchScalarGridSpec(
        num_scalar_prefetch=1,
        grid=(n_b, n_groups),
        in_specs=[per_b((rows, MLA_LORA)), per_b((rows, MLA_ROPE)), per_b((page, MLA_LORA)), per_b((page, MLA_ROPE))]
        + [page_spec((page, MLA_LORA), p) for p in range(n_pg)] + [page_spec((MLA_ROPE, page), p) for p in range(n_pg)],
        out_specs=per_b((rows, MLA_LORA)),
        scratch_shapes=[pltpu.VMEM((rows, 1), F32), pltpu.VMEM((rows, 1), F32), pltpu.VMEM((rows, MLA_LORA), F32)],
    )
    vmem = 2 * n_pg * 2 * _nbytes((page, LANES)) + 4 * _nbytes((page, 2 * LANES)) + (8 << 20)
    return pl.pallas_call(
        functools.partial(_mla_sample_kernel, n_pg=n_pg, page=page),
        grid_spec=grid_spec,
        out_shape=jax.ShapeDtypeStruct((n_b, rows, MLA_LORA), F32),
        compiler_params=_params(("parallel", "arbitrary"), vmem),
    )(pt_flat, q_lat, q_pe, ckv_new, kpe_new, *([cache_ckv] * n_pg), *([cache_kpe_t] * n_pg))


V_TILES = HEAD // SUBLANES


def _scan_kernel(gk_ref, rk_ref, mk_ref, v_ref, sg0, sr0, sc0, sn0, sm0,
                 o_ref, sg_o, sr_o, sc_o, sn_o, sm_o, sg, sr, sc, sn, sm, *, tb, ksg, ks, groups):
    i = pl.program_id(1)

    @pl.when(i == 0)
    def _():
        sg[...] = sg0[...]
        sr[...] = sr0[...]
        sc[...] = sc0[...]
        sn[...] = sn0[...]
        sm[...] = sm0[...]

    def fold(x):
        return x + pltpu.roll(x, LANES // 2, 1) if groups == 2 else x

    def vt_slice(vt):
        return slice(vt * SUBLANES, (vt + 1) * SUBLANES)

    def rwkv_sa(t):
        acc = [jnp.zeros((SUBLANES, LANES), F32)] * V_TILES
        for k in range(ks):
            kk = rk_ref[t, ks + k : ks + k + 1, :]
            for vt in range(V_TILES):
                acc[vt] = acc[vt] + sr[k, vt_slice(vt), :] * kk
        return tuple(fold(a) for a in acc)

    def step(t, sa):
        tn = jnp.minimum(t + 1, tb - 1)
        zeros = [jnp.zeros((SUBLANES, LANES), F32)] * V_TILES

        gv = [v_ref[t, vt * SUBLANES : (vt + 1) * SUBLANES, :] for vt in range(V_TILES)]
        o = list(zeros)
        for k in range(ksg):
            q = gk_ref[t, k : k + 1, :]
            kr = gk_ref[t, ksg + k : ksg + k + 1, :]
            a = gk_ref[t, 2 * ksg + k : 2 * ksg + k + 1, :]
            for vt in range(V_TILES):
                new = sg[k, vt_slice(vt), :] * a + gv[vt] * kr
                sg[k, vt_slice(vt), :] = new
                o[vt] = o[vt] + new * q
        for vt in range(V_TILES):
            o_ref[t, vt_slice(vt), :] = fold(o[vt])

        rv = [v_ref[t, HEAD + vt * SUBLANES : HEAD + (vt + 1) * SUBLANES, :] for vt in range(V_TILES)]
        y = list(zeros)
        san = list(zeros)
        for k in range(ks):
            w = rk_ref[t, k : k + 1, :]
            b = rk_ref[t, 2 * ks + k : 2 * ks + k + 1, :]
            kr = rk_ref[t, 3 * ks + k : 3 * ks + k + 1, :]
            r = rk_ref[t, 4 * ks + k : 4 * ks + k + 1, :]
            kkn = rk_ref[tn, ks + k : ks + k + 1, :]
            for vt in range(V_TILES):
                new = sr[k, vt_slice(vt), :] * w - sa[vt] * b + rv[vt] * kr
                sr[k, vt_slice(vt), :] = new
                y[vt] = y[vt] + new * r
                san[vt] = san[vt] + new * kkn
        for vt in range(V_TILES):
            o_ref[t, HEAD + vt * SUBLANES : HEAD + (vt + 1) * SUBLANES, :] = fold(y[vt])

        mv = [v_ref[t, 2 * HEAD + vt * SUBLANES : 2 * HEAD + (vt + 1) * SUBLANES, :] for vt in range(V_TILES)]
        i_t = v_ref[t, 3 * HEAD : 3 * HEAD + 1, :]
        lf = v_ref[t, 3 * HEAD + 1 : 3 * HEAD + 2, :]
        m_old = sm[0:1, :]
        m_new = jnp.maximum(lf + m_old, i_t)
        fs = jnp.exp(lf + m_old - m_new)
        isc = jnp.exp(i_t - m_new)
        sm[0:1, :] = m_new
        q_tile = mk_ref[t, 0:ks, :]
        n_new = fs * sn[...] + isc * mk_ref[t, ks : 2 * ks, :]
        sn[...] = n_new
        den = fold(jnp.sum(n_new * q_tile, axis=0, keepdims=True))
        fs_b = jnp.broadcast_to(fs, (SUBLANES, LANES))
        num = list(zeros)
        for k in range(ks):
            q = mk_ref[t, k : k + 1, :]
            ik = isc * mk_ref[t, ks + k : ks + k + 1, :]
            for vt in range(V_TILES):
                new = sc[k, vt_slice(vt), :] * fs_b + mv[vt] * ik
                sc[k, vt_slice(vt), :] = new
                num[vt] = num[vt] + new * q
        scale = 1.0 / jnp.maximum(jnp.abs(den), jnp.exp(-m_new))
        for vt in range(V_TILES):
            o_ref[t, 2 * HEAD + vt * SUBLANES : 2 * HEAD + (vt + 1) * SUBLANES, :] = fold(num[vt]) * scale
        return tuple(fold(x) for x in san)

    lax.fori_loop(0, tb, step, rwkv_sa(0))

    @pl.when(i == pl.num_programs(1) - 1)
    def _():
        sg_o[...] = sg[...]
        sr_o[...] = sr[...]
        sc_o[...] = sc[...]
        sn_o[...] = sn[...]
        sm_o[...] = sm[...]


def _scan(gk, rk, mk, vops, states, t_len, groups):
    ks = HEAD // groups
    ksg = GLA_K // groups
    n_lanes = gk.shape[2]
    n_lt = n_lanes // LANES
    tb = min(32, t_len)
    op = lambda rows: pl.BlockSpec((tb, rows, LANES), lambda j, i: (i, 0, j))
    st3 = lambda k: pl.BlockSpec((k, HEAD, LANES), lambda j, i: (0, 0, j))
    st2 = lambda k: pl.BlockSpec((k, LANES), lambda j, i: (0, j))
    v_rows = vops.shape[1]
    st_specs = [st3(ksg), st3(ks), st3(ks), st2(ks), st2(SUBLANES)]
    st_shapes = [(ksg, HEAD, n_lanes), (ks, HEAD, n_lanes), (ks, HEAD, n_lanes), (ks, n_lanes), (SUBLANES, n_lanes)]
    blk = _nbytes((tb, 3 * ksg + 5 * ks + 2 * ks + v_rows + 3 * HEAD, LANES))
    st_bytes = _nbytes(((ksg + 2 * ks) * HEAD + ks + SUBLANES, LANES))
    return pl.pallas_call(
        functools.partial(_scan_kernel, tb=tb, ksg=ksg, ks=ks, groups=groups),
        grid=(n_lt, t_len // tb),
        in_specs=[op(3 * ksg), op(5 * ks), op(2 * ks), op(v_rows)] + st_specs,
        out_specs=[op(3 * HEAD)] + st_specs,
        out_shape=[jax.ShapeDtypeStruct((t_len, 3 * HEAD, n_lanes), F32)]
        + [jax.ShapeDtypeStruct(s, F32) for s in st_shapes],
        scratch_shapes=[pltpu.VMEM((ksg, HEAD, LANES), F32), pltpu.VMEM((ks, HEAD, LANES), F32),
                        pltpu.VMEM((ks, HEAD, LANES), F32), pltpu.VMEM((ks, LANES), F32),
                        pltpu.VMEM((SUBLANES, LANES), F32)],
        compiler_params=_params(("parallel", "arbitrary"), 2 * blk + 5 * st_bytes + (4 << 20)),
    )(gk, rk, mk, vops, *states)


def _head_ln(x, ones_bd, eps):
    mean = _group_sum(x, ones_bd) * (1.0 / HEAD)
    xc = x - mean
    var = _group_sum(xc * xc, ones_bd) * (1.0 / HEAD)
    return xc * lax.rsqrt(var + eps)


def _gla_out(o, gate_r, ones_bd, gnorm):
    ms = _group_sum(o * o, ones_bd) * (1.0 / HEAD)
    return o * lax.rsqrt(ms + EPS) * gnorm * jax.nn.silu(gate_r)


def _rwkv_out(y, bonus, g, ones_bd, lnw, lnb):
    return (_head_ln(y, ones_bd, RWKV_LN_EPS) * lnw + lnb + bonus) * g


def _mlstm_out(h, gate_o, ones_bd, mnorm):
    return _head_ln(h, ones_bd, EPS) * mnorm * jax.nn.sigmoid(gate_o)


def _post_kernel(scan_ref, pg_ref, aux_ref, pm_ref, ones_ref, gnorm_ref, lnw_ref, lnb_ref, mnorm_ref,
                 og_ref, or_ref, om_ref):
    ones_bd = ones_ref[...]
    sc = scan_ref[...]
    aux = aux_ref[...]
    og_ref[...] = _gla_out(sc[:, 0:256], pg_ref[:, 512:768], ones_bd, gnorm_ref[...])
    or_ref[...] = _rwkv_out(sc[:, 256:512], aux[:, 256:512], aux[:, 0:256], ones_bd, lnw_ref[...], lnb_ref[...])
    om_ref[...] = _mlstm_out(sc[:, 512:768], pm_ref[:, 768:1024], ones_bd, mnorm_ref[...])


def _post(scan_rows, p_gla, aux, p_mlstm, wts, tm):
    n = scan_rows.shape[0]
    row = lambda w: pl.BlockSpec((tm, w), lambda i: (i, 0))
    const = lambda a: pl.BlockSpec(a.shape, lambda i: (0,) * a.ndim)
    widths = (768, W_GLA, 512, W_MLSTM)
    vmem = 2 * _nbytes((tm, sum(widths) + 768)) + 12 * _nbytes((tm, BRANCH)) + (4 << 20)
    return pl.pallas_call(
        _post_kernel,
        grid=(n // tm,),
        in_specs=[row(w) for w in widths] + [const(a) for a in wts],
        out_specs=[row(BRANCH)] * 3,
        out_shape=[jax.ShapeDtypeStruct((n, BRANCH), F32)] * 3,
        compiler_params=_params(("parallel",), vmem),
    )(scan_rows, p_gla, aux, p_mlstm, *wts)


def _merge_kernel(x_ref, og_ref, olat_ref, or_ref, om_ref, gate_ref, wuv_ref, wbr_ref, wout_ref, gpost_ref, o_ref):
    o_mla = _bdot(olat_ref[...], wuv_ref[...])
    mixed = jnp.zeros((x_ref.shape[0], D_MODEL), F32)
    for n, br in enumerate((og_ref[...], o_mla, or_ref[...], om_ref[...])):
        up = _bdot(br, wbr_ref[n])
        mixed = mixed + gate_ref[:, n * D_MODEL : (n + 1) * D_MODEL].astype(F32) * up
    out = _bdot(mixed, wout_ref[...])
    o_ref[...] = x_ref[...] + _rms(out, gpost_ref[...])


def _merge(x, o_gla, o_lat, o_rwkv, o_mlstm, p_gate, wts, tm):
    n = x.shape[0]
    row = lambda w: pl.BlockSpec((tm, w), lambda i: (i, 0))
    const = lambda a: pl.BlockSpec(a.shape, lambda i: (0,) * a.ndim)
    widths = (D_MODEL, BRANCH, 512, BRANCH, BRANCH, W_GATE)
    vmem = 2 * _nbytes((tm, sum(widths) + D_MODEL)) + 10 * _nbytes((tm, D_MODEL)) + (12 << 20)
    return pl.pallas_call(
        _merge_kernel,
        grid=(n // tm,),
        in_specs=[row(w) for w in widths] + [const(a) for a in wts],
        out_specs=row(D_MODEL),
        out_shape=jax.ShapeDtypeStruct((n, D_MODEL), F32),
        compiler_params=_params(("parallel",), vmem),
    )(x, o_gla, o_lat, o_rwkv, o_mlstm, p_gate, *wts)


GLA_CHUNK = 32
MIX_CHUNK = 64


def _bdot_tn(a, b):
    return lax.dot_general(a.astype(BF16), b.astype(BF16), (((0,), (0,)), ((), ())), preferred_element_type=F32)


def _cumsum_rows(tri, x):
    hi = x.astype(BF16)
    lo = (x - hi.astype(F32)).astype(BF16)
    return jnp.dot(tri, hi, preferred_element_type=F32) + jnp.dot(tri, lo, preferred_element_type=F32)


def _rep_rows(x, mask):
    return jnp.concatenate([x] * N_HEADS, axis=0) * mask


def _gla_chunk_kernel(pg_ref, gate2_ref, bgate_ref, gnorm_ref, ones_ref, tri_ref, causal_ref, kmask_ref, vmask_ref,
                      smask_ref, o_ref, st_ref, st, *, tc):
    i = pl.program_id(1)

    @pl.when(i == 0)
    def _():
        st[...] = jnp.zeros(st.shape, F32)

    L = GLA_CHUNK
    pg = pg_ref[...]
    log_a = _log_sigmoid(_bdot(pg[:, 768:896], gate2_ref[...]) + bgate_ref[...]) / GLA_TAU
    tri, causal = tri_ref[...], causal_ref[...]
    kmask, vmask, smask = kmask_ref[...], vmask_ref[...], smask_ref[...]
    outs = []
    for c in range(tc // L):
        rows = slice(c * L, (c + 1) * L)
        b = _cumsum_rows(tri, log_a[rows])
        b_last = b[L - 1 : L, :]
        q_dec = pg[rows, 0:128] * (GLA_K ** -0.5) * jnp.exp(b)
        k = pg[rows, 128:256]
        v = pg[rows, 256:512]
        k_rep = _rep_rows(k * jnp.exp(-b), kmask)
        v_rep = _rep_rows(v, vmask)
        att = _bdot_t(q_dec, k_rep) * causal
        s_t = st[...]
        outs.append(_bdot_t(q_dec, s_t) + _bdot(att, v_rep))
        st[...] = s_t * jnp.exp(b_last) + _bdot_tn(v, k * jnp.exp(b_last - b)) * smask
    o = jnp.concatenate(outs, axis=0)
    o_ref[...] = _gla_out(o, pg[:, 512:768], ones_ref[...], gnorm_ref[...])

    @pl.when(i == pl.num_programs(1) - 1)
    def _():
        st_ref[0] = st[...]


def _gla_chunk(p_gla, wts, n_b, t_len):
    tc = min(256, t_len)
    nt = t_len // tc
    const = lambda a: pl.BlockSpec(a.shape, lambda b, i: (0,) * a.ndim)
    vmem = 2 * _nbytes((tc, W_GLA + BRANCH)) + 12 * _nbytes((tc, BRANCH)) + (6 << 20)
    return pl.pallas_call(
        functools.partial(_gla_chunk_kernel, tc=tc),
        grid=(n_b, nt),
        in_specs=[pl.BlockSpec((tc, W_GLA), lambda b, i: (b * nt + i, 0))] + [const(a) for a in wts],
        out_specs=[pl.BlockSpec((tc, BRANCH), lambda b, i: (b * nt + i, 0)),
                   pl.BlockSpec((1, BRANCH, N_HEADS * GLA_K), lambda b, i: (b, 0, 0))],
        out_shape=[jax.ShapeDtypeStruct((n_b * t_len, BRANCH), F32),
                   jax.ShapeDtypeStruct((n_b, BRANCH, N_HEADS * GLA_K), F32)],
        scratch_shapes=[pltpu.VMEM((BRANCH, N_HEADS * GLA_K), F32)],
        compiler_params=_params(("parallel", "arbitrary"), vmem),
    )(p_gla, *wts)


def _mlstm_chunk_kernel(pm_ref, ibias_ref, fbias_ref, mnorm_ref, ones_ref, tri_ref,
                        o_ref, c_out, n_out, m_out, cs, ns, ms, *, tc):
    i = pl.program_id(1)

    @pl.when(i == 0)
    def _():
        cs[...] = jnp.zeros(cs.shape, F32)
        ns[...] = jnp.zeros(ns.shape, F32)
        ms[...] = jnp.zeros(ms.shape, F32)

    L = MIX_CHUNK
    pm = pm_ref[...]
    ifp = pm[:, 1024:1152]
    lane = lax.broadcasted_iota(jnp.int32, ifp.shape, 1)
    gates = jnp.where(lane < N_HEADS, ifp + ibias_ref[...], _log_sigmoid(ifp + fbias_ref[...]))
    tri = tri_ref[...]
    causal = lax.broadcasted_iota(jnp.int32, (L, L), 1) <= lax.broadcasted_iota(jnp.int32, (L, L), 0)
    lane_c = lax.broadcasted_iota(jnp.int32, (L, LANES), 1)
    row_t = lax.broadcasted_iota(jnp.int32, (LANES, L), 0)
    last_lane = lax.broadcasted_iota(jnp.int32, (1, L), 1) == L - 1

    def pick_col(x, j):
        return jnp.sum(jnp.where(lane_c == j, x, 0.0), axis=1, keepdims=True)

    def pick_row(x_t, j):
        return jnp.sum(jnp.where(row_t == j, x_t, 0.0), axis=0, keepdims=True)

    chunks = []
    for c in range(tc // L):
        rows = slice(c * L, (c + 1) * L)
        g_c = gates[rows]
        z = jnp.where(lane_c < N_HEADS, g_c, _cumsum_rows(tri, g_c))
        z_t = z.T
        heads = []
        for h in range(N_HEADS):
            cols = slice(h * HEAD, (h + 1) * HEAD)
            q = pm[rows, cols] * (HEAD ** -0.5)
            k = pm[rows, 256 + h * HEAD : 256 + (h + 1) * HEAD]
            v = pm[rows, 512 + h * HEAD : 512 + (h + 1) * HEAD]
            i_col, b_col = pick_col(z, h), pick_col(z, N_HEADS + h)
            i_row, b_row = pick_row(z_t, h), pick_row(z_t, N_HEADS + h)
            m0 = jnp.max(ms[h : h + 1, :], axis=1, keepdims=True)
            d = jnp.where(causal, b_col - b_row + i_row, -jnp.inf)
            inter = b_col + m0
            m_t = jnp.maximum(inter, jnp.max(d, axis=1, keepdims=True))
            w_inter = jnp.exp(inter - m_t)
            w_intra = jnp.exp(d - m_t) * _bdot_t(q, k)
            c_h, n_h = cs[h], ns[h : h + 1, :]
            num = w_inter * _bdot(q, c_h) + _bdot(w_intra, v)
            den = w_inter * jnp.sum(q * n_h, axis=1, keepdims=True) + jnp.sum(w_intra, axis=1, keepdims=True)
            heads.append(num / jnp.maximum(jnp.abs(den), jnp.exp(-m_t)))
            b_last = jnp.sum(jnp.where(last_lane, b_row, 0.0), axis=1, keepdims=True)
            m_new = jnp.maximum(b_last + m0, jnp.max(b_last - b_row + i_row, axis=1, keepdims=True))
            scale_old = jnp.exp(b_last + m0 - m_new)
            kw = k * jnp.exp(b_last - b_col + i_col - m_new)
            cs[h] = scale_old * c_h + _bdot_tn(kw, v)
            ns[h : h + 1, :] = scale_old * n_h + jnp.sum(kw, axis=0, keepdims=True)
            ms[h : h + 1, :] = jnp.broadcast_to(m_new, (1, LANES))
        chunks.append(jnp.concatenate(heads, axis=1))
    hm = jnp.concatenate(chunks, axis=0)
    o_ref[...] = _mlstm_out(hm, pm[:, 768:1024], ones_ref[...], mnorm_ref[...])

    @pl.when(i == pl.num_programs(1) - 1)
    def _():
        c_out[0] = cs[...]
        n_out[0] = ns[...]
        m_out[0] = ms[...]


def _mlstm_chunk(p_mlstm, wts, n_b, t_len):
    tc = min(256, t_len)
    nt = t_len // tc
    const = lambda a: pl.BlockSpec(a.shape, lambda b, i: (0,) * a.ndim)
    vmem = 2 * _nbytes((tc, W_MLSTM + BRANCH)) + 12 * _nbytes((tc, BRANCH)) + (6 << 20)
    return pl.pallas_call(
        functools.partial(_mlstm_chunk_kernel, tc=tc),
        grid=(n_b, nt),
        in_specs=[pl.BlockSpec((tc, W_MLSTM), lambda b, i: (b * nt + i, 0))] + [const(a) for a in wts],
        out_specs=[pl.BlockSpec((tc, BRANCH), lambda b, i: (b * nt + i, 0)),
                   pl.BlockSpec((1, N_HEADS, HEAD, HEAD), lambda b, i: (b, 0, 0, 0)),
                   pl.BlockSpec((1, SUBLANES, HEAD), lambda b, i: (b, 0, 0)),
                   pl.BlockSpec((1, SUBLANES, LANES), lambda b, i: (b, 0, 0))],
        out_shape=[jax.ShapeDtypeStruct((n_b * t_len, BRANCH), F32),
                   jax.ShapeDtypeStruct((n_b, N_HEADS, HEAD, HEAD), F32),
                   jax.ShapeDtypeStruct((n_b, SUBLANES, HEAD), F32),
                   jax.ShapeDtypeStruct((n_b, SUBLANES, LANES), F32)],
        scratch_shapes=[pltpu.VMEM((N_HEADS, HEAD, HEAD), F32), pltpu.VMEM((SUBLANES, HEAD), F32),
                        pltpu.VMEM((SUBLANES, LANES), F32)],
        compiler_params=_params(("parallel", "arbitrary"), vmem),
    )(p_mlstm, *wts)


def _rwkv_mix(pr, prev, mu, w0, w2, a0, a2, g2, kkw, ka, rkw, ones_bd):
    pm = pr + (prev - pr) * mu
    rr, rk, rv = pm[:, 0:256], pm[:, 256:512], pm[:, 512:768]
    wlo, alo, rglo = pm[:, 768:832], pm[:, 832:896], pm[:, 896:1024]
    log_w = -jnp.exp(-_softplus(-(w0 + _bdot(jnp.tanh(wlo), w2))) - 0.5)
    a = jax.nn.sigmoid(a0 + _bdot(alo, a2))
    g = _bdot(jax.nn.sigmoid(rglo), g2)
    kkr = rk * kkw
    kk = kkr / jnp.maximum(jnp.sqrt(_group_sum(kkr * kkr, ones_bd)), 1e-12)
    k_mod = rk * (1.0 + (a - 1.0) * ka)
    bonus = _group_sum(rr * k_mod * rkw, ones_bd) * rv
    return log_w, kk, a, k_mod, rr, rv, g, bonus


def _rwkv_chunk_kernel(pr_ref, mu_ref, w0_ref, w2_ref, a0_ref, a2_ref, g2_ref, kk_ref, ka_ref, rk_ref, ones_ref,
                       lnw_ref, lnb_ref, tri_ref, strict_ref, incl_ref, o_ref, s_out, carry, sv, *, tc):
    i = pl.program_id(1)

    @pl.when(i == 0)
    def _():
        carry[...] = jnp.zeros(carry.shape, F32)
        sv[...] = jnp.zeros(sv.shape, F32)

    L = MIX_CHUNK
    pr = pr_ref[...]
    prev = _shift_rows(pr, carry[...], 1)
    carry[...] = pr[tc - 1 :, :]
    ones_bd = ones_ref[...]
    log_w, kk, a, k_mod, rr, rv, g, bonus = _rwkv_mix(
        pr, prev, mu_ref[...], w0_ref[...], w2_ref[...], a0_ref[...], a2_ref[...], g2_ref[...], kk_ref[...],
        ka_ref[...], rk_ref[...], ones_bd)
    bd = ones_bd.astype(F32)
    tri, strict, incl = tri_ref[...], strict_ref[...], incl_ref[...]
    outs = []
    for c in range(tc // L):
        rows = slice(c * L, (c + 1) * L)
        lw = log_w[rows]
        cum = _cumsum_rows(tri, lw)
        end = jnp.exp(cum[L - 1 : L, :] - cum)
        w_inc, w_inv = jnp.exp(cum), jnp.exp(-cum)
        kk_c, v_c = kk[rows], rv[rows]
        ka_c = kk_c * a[rows]
        al = -kk_c * jnp.exp(cum - lw)
        rt = rr[rows] * w_inc
        be, kt = ka_c * w_inv, k_mod[rows] * w_inv
        gm = _bdot_t(jnp.concatenate([al, rt], axis=0),
                     jnp.concatenate([_rep_rows(be, bd), _rep_rows(kt, bd)], axis=0))
        a_ab = jnp.where(strict > 0, gm[0:L, 0:256], 0.0)
        a_ak = jnp.where(strict > 0, gm[0:L, 256:512], 0.0)
        r_b = jnp.where(incl > 0, gm[L : 2 * L, 0:256], 0.0)
        r_k = jnp.where(incl > 0, gm[L : 2 * L, 256:512], 0.0)
        t_cols = (incl - strict) + a_ab
        p_cols = _bdot(a_ab, _rep_rows(a_ab, bd))
        for _ in range(4):
            prod = _bdot(jnp.concatenate([t_cols, p_cols], axis=0), _rep_rows(p_cols, bd))
            t_cols = t_cols + prod[0:L]
            p_cols = prod[L : 2 * L]
        t_cols = t_cols + _bdot(t_cols, _rep_rows(p_cols, bd))
        s_vk = sv[...]
        v_rep = _rep_rows(v_c, bd)
        x = _bdot_t(al, s_vk) + _bdot(a_ak, v_rep)
        u = _bdot(t_cols, _rep_rows(x, bd))
        outs.append(_bdot_t(rt, s_vk) + _bdot(r_b, _rep_rows(u, bd)) + _bdot(r_k, v_rep))
        sv[...] = s_vk * jnp.exp(cum[L - 1 : L, :]) + _bdot_tn(
            jnp.concatenate([u, v_c], axis=0), jnp.concatenate([ka_c * end, k_mod[rows] * end], axis=0)) * bd
    y = jnp.concatenate(outs, axis=0)
    o_ref[...] = _rwkv_out(y, bonus, g, ones_bd, lnw_ref[...], lnb_ref[...])

    @pl.when(i == pl.num_programs(1) - 1)
    def _():
        s_out[0] = sv[...]


def _rwkv_chunk(p_rwkv, wts, n_b, t_len):
    tc = min(256, t_len)
    nt = t_len // tc
    const = lambda a: pl.BlockSpec(a.shape, lambda b, i: (0,) * a.ndim)
    vmem = 2 * _nbytes((tc, W_RWKV + BRANCH)) + 24 * _nbytes((tc, BRANCH)) + 16 * _nbytes((BRANCH, BRANCH)) + (6 << 20)
    return pl.pallas_call(
        functools.partial(_rwkv_chunk_kernel, tc=tc),
        grid=(n_b, nt),
        in_specs=[pl.BlockSpec((tc, W_RWKV), lambda b, i: (b * nt + i, 0))] + [const(a) for a in wts],
        out_specs=[pl.BlockSpec((tc, BRANCH), lambda b, i: (b * nt + i, 0)),
                   pl.BlockSpec((1, BRANCH, BRANCH), lambda b, i: (b, 0, 0))],
        out_shape=[jax.ShapeDtypeStruct((n_b * t_len, BRANCH), F32), jax.ShapeDtypeStruct((n_b, BRANCH, BRANCH), F32)],
        scratch_shapes=[pltpu.VMEM((1, W_RWKV), F32), pltpu.VMEM((BRANCH, BRANCH), F32)],
        compiler_params=_params(("parallel", "arbitrary"), vmem),
    )(p_rwkv, *wts)


def _norm_matmul_kernel(x_ref, g_ref, w_ref, *outs):
    h = _rms(x_ref[...], g_ref[...]).astype(BF16)
    off = 0
    for o_ref in outs:
        width = o_ref.shape[1]
        o_ref[...] = jnp.dot(h, w_ref[:, off : off + width], preferred_element_type=F32)
        off += width


def _norm_matmul(x, g, w, widths, tm):
    n = x.shape[0]
    total = sum(widths)
    vmem = 2 * _nbytes((tm, D_MODEL + total)) + 2 * _nbytes((D_MODEL, total), BF16) + (6 << 20)
    return pl.pallas_call(
        _norm_matmul_kernel,
        grid=(n // tm,),
        in_specs=[pl.BlockSpec((tm, D_MODEL), lambda i: (i, 0)), pl.BlockSpec((1, D_MODEL), lambda i: (0, 0)),
                  pl.BlockSpec((D_MODEL, total), lambda i: (0, 0))],
        out_specs=[pl.BlockSpec((tm, w_), lambda i: (i, 0)) for w_ in widths],
        out_shape=[jax.ShapeDtypeStruct((n, w_), F32) for w_ in widths],
        compiler_params=_params(("parallel",), vmem),
    )(x, g, w)


def _proj_res_kernel(x_ref, a_ref, w_ref, g_ref, o_ref):
    o_ref[...] = x_ref[...] + _rms(_bdot(a_ref[...], w_ref[...]), g_ref[...])


def _proj_res(x, a, w, g, tm):
    n = x.shape[0]
    row = pl.BlockSpec((tm, D_MODEL), lambda i: (i, 0))
    vmem = 6 * _nbytes((tm, D_MODEL)) + 2 * _nbytes((D_MODEL, D_MODEL), BF16) + (6 << 20)
    return pl.pallas_call(
        _proj_res_kernel,
        grid=(n // tm,),
        in_specs=[row, row, pl.BlockSpec((D_MODEL, D_MODEL), lambda i: (0, 0)), pl.BlockSpec((1, D_MODEL), lambda i: (0, 0))],
        out_specs=row,
        out_shape=jax.ShapeDtypeStruct((n, D_MODEL), F32),
        compiler_params=_params(("parallel",), vmem),
    )(x, a, w, g)


def _cross_attn_kernel(q_ref, k_ref, v_ref, o_ref):
    q = q_ref[...]
    outs = []
    for h in range(N_HEADS):
        cols = slice(h * CA_DIM, (h + 1) * CA_DIM)
        s = _bdot_t(q[:, cols], k_ref[:, cols]) * (CA_DIM ** -0.5)
        s = s - jnp.max(s, axis=1, keepdims=True)
        p = jnp.exp(s)
        p = p / jnp.sum(p, axis=1, keepdims=True)
        outs.append(_bdot(p, v_ref[:, cols]))
    o_ref[...] = jnp.concatenate(outs, axis=1)


def _cross_attn(q, mem_k, mem_v, layer, tq):
    n_b, t_len, _ = q.shape
    mem = mem_k.shape[2]
    qspec = pl.BlockSpec((None, tq, D_MODEL), lambda b, i: (b, i, 0))
    mspec = pl.BlockSpec((None, None, mem, D_MODEL), lambda b, i: (layer, b, 0, 0))
    vmem = 4 * _nbytes((tq, D_MODEL)) + 4 * _nbytes((mem, D_MODEL)) + 8 * _nbytes((tq, D_MODEL)) + (6 << 20)
    return pl.pallas_call(
        _cross_attn_kernel,
        grid=(n_b, t_len // tq),
        in_specs=[qspec, mspec, mspec],
        out_specs=qspec,
        out_shape=jax.ShapeDtypeStruct(q.shape, F32),
        compiler_params=_params(("parallel", "parallel"), vmem),
    )(q, mem_k, mem_v)


def _gelu(x):
    return 0.5 * x * (1.0 + lax.erf(x * (2.0 ** -0.5)))


def _ffn_kernel(x_ref, gpre_ref, wa_ref, wg_ref, cw_ref, cb_ref, wd_ref, st_ref, gpost_ref,
                o_ref, cnew_ref, hn, acc, carry, *, stride):
    i = pl.program_id(1)
    c = pl.program_id(2)
    tm = x_ref.shape[0]

    @pl.when(c == 0)
    def _():
        hn[...] = _rms(x_ref[...], gpre_ref[...]).astype(BF16)
        acc[...] = jnp.zeros(acc.shape, F32)

    @pl.when(i == 0)
    def _():
        carry[c] = st_ref[0]

    h = hn[...]
    a = jnp.dot(h, wa_ref[...], preferred_element_type=F32)
    gate = jnp.dot(h, wg_ref[...], preferred_element_type=F32)
    st = carry[c]
    prev1 = _shift_rows(a, st[stride:, :], stride)
    prev2 = _shift_rows(a, st, 2 * stride)
    cw = cw_ref[...]
    conv = cb_ref[...] + prev2 * cw[0:1, :] + prev1 * cw[1:2, :] + a * cw[2:3, :]
    tail = a[tm - 2 * stride :, :]
    carry[c] = tail
    fc = a.shape[1]
    for cc in range(FF_CHUNKS):
        @pl.when((c == cc) & (i == pl.num_programs(1) - 1))
        def _(cc=cc):
            cnew_ref[0, :, cc * fc : (cc + 1) * fc] = tail
    acc[...] += _bdot(_gelu(conv) * gate, wd_ref[...])

    @pl.when(c == pl.num_programs(2) - 1)
    def _():
        o_ref[...] = x_ref[...] + _rms(acc[...], gpost_ref[...])


def _ffn(x, gpre, wa, wg, cw, cb, wd, state, gpost, n_seq, tm, stride):
    n = x.shape[0]
    n_tiles = n // n_seq // tm
    fc = D_FF // FF_CHUNKS
    row = pl.BlockSpec((tm, D_MODEL), lambda q, i, c: (q * n_tiles + i, 0))
    vec = pl.BlockSpec((1, D_MODEL), lambda q, i, c: (0, 0))
    st = pl.BlockSpec((1, 2 * stride, fc), lambda q, i, c: (q, 0, c))
    vmem = (4 * _nbytes((tm, D_MODEL)) + 4 * _nbytes((D_MODEL, fc), BF16) + 2 * _nbytes((fc, D_MODEL), BF16)
            + 2 * _nbytes((tm, D_MODEL)) + 8 * _nbytes((tm, fc)) + 6 * _nbytes((2 * stride, fc)) + (4 << 20))
    return pl.pallas_call(
        functools.partial(_ffn_kernel, stride=stride),
        grid=(n_seq, n_tiles, FF_CHUNKS),
        in_specs=[row, vec,
                  pl.BlockSpec((D_MODEL, fc), lambda q, i, c: (0, c)),
                  pl.BlockSpec((D_MODEL, fc), lambda q, i, c: (0, c)),
                  pl.BlockSpec((3, fc), lambda q, i, c: (0, c)),
                  pl.BlockSpec((1, fc), lambda q, i, c: (0, c)),
                  pl.BlockSpec((fc, D_MODEL), lambda q, i, c: (c, 0)),
                  st, vec],
        out_specs=[row, pl.BlockSpec((1, 2 * stride, D_FF), lambda q, i, c: (q, 0, 0))],
        out_shape=[jax.ShapeDtypeStruct((n, D_MODEL), F32), jax.ShapeDtypeStruct((n_seq, 2 * stride, D_FF), F32)],
        scratch_shapes=[pltpu.VMEM((tm, D_MODEL), BF16), pltpu.VMEM((tm, D_MODEL), F32),
                        pltpu.VMEM((FF_CHUNKS, 2 * stride, fc), F32)],
        compiler_params=_params(("arbitrary", "arbitrary", "arbitrary"), vmem),
    )(x, gpre, wa, wg, cw, cb, wd, state, gpost)


def _pad_cols(w, width):
    return jnp.pad(w, ((0, 0), (0, width - w.shape[1])))


def _row(v):
    return v.reshape(1, -1).astype(F32)


def _block_diag(blocks):
    rows = sum(b.shape[0] for b in blocks)
    cols = sum(b.shape[1] for b in blocks)
    out = jnp.zeros((rows, cols), blocks[0].dtype)
    r = c = 0
    for b in blocks:
        out = lax.dynamic_update_slice(out, b, (r, c))
        r += b.shape[0]
        c += b.shape[1]
    return out


def _layer_weights(p, l):
    w = p["w_in"][l]
    gla, mla, rw, ml = w[:, 0:784], w[:, 784:1200], w[:, 1200:2224], w[:, 2224:3256]
    gate = w[:, 3256:]
    w_gla = jnp.concatenate([gla[:, 0:512], gla[:, 528:784], _pad_cols(gla[:, 512:528], LANES)], axis=1)
    w_mla = _pad_cols(mla, W_MLA)
    w_ml = jnp.concatenate([ml[:, 0:768], ml[:, 776:1032], _pad_cols(ml[:, 768:776], LANES)], axis=1)
    w_all = jnp.concatenate([w_gla, w_mla, rw, w_ml, gate], axis=1).astype(BF16)

    ones_bd = _block_diag([jnp.ones((HEAD, HEAD), BF16)] * N_HEADS)
    ibias = jnp.zeros((LANES,), F32).at[0:N_HEADS].set(p["mlstm_i_bias"][l])
    fbias = jnp.zeros((LANES,), F32).at[N_HEADS : 2 * N_HEADS].set(p["mlstm_f_bias"][l])
    prep = [
        jnp.pad(p["gla_w_gate2"][l], ((0, LANES - 16), (0, 0))).astype(BF16), _row(p["gla_b_gate"][l]),
        _row(p["rwkv_mu"][l]), _row(p["rwkv_w0"][l]), p["rwkv_w2"][l].astype(BF16), _row(p["rwkv_a0"][l]),
        p["rwkv_a2"][l].astype(BF16), p["rwkv_g2"][l].astype(BF16), _row(p["rwkv_k_k"][l]), _row(p["rwkv_k_a"][l]),
        _row(p["rwkv_r_k"][l]), ones_bd, _row(ibias), _row(fbias),
    ]
    wuq = p["mla_w_uq"][l].reshape(256, N_HEADS, MLA_NOPE + MLA_ROPE)
    mla_w = [
        _row(p["mla_norm_q"][l]), _row(p["mla_norm_kv"][l]),
        wuq[:, :, :MLA_NOPE].reshape(256, N_HEADS * MLA_NOPE).astype(BF16),
        wuq[:, :, MLA_NOPE:].reshape(256, N_HEADS * MLA_ROPE).astype(BF16),
        _block_diag([p["mla_w_uk"][l][h].T for h in range(N_HEADS)]).astype(BF16),
    ]
    gnorm = _row(jnp.tile(p["gla_norm"][l], N_HEADS))
    post = [ones_bd, gnorm, _row(p["rwkv_ln_w"][l]), _row(p["rwkv_ln_b"][l]), _row(p["mlstm_norm"][l])]
    merge = [
        _block_diag([p["mla_w_uv"][l][h] for h in range(N_HEADS)]).astype(BF16),
        p["w_branch"][l].astype(BF16), p["w_out"][l].astype(BF16), _row(p["norm_post_mix"][l]),
    ]
    tri = lambda n: jnp.tril(jnp.ones((n, n), F32))
    blocks = lambda r, c: _block_diag([jnp.ones((r, c), F32)] * N_HEADS)
    gla_c = prep[0:2] + [gnorm, ones_bd, tri(GLA_CHUNK).astype(BF16), jnp.tile(tri(GLA_CHUNK), (1, N_HEADS)),
                         blocks(GLA_CHUNK, GLA_K), blocks(GLA_CHUNK, HEAD), blocks(HEAD, GLA_K)]
    mlstm_c = [_row(ibias), _row(fbias), _row(p["mlstm_norm"][l]), ones_bd, tri(MIX_CHUNK).astype(BF16)]
    rwkv_c = prep[2:12] + [_row(p["rwkv_ln_w"][l]), _row(p["rwkv_ln_b"][l]), tri(MIX_CHUNK).astype(BF16),
                           jnp.tile(jnp.tril(jnp.ones((MIX_CHUNK, MIX_CHUNK), F32), -1), (1, N_HEADS)),
                           jnp.tile(tri(MIX_CHUNK), (1, N_HEADS))]
    up = p["ffn_w_up"][l]
    return dict(
        w_all=w_all, prep=prep, mla=mla_w, post=post, merge=merge, gla_c=gla_c, mlstm_c=mlstm_c, rwkv_c=rwkv_c,
        g_pre_mix=_row(p["norm_pre_mix"][l]),
        g_pre_ca=_row(p["norm_pre_ca"][l]), wq=p["ca_w_q"][l].astype(BF16), wo=p["ca_w_o"][l].astype(BF16),
        g_post_ca=_row(p["norm_post_ca"][l]),
        g_mem=_row(p["ca_norm_mem"][l]),
        wkv=jnp.concatenate([p["ca_w_k"][l], p["ca_w_v"][l]], axis=1).astype(BF16),
        g_pre_ffn=_row(p["norm_pre_ffn"][l]), wa=up[:, :D_FF].astype(BF16), wg=up[:, D_FF:].astype(BF16),
        cw=p["ffn_conv_w"][l].astype(F32), cb=_row(p["ffn_conv_b"][l]), wd=p["ffn_w_down"][l].astype(BF16),
        g_post_ffn=_row(p["norm_post_ffn"][l]),
    )


def _rope_tables(pos):
    half = MLA_ROPE // 2
    inv_freq = ROPE_THETA ** (-jnp.arange(half, dtype=F32) / half)
    ang = pos[:, None] * inv_freq[None, :]
    cos, sin = jnp.cos(ang), jnp.sin(ang)
    return jnp.concatenate([cos, cos], axis=1), jnp.concatenate([-sin, sin], axis=1)


class _Group:
    def __init__(self, n_b, t_len, time_major):
        self.n_b, self.t_len, self.time_major = n_b, t_len, time_major
        self.n_seq = 1 if time_major else n_b
        self.stride = n_b if time_major else 1
        self.rows = n_b * t_len
        self.tm = min(512, self.rows // self.n_seq)
        pairs = n_b * N_HEADS
        self.groups = max(1, LANES // pairs)
        assert not time_major or (self.groups in (1, 2) and (pairs * self.groups) % LANES == 0)

    def to_btc(self, x):
        c = x.shape[-1]
        if self.time_major:
            return x.reshape(self.t_len, self.n_b, c).transpose(1, 0, 2)
        return x.reshape(self.n_b, self.t_len, c)

    def from_btc(self, x):
        c = x.shape[-1]
        if self.time_major:
            return x.transpose(1, 0, 2).reshape(self.rows, c)
        return x.reshape(self.rows, c)

    def key_operand(self, x, dim):
        g = self.groups
        x = self.to_btc(x).reshape(self.n_b, self.t_len, N_HEADS, g, dim // g)
        return x.transpose(1, 4, 3, 0, 2).reshape(self.t_len, dim // g, g * self.n_b * N_HEADS)

    def val_operand(self, x, dim):
        x = self.to_btc(x).reshape(self.n_b, self.t_len, N_HEADS, dim)
        x = x.transpose(1, 3, 0, 2).reshape(self.t_len, dim, self.n_b * N_HEADS)
        return jnp.concatenate([x] * self.groups, axis=2)

    def scan_to_rows(self, o):
        pairs = self.n_b * N_HEADS
        o = o[:, :, :pairs].reshape(self.t_len, 3, HEAD, self.n_b, N_HEADS)
        return self.from_btc(o.transpose(3, 0, 1, 4, 2).reshape(self.n_b, self.t_len, 3 * BRANCH))

    def state_kv_in(self, s, key_axis_last):
        g = self.groups
        if key_axis_last:
            s = jnp.swapaxes(s, 2, 3)
        k, v = s.shape[2], s.shape[3]
        s = s.reshape(self.n_b, N_HEADS, g, k // g, v)
        return s.transpose(3, 4, 2, 0, 1).reshape(k // g, v, g * self.n_b * N_HEADS)

    def state_kv_out(self, s, key_axis_last):
        g = self.groups
        kg, v, _ = s.shape
        s = s.reshape(kg, v, g, self.n_b, N_HEADS).transpose(3, 4, 2, 0, 1).reshape(self.n_b, N_HEADS, g * kg, v)
        return jnp.swapaxes(s, 2, 3) if key_axis_last else s

    def state_k_in(self, s):
        g = self.groups
        k = s.shape[2]
        s = s.reshape(self.n_b, N_HEADS, g, k // g)
        return s.transpose(3, 2, 0, 1).reshape(k // g, g * self.n_b * N_HEADS)

    def state_k_out(self, s):
        g = self.groups
        kg = s.shape[0]
        return s.reshape(kg, g, self.n_b, N_HEADS).transpose(2, 3, 1, 0).reshape(self.n_b, N_HEADS, g * kg)

    def state_m_in(self, s):
        row = jnp.concatenate([s.reshape(1, self.n_b * N_HEADS)] * self.groups, axis=1)
        return jnp.concatenate([row, jnp.zeros((SUBLANES - 1, row.shape[1]), F32)], axis=0)

    def state_m_out(self, s):
        return s[0, : self.n_b * N_HEADS].reshape(self.n_b, N_HEADS)


def _diag_blocks(s, rows, cols):
    n_b = s.shape[0]
    s = s.reshape(n_b, N_HEADS, rows, N_HEADS, cols)
    return jnp.stack([s[:, h, :, h, :] for h in range(N_HEADS)], axis=1)


def _mixer(grp, x, lw, states, pos0, mla_attend):
    p_gla, p_mla, p_rwkv, p_mlstm, p_gate = _mix_in(x, lw["g_pre_mix"], lw["w_all"], min(256, grp.tm))

    pos = pos0 + jnp.arange(grp.t_len, dtype=F32)
    cos2, sin2 = _rope_tables(pos)
    if grp.time_major:
        cos2, sin2 = jnp.repeat(cos2, grp.n_b, axis=0), jnp.repeat(sin2, grp.n_b, axis=0)
    q_lat, q_pe, ckv, kpe = _mla_prep(p_mla, cos2, sin2, lw["mla"], grp.n_seq, grp.tm)
    o_lat = mla_attend(q_lat, q_pe, ckv, kpe)

    recur = _recurrent_scan if grp.time_major else _recurrent_chunked
    o_gla, o_rwkv, o_mlstm, rec_state = recur(grp, p_gla, p_rwkv, p_mlstm, lw, states)
    x_new = _merge(x, o_gla, o_lat, o_rwkv, o_mlstm, p_gate, lw["merge"], min(256, grp.tm))
    gla_new, rwkv_new, mc_new, mn_new, mm_new = rec_state
    shift_new = grp.to_btc(p_rwkv)[:, -1]
    return x_new, (ckv, kpe, gla_new, rwkv_new, shift_new, mc_new, mn_new, mm_new)


def _recurrent_chunked(grp, p_gla, p_rwkv, p_mlstm, lw, states):
    del states
    o_gla, s_gla = _gla_chunk(p_gla, lw["gla_c"], grp.n_b, grp.t_len)
    o_ml, c_new, n_new, m_new = _mlstm_chunk(p_mlstm, lw["mlstm_c"], grp.n_b, grp.t_len)
    o_rw, s_rw = _rwkv_chunk(p_rwkv, lw["rwkv_c"], grp.n_b, grp.t_len)
    gla_new = jnp.swapaxes(_diag_blocks(s_gla, HEAD, GLA_K), 2, 3)
    rwkv_new = _diag_blocks(s_rw, HEAD, HEAD)
    return o_gla, o_rw, o_ml, (gla_new, rwkv_new, c_new, n_new[:, :N_HEADS, :], m_new[:, :N_HEADS, 0])


def _recurrent_scan(grp, p_gla, p_rwkv, p_mlstm, lw, states):
    gla_s, rwkv_s, shift_s, mc_s, mn_s, mm_s = states
    shift0 = shift_s.reshape(grp.n_seq, grp.stride, W_RWKV)
    gla_ops, rw_ops, aux, ml_ops = _prep(p_gla, p_rwkv, p_mlstm, shift0, lw["prep"], grp.n_seq, grp.tm, grp.stride)
    g = grp.groups
    gk = jnp.concatenate([grp.key_operand(gla_ops[:, 0:128], GLA_K), grp.key_operand(p_gla[:, 128:256], GLA_K),
                          grp.key_operand(gla_ops[:, 128:256], GLA_K)], axis=1)
    rk = jnp.concatenate([grp.key_operand(rw_ops[:, c * 256 : (c + 1) * 256], HEAD) for c in range(5)], axis=1)
    mk = jnp.concatenate([grp.key_operand(ml_ops[:, 0:256], HEAD), grp.key_operand(p_mlstm[:, 256:512], HEAD)], axis=1)
    gates = grp.to_btc(ml_ops[:, 256 : 256 + 2 * N_HEADS]).reshape(grp.n_b, grp.t_len, 2, N_HEADS)
    gates = gates.transpose(1, 2, 0, 3).reshape(grp.t_len, 2, grp.n_b * N_HEADS)
    gates = jnp.concatenate([gates] * g, axis=2)
    vops = jnp.concatenate([grp.val_operand(p_gla[:, 256:512], HEAD), grp.val_operand(rw_ops[:, 1280:1536], HEAD),
                            grp.val_operand(p_mlstm[:, 512:768], HEAD), gates,
                            jnp.zeros((grp.t_len, SUBLANES - 2, gates.shape[2]), F32)], axis=1)
    st_in = [grp.state_kv_in(gla_s, False), grp.state_kv_in(rwkv_s, True), grp.state_kv_in(mc_s, False),
             grp.state_k_in(mn_s), grp.state_m_in(mm_s)]
    o_scan, sg, sr, sc, sn, sm = _scan(gk, rk, mk, vops, st_in, grp.t_len, g)
    scan_rows = grp.scan_to_rows(o_scan)
    o_gla, o_rwkv, o_mlstm = _post(scan_rows, p_gla, aux, p_mlstm, lw["post"], min(256, grp.tm))
    return o_gla, o_rwkv, o_mlstm, (grp.state_kv_out(sg, False), grp.state_kv_out(sr, True),
                                    grp.state_kv_out(sc, False), grp.state_k_out(sn), grp.state_m_out(sm))


def _cross_and_ffn(grp, x, lw, mem_k, mem_v, layer, conv_state):
    (q,) = _norm_matmul(x, lw["g_pre_ca"], lw["wq"], (D_MODEL,), grp.tm)
    q = grp.to_btc(q)
    ca = _cross_attn(q, mem_k, mem_v, layer, min(512, grp.t_len))
    x = _proj_res(x, grp.from_btc(ca), lw["wo"], lw["g_post_ca"], grp.tm)
    x, conv_new = _ffn(x, lw["g_pre_ffn"], lw["wa"], lw["wg"], lw["cw"], lw["cb"], lw["wd"], conv_state,
                       lw["g_post_ffn"], grp.n_seq, grp.tm, grp.stride)
    return x, conv_new


def kernel(x_prompt, x_sample, mem_prompt, cache_ckv, cache_kpe, page_table, cache_mem_k, cache_mem_v, state_gla, state_rwkv, state_rwkv_shift, state_mlstm_c, state_mlstm_n, state_mlstm_m, state_conv, norm_pre_mix, norm_post_mix, norm_pre_ca, norm_post_ca, norm_pre_ffn, norm_post_ffn, w_in, gla_w_gate2, gla_b_gate, gla_norm, mla_norm_q, mla_norm_kv, mla_w_uq, mla_w_uk, mla_w_uv, rwkv_mu, rwkv_w0, rwkv_w2, rwkv_a0, rwkv_a2, rwkv_g2, rwkv_k_k, rwkv_k_a, rwkv_r_k, rwkv_ln_w, rwkv_ln_b, mlstm_i_bias, mlstm_f_bias, mlstm_norm, w_branch, w_out, ca_norm_mem, ca_w_q, ca_w_k, ca_w_v, ca_w_o, ffn_w_up, ffn_conv_w, ffn_conv_b, ffn_w_down):
    p = dict(norm_pre_mix=norm_pre_mix, norm_post_mix=norm_post_mix, norm_pre_ca=norm_pre_ca, norm_post_ca=norm_post_ca,
             norm_pre_ffn=norm_pre_ffn, norm_post_ffn=norm_post_ffn, w_in=w_in, gla_w_gate2=gla_w_gate2,
             gla_b_gate=gla_b_gate, gla_norm=gla_norm, mla_norm_q=mla_norm_q, mla_norm_kv=mla_norm_kv,
             mla_w_uq=mla_w_uq, mla_w_uk=mla_w_uk, mla_w_uv=mla_w_uv, rwkv_mu=rwkv_mu, rwkv_w0=rwkv_w0,
             rwkv_w2=rwkv_w2, rwkv_a0=rwkv_a0, rwkv_a2=rwkv_a2, rwkv_g2=rwkv_g2, rwkv_k_k=rwkv_k_k,
             rwkv_k_a=rwkv_k_a, rwkv_r_k=rwkv_r_k, rwkv_ln_w=rwkv_ln_w, rwkv_ln_b=rwkv_ln_b,
             mlstm_i_bias=mlstm_i_bias, mlstm_f_bias=mlstm_f_bias, mlstm_norm=mlstm_norm, w_branch=w_branch,
             w_out=w_out, ca_norm_mem=ca_norm_mem, ca_w_q=ca_w_q, ca_w_k=ca_w_k, ca_w_v=ca_w_v, ca_w_o=ca_w_o,
             ffn_w_up=ffn_w_up, ffn_conv_w=ffn_conv_w, ffn_conv_b=ffn_conv_b, ffn_w_down=ffn_w_down)
    depth = w_in.shape[0]
    b_p, t_p, _ = x_prompt.shape
    b_s, t_s, _ = x_sample.shape
    n_pages, page = page_table.shape[1], cache_ckv.shape[2]
    past_len = n_pages * page
    mem_len = mem_prompt.shape[1]
    gp = _Group(b_p, t_p, time_major=False)
    gs = _Group(b_s, t_s, time_major=True)

    xp = gp.from_btc(x_prompt)
    xs = gs.from_btc(x_sample)
    mem_rows = mem_prompt.reshape(b_p * mem_len, D_MODEL)
    mem_k_c = cache_mem_k.reshape(depth, b_s, mem_len, D_MODEL)
    mem_v_c = cache_mem_v.reshape(depth, b_s, mem_len, D_MODEL)
    cache_kpe_t = jnp.swapaxes(cache_kpe, 2, 3)

    conv0_p = jnp.zeros((b_p, 2, D_FF), F32)

    new_p, new_s, mem_k_new, mem_v_new = [], [], [], []
    for l in range(depth):
        lw = _layer_weights(p, l)

        mk, mv = _norm_matmul(mem_rows, lw["g_mem"], lw["wkv"], (D_MODEL, D_MODEL), min(512, mem_rows.shape[0]))
        attend_p = lambda q_lat, q_pe, ckv, kpe: _mla_prompt(q_lat, q_pe, ckv, kpe, b_p, t_p)
        xp, st = _mixer(gp, xp, lw, None, 0.0, attend_p)
        xp, conv_new = _cross_and_ffn(gp, xp, lw, mk.reshape(1, b_p, mem_len, D_MODEL),
                                      mv.reshape(1, b_p, mem_len, D_MODEL), 0, conv0_p)
        new_p.append(st + (conv_new,))
        mem_k_new.append(mk.reshape(b_p, mem_len, N_HEADS, CA_DIM))
        mem_v_new.append(mv.reshape(b_p, mem_len, N_HEADS, CA_DIM))

        def attend_s(q_lat, q_pe, ckv, kpe, l=l):
            rows = t_s * N_HEADS
            ql = gs.to_btc(q_lat).reshape(b_s, rows, MLA_LORA)
            qp = gs.to_btc(q_pe).reshape(b_s, rows, MLA_ROPE)
            cn = jnp.pad(gs.to_btc(ckv), ((0, 0), (0, page - t_s), (0, 0)))
            pn = jnp.pad(gs.to_btc(kpe), ((0, 0), (0, page - t_s), (0, 0)))
            o = _mla_sample(page_table, ql, qp, cn, pn, cache_ckv, cache_kpe_t, l)
            return gs.from_btc(o.reshape(b_s, t_s, N_HEADS * MLA_LORA))

        st_s = (state_gla[l], state_rwkv[l], state_rwkv_shift[l], state_mlstm_c[l], state_mlstm_n[l], state_mlstm_m[l])
        xs, st = _mixer(gs, xs, lw, st_s, float(past_len), attend_s)
        conv_s = state_conv[l].transpose(1, 0, 2).reshape(1, 2 * b_s, D_FF)
        xs, conv_new = _cross_and_ffn(gs, xs, lw, mem_k_c, mem_v_c, l, conv_s)
        new_s.append(st + (conv_new.reshape(2, b_s, D_FF).transpose(1, 0, 2),))

    def stack(per_layer, i, grp):
        vals = [st[i] for st in per_layer]
        if i in (0, 1):
            vals = [grp.to_btc(v) for v in vals]
        return jnp.stack(vals, axis=0)

    outs = [gp.to_btc(xp), gs.to_btc(xs), stack(new_p, 0, gp), stack(new_p, 1, gp), stack(new_s, 0, gs), stack(new_s, 1, gs),
            jnp.stack(mem_k_new, axis=0), jnp.stack(mem_v_new, axis=0)]
    for i in range(2, 9):
        outs.append(stack(new_p, i, gp))
        outs.append(stack(new_s, i, gs))
    return tuple(outs)
```

```python
import functools
import math

import jax
import jax.numpy as jnp
from jax import lax
from jax.experimental import pallas as pl
from jax.experimental.pallas import tpu as pltpu

F32 = jnp.float32
BF16 = jnp.bfloat16

D_MODEL = 1024
BRANCH = 256
N_HEADS = 4
EPS = 1e-6
GLA_K = 32
GLA_TAU = 16.0
MLA_NOPE = 64
MLA_ROPE = 32
MLA_LORA = 128
ROPE_THETA = 10000.0
RWKV_LN_EPS = 64e-5
HEAD = 64
CA_DIM = 256
D_FF = 2816
FF_CHUNKS = 2

LANES = 128
SUBLANES = 8
VMEM_BUDGET = 56 * 1024 * 1024

W_GLA, W_MLA, W_RWKV, W_MLSTM, W_GATE = 896, 512, 1024, 1152, 4096
PIECES = (W_GLA, W_MLA, W_RWKV, W_MLSTM, W_GATE)


def _params(sem, vmem_bytes):
    return pltpu.CompilerParams(dimension_semantics=sem, vmem_limit_bytes=int(min(max(vmem_bytes, 16 << 20), VMEM_BUDGET)))


def _nbytes(shape, dtype=F32):
    return math.prod(shape) * jnp.dtype(dtype).itemsize


def _rms(x, g):
    return x * lax.rsqrt(jnp.mean(x * x, axis=-1, keepdims=True) + EPS) * g


def _bdot(a, b):
    return jnp.dot(a.astype(BF16), b.astype(BF16), preferred_element_type=F32)


def _bdot_t(a, b):
    return lax.dot_general(a.astype(BF16), b.astype(BF16), (((1,), (1,)), ((), ())), preferred_element_type=F32)


def _group_sum(x, ones_bd):
    hi = x.astype(BF16)
    lo = (x - hi.astype(F32)).astype(BF16)
    return jnp.dot(hi, ones_bd, preferred_element_type=F32) + jnp.dot(lo, ones_bd, preferred_element_type=F32)


def _softplus(x):
    return jnp.maximum(x, 0.0) + jnp.log1p(jnp.exp(-jnp.abs(x)))


def _log_sigmoid(x):
    return -_softplus(-x)


def _shift_rows(x, first, s):
    n = x.shape[0]
    if s % SUBLANES == 0:
        return jnp.concatenate([first, x[: n - s]], axis=0)
    rolled = pltpu.roll(x, s, 0)
    row = lax.broadcasted_iota(jnp.int32, x.shape, 0)
    for r in range(s):
        rolled = jnp.where(row == r, first[r : r + 1, :], rolled)
    return rolled


def _mix_in_kernel(x_ref, g_ref, w_ref, *outs):
    h = _rms(x_ref[...], g_ref[...]).astype(BF16)
    off = 0
    for o_ref, width in zip(outs, PIECES):
        piece = jnp.dot(h, w_ref[:, off : off + width], preferred_element_type=F32)
        if o_ref.dtype == BF16:
            piece = jax.nn.sigmoid(piece)
        o_ref[...] = piece.astype(o_ref.dtype)
        off += width


def _mix_in(x, g, w_all, tm):
    n = x.shape[0]
    total = sum(PIECES)
    vmem = 2 * _nbytes((tm, D_MODEL)) + 2 * _nbytes((D_MODEL, total), BF16) + 2 * _nbytes((tm, total)) + (4 << 20)
    return pl.pallas_call(
        _mix_in_kernel,
        grid=(n // tm,),
        in_specs=[
            pl.BlockSpec((tm, D_MODEL), lambda i: (i, 0)),
            pl.BlockSpec((1, D_MODEL), lambda i: (0, 0)),
            pl.BlockSpec((D_MODEL, total), lambda i: (0, 0)),
        ],
        out_specs=[pl.BlockSpec((tm, w), lambda i: (i, 0)) for w in PIECES],
        out_shape=[jax.ShapeDtypeStruct((n, w), BF16 if w == W_GATE else F32) for w in PIECES],
        compiler_params=_params(("parallel",), vmem),
    )(x, g, w_all)


def _prep_kernel(pg_ref, pr_ref, pm_ref, shift_ref,
                 gate2_ref, bgate_ref, mu_ref, w0_ref, w2_ref, a0_ref, a2_ref, g2_ref, kk_ref, ka_ref, rk_ref,
                 ones_ref, ibias_ref, fbias_ref,
                 gla_ref, rw_ref, aux_ref, ml_ref, carry, *, stride):
    i = pl.program_id(1)
    tm = pr_ref.shape[0]

    @pl.when(i == 0)
    def _():
        carry[...] = shift_ref[0]

    pr = pr_ref[...]
    prev = _shift_rows(pr, carry[...], stride)
    carry[...] = pr[tm - stride :, :]
    log_w, kk, a, k_mod, rr, rv, g, bonus = _rwkv_mix(
        pr, prev, mu_ref[...], w0_ref[...], w2_ref[...], a0_ref[...], a2_ref[...], g2_ref[...], kk_ref[...],
        ka_ref[...], rk_ref[...], ones_ref[...])
    rw_ref[:, 0:256] = jnp.exp(log_w)
    rw_ref[:, 256:512] = kk
    rw_ref[:, 512:768] = kk * a
    rw_ref[:, 768:1024] = k_mod
    rw_ref[:, 1024:1280] = rr
    rw_ref[:, 1280:1536] = rv
    aux_ref[:, 0:256] = g
    aux_ref[:, 256:512] = bonus

    pg = pg_ref[...]
    z = _bdot(pg[:, 768:896], gate2_ref[...]) + bgate_ref[...]
    gla_ref[:, 0:128] = pg[:, 0:128] * (GLA_K ** -0.5)
    gla_ref[:, 128:256] = jnp.exp(_log_sigmoid(z) / GLA_TAU)

    pml = pm_ref[...]
    ifp = pml[:, 1024:1152]
    lane = lax.broadcasted_iota(jnp.int32, ifp.shape, 1)
    ml_ref[:, 0:256] = pml[:, 0:256] * (HEAD ** -0.5)
    ml_ref[:, 256:384] = jnp.where(lane < N_HEADS, ifp + ibias_ref[...], _log_sigmoid(ifp + fbias_ref[...]))


def _prep(p_gla, p_rwkv, p_mlstm, shift0, wts, n_seq, tm, stride):
    n = p_rwkv.shape[0]
    n_tiles = n // n_seq // tm
    row = lambda w: pl.BlockSpec((tm, w), lambda q, i: (q * n_tiles + i, 0))
    const = lambda a: pl.BlockSpec(a.shape, lambda q, i: (0,) * a.ndim)
    out_w = (256, 1536, 512, 384)
    vmem = 2 * _nbytes((tm, W_GLA + W_RWKV + W_MLSTM + sum(out_w))) + 8 * _nbytes((tm, W_RWKV)) + (6 << 20)
    return pl.pallas_call(
        functools.partial(_prep_kernel, stride=stride),
        grid=(n_seq, n_tiles),
        in_specs=[row(W_GLA), row(W_RWKV), row(W_MLSTM),
                  pl.BlockSpec((1, stride, W_RWKV), lambda q, i: (q, 0, 0))] + [const(a) for a in wts],
        out_specs=[row(w) for w in out_w],
        out_shape=[jax.ShapeDtypeStruct((n, w), F32) for w in out_w],
        scratch_shapes=[pltpu.VMEM((stride, W_RWKV), F32)],
        compiler_params=_params(("arbitrary", "arbitrary"), vmem),
    )(p_gla, p_rwkv, p_mlstm, shift0, *wts)


def _rope32(x, cos2, sin2):
    half = MLA_ROPE // 2
    swapped = jnp.concatenate([x[:, half:], x[:, :half]], axis=1)
    return x * cos2 + swapped * sin2


def _mla_prep_kernel(p_ref, cos_ref, sin_ref, nq_ref, nkv_ref, wn_ref, wp_ref, wuk_ref,
                     qlat_ref, qpe_ref, ckv_ref, kpe_ref):
    p = p_ref[...]
    cos2, sin2 = cos_ref[...], sin_ref[...]
    cq = _rms(p[:, 0:256], nq_ref[...])
    q_nope = _bdot(cq, wn_ref[...])
    q_pe = _bdot(cq, wp_ref[...])
    qlat_ref[...] = _bdot(q_nope, wuk_ref[...])
    qpe_ref[...] = jnp.concatenate(
        [_rope32(q_pe[:, h * MLA_ROPE : (h + 1) * MLA_ROPE], cos2, sin2) for h in range(N_HEADS)], axis=1)
    ckv_ref[...] = _rms(p[:, 256:384], nkv_ref[...])
    kpe_ref[...] = _rope32(p[:, 384:416], cos2, sin2)


def _mla_prep(p_mla, cos2, sin2, wts, n_seq, tm):
    n = p_mla.shape[0]
    n_tiles = n // n_seq // tm
    row = lambda w: pl.BlockSpec((tm, w), lambda q, i: (q * n_tiles + i, 0))
    tab = pl.BlockSpec((tm, MLA_ROPE), lambda q, i: (i, 0))
    const = lambda a: pl.BlockSpec(a.shape, lambda q, i: (0,) * a.ndim)
    out_w = (N_HEADS * MLA_LORA, N_HEADS * MLA_ROPE, MLA_LORA, MLA_ROPE)
    vmem = 2 * _nbytes((tm, W_MLA + sum(out_w) + 2 * LANES)) + 6 * _nbytes((tm, 512)) + (4 << 20)
    return pl.pallas_call(
        _mla_prep_kernel,
        grid=(n_seq, n_tiles),
        in_specs=[row(W_MLA), tab, tab] + [const(a) for a in wts],
        out_specs=[row(w) for w in out_w],
        out_shape=[jax.ShapeDtypeStruct((n, w), F32) for w in out_w],
        compiler_params=_params(("parallel", "parallel"), vmem),
    )(p_mla, cos2, sin2, *wts)


MLA_SCALE = (MLA_NOPE + MLA_ROPE) ** -0.5


def _stack_heads(x, width):
    return jnp.concatenate([x[:, h * width : (h + 1) * width] for h in range(N_HEADS)], axis=0)


def _mla_prompt_kernel(ql_ref, qp_ref, ckv_ref, kpe_ref, o_ref, *, tq, tk):
    qi = pl.program_id(1)
    q = (_stack_heads(ql_ref[...], MLA_LORA) * MLA_SCALE).astype(BF16)
    qp = (_stack_heads(qp_ref[...], MLA_ROPE) * MLA_SCALE).astype(BF16)
    rows = N_HEADS * tq

    def block(kb, carry, causal):
        m, l, acc = carry
        k0 = pl.multiple_of(kb * tk, tk)
        kc = ckv_ref[pl.ds(k0, tk), :].astype(BF16)
        kp = kpe_ref[pl.ds(k0, tk), :].astype(BF16)
        s = _bdot_t(q, kc) + _bdot_t(qp, kp)
        if causal:
            tpos = lax.broadcasted_iota(jnp.int32, (tq, tk), 0)
            qpos = jnp.concatenate([tpos] * N_HEADS, axis=0)
            s = jnp.where(lax.broadcasted_iota(jnp.int32, (rows, tk), 1) <= qpos, s, -jnp.inf)
        m_new = jnp.maximum(m, jnp.max(s, axis=1, keepdims=True))
        p = jnp.exp(s - m_new)
        alpha = jnp.exp(m - m_new)
        l = alpha * l + jnp.sum(p, axis=1, keepdims=True)
        acc = alpha * acc + jnp.dot(p.astype(BF16), kc, preferred_element_type=F32)
        return m_new, l, acc

    init = (jnp.full((rows, 1), -jnp.inf, F32), jnp.zeros((rows, 1), F32), jnp.zeros((rows, MLA_LORA), F32))
    carry = lax.fori_loop(0, qi, functools.partial(block, causal=False), init)
    _, l, acc = block(qi, carry, causal=True)
    out = acc / l
    o_ref[...] = jnp.concatenate([out[h * tq : (h + 1) * tq, :] for h in range(N_HEADS)], axis=1)


def _mla_prompt(q_lat, q_pe, ckv, kpe, n_b, t_len):
    tq = tk = min(256, t_len)
    nq = t_len // tq
    vmem = 2 * _nbytes((tq, 1024 + 128)) + 2 * _nbytes((t_len, 256)) + 12 * _nbytes((N_HEADS * tq, tk)) + (4 << 20)
    return pl.pallas_call(
        functools.partial(_mla_prompt_kernel, tq=tq, tk=tk),
        grid=(n_b, nq),
        in_specs=[
            pl.BlockSpec((tq, N_HEADS * MLA_LORA), lambda b, i: (b * nq + i, 0)),
            pl.BlockSpec((tq, N_HEADS * MLA_ROPE), lambda b, i: (b * nq + i, 0)),
            pl.BlockSpec((t_len, MLA_LORA), lambda b, i: (b, 0)),
            pl.BlockSpec((t_len, MLA_ROPE), lambda b, i: (b, 0)),
        ],
        out_specs=pl.BlockSpec((tq, N_HEADS * MLA_LORA), lambda b, i: (b * nq + i, 0)),
        out_shape=jax.ShapeDtypeStruct((n_b * t_len, N_HEADS * MLA_LORA), F32),
        compiler_params=_params(("parallel", "parallel"), vmem),
    )(q_lat, q_pe, ckv, kpe)


def _mla_sample_kernel(pt_ref, ql_ref, qp_ref, cnew_ref, pnew_ref, *rest, n_pg, page):
    ckv_refs = rest[:n_pg]
    kpe_refs = rest[n_pg : 2 * n_pg]
    o_ref = rest[2 * n_pg]
    m_s, l_s, acc_s = rest[2 * n_pg + 1 :]
    g = pl.program_id(1)
    rows = ql_ref.shape[1]

    @pl.when(g == 0)
    def _():
        m_s[...] = jnp.full(m_s.shape, -jnp.inf, F32)
        l_s[...] = jnp.zeros(l_s.shape, F32)
        acc_s[...] = jnp.zeros(acc_s.shape, F32)

    q = (ql_ref[0] * MLA_SCALE).astype(BF16)
    qp = (qp_ref[0] * MLA_SCALE).astype(BF16)

    def update(s, v):
        m = m_s[...]
        m_new = jnp.maximum(m, jnp.max(s, axis=1, keepdims=True))
        alpha = jnp.exp(m - m_new)
        p = jnp.exp(s - m_new)
        m_s[...] = m_new
        l_s[...] = alpha * l_s[...] + jnp.sum(p, axis=1, keepdims=True)
        acc_s[...] = alpha * acc_s[...] + jnp.dot(p.astype(BF16), v, preferred_element_type=F32)

    kc = jnp.concatenate([c_ref[0, 0].astype(BF16) for c_ref in ckv_refs], axis=0)
    kp_t = jnp.concatenate([p_ref[0, 0].astype(BF16) for p_ref in kpe_refs], axis=1)
    update(_bdot_t(q, kc) + jnp.dot(qp, kp_t, preferred_element_type=F32), kc)

    @pl.when(g == pl.num_programs(1) - 1)
    def _():
        kn = cnew_ref[0].astype(BF16)
        kpn = pnew_ref[0].astype(BF16)
        s = _bdot_t(q, kn) + _bdot_t(qp, kpn)
        row = lax.broadcasted_iota(jnp.int32, (rows, page), 0)
        tk = lax.broadcasted_iota(jnp.int32, (rows, page), 1)
        s = jnp.where(tk * N_HEADS <= row, s, -jnp.inf)
        update(s, kn)
        o_ref[0] = acc_s[...] / l_s[...]


def _mla_sample(page_table, q_lat, q_pe, ckv_new, kpe_new, cache_ckv, cache_kpe_t, layer):
    n_b, n_pages = page_table.shape
    page = cache_ckv.shape[2]
    rows = q_lat.shape[1]
    n_pg = math.gcd(n_pages, 32)
    n_groups = n_pages // n_pg
    pt_flat = page_table.reshape(-1)

    def page_spec(shape, p):
        return pl.BlockSpec((1, 1) + shape, lambda b, g, pt: (layer, pt[b * n_pages + g * n_pg + p], 0, 0))

    per_b = lambda shape: pl.BlockSpec((1,) + shape, lambda b, g, pt: (b, 0, 0))
    grid_spec = pltpu.PrefetchScalarGridSpec(
        num_scalar_prefetch=1,
        grid=(n_b, n_groups),
        in_specs=[per_b((rows, MLA_LORA)), per_b((rows, MLA_ROPE)), per_b((page, MLA_LORA)), per_b((page, MLA_ROPE))]
        + [page_spec((page, MLA_LORA), p) for p in range(n_pg)] + [page_spec((MLA_ROPE, page), p) for p in range(n_pg)],
        out_specs=per_b((rows, MLA_LORA)),
        scratch_shapes=[pltpu.VMEM((rows, 1), F32), pltpu.VMEM((rows, 1), F32), pltpu.VMEM((rows, MLA_LORA), F32)],
    )
    vmem = 2 * n_pg * 2 * _nbytes((page, LANES)) + 4 * _nbytes((page, 2 * LANES)) + (8 << 20)
    return pl.pallas_call(
        functools.partial(_mla_sample_kernel, n_pg=n_pg, page=page),
        grid_spec=grid_spec,
        out_shape=jax.ShapeDtypeStruct((n_b, rows, MLA_LORA), F32),
        compiler_params=_params(("parallel", "arbitrary"), vmem),
    )(pt_flat, q_lat, q_pe, ckv_new, kpe_new, *([cache_ckv] * n_pg), *([cache_kpe_t] * n_pg))


V_TILES = HEAD // SUBLANES


def _scan_kernel(gk_ref, rk_ref, mk_ref, v_ref, sg0, sr0, sc0, sn0, sm0,
                 o_ref, sg_o, sr_o, sc_o, sn_o, sm_o, sg, sr, sc, sn, sm, *, tb, ksg, ks, groups):
    i = pl.program_id(1)

    @pl.when(i == 0)
    def _():
        sg[...] = sg0[...]
        sr[...] = sr0[...]
        sc[...] = sc0[...]
        sn[...] = sn0[...]
        sm[...] = sm0[...]

    def fold(x):
        return x + pltpu.roll(x, LANES // 2, 1) if groups == 2 else x

    def vt_slice(vt):
        return slice(vt * SUBLANES, (vt + 1) * SUBLANES)

    def rwkv_sa(t):
        acc = [jnp.zeros((SUBLANES, LANES), F32)] * V_TILES
        for k in range(ks):
            kk = rk_ref[t, ks + k : ks + k + 1, :]
            for vt in range(V_TILES):
                acc[vt] = acc[vt] + sr[k, vt_slice(vt), :] * kk
        return tuple(fold(a) for a in acc)

    def step(t, sa):
        tn = jnp.minimum(t + 1, tb - 1)
        zeros = [jnp.zeros((SUBLANES, LANES), F32)] * V_TILES

        gv = [v_ref[t, vt * SUBLANES : (vt + 1) * SUBLANES, :] for vt in range(V_TILES)]
        o = list(zeros)
        for k in range(ksg):
            q = gk_ref[t, k : k + 1, :]
            kr = gk_ref[t, ksg + k : ksg + k + 1, :]
            a = gk_ref[t, 2 * ksg + k : 2 * ksg + k + 1, :]
            for vt in range(V_TILES):
                new = sg[k, vt_slice(vt), :] * a + gv[vt] * kr
                sg[k, vt_slice(vt), :] = new
                o[vt] = o[vt] + new * q
        for vt in range(V_TILES):
            o_ref[t, vt_slice(vt), :] = fold(o[vt])

        rv = [v_ref[t, HEAD + vt * SUBLANES : HEAD + (vt + 1) * SUBLANES, :] for vt in range(V_TILES)]
        y = list(zeros)
        san = list(zeros)
        for k in range(ks):
            w = rk_ref[t, k : k + 1, :]
            b = rk_ref[t, 2 * ks + k : 2 * ks + k + 1, :]
            kr = rk_ref[t, 3 * ks + k : 3 * ks + k + 1, :]
            r = rk_ref[t, 4 * ks + k : 4 * ks + k + 1, :]
            kkn = rk_ref[tn, ks + k : ks + k + 1, :]
            for vt in range(V_TILES):
                new = sr[k, vt_slice(vt), :] * w - sa[vt] * b + rv[vt] * kr
                sr[k, vt_slice(vt), :] = new
                y[vt] = y[vt] + new * r
                san[vt] = san[vt] + new * kkn
        for vt in range(V_TILES):
            o_ref[t, HEAD + vt * SUBLANES : HEAD + (vt + 1) * SUBLANES, :] = fold(y[vt])

        mv = [v_ref[t, 2 * HEAD + vt * SUBLANES : 2 * HEAD + (vt + 1) * SUBLANES, :] for vt in range(V_TILES)]
        i_t = v_ref[t, 3 * HEAD : 3 * HEAD + 1, :]
        lf = v_ref[t, 3 * HEAD + 1 : 3 * HEAD + 2, :]
        m_old = sm[0:1, :]
        m_new = jnp.maximum(lf + m_old, i_t)
        fs = jnp.exp(lf + m_old - m_new)
        isc = jnp.exp(i_t - m_new)
        sm[0:1, :] = m_new
        q_tile = mk_ref[t, 0:ks, :]
        n_new = fs * sn[...] + isc * mk_ref[t, ks : 2 * ks, :]
        sn[...] = n_new
        den = fold(jnp.sum(n_new * q_tile, axis=0, keepdims=True))
        fs_b = jnp.broadcast_to(fs, (SUBLANES, LANES))
        num = list(zeros)
        for k in range(ks):
            q = mk_ref[t, k : k + 1, :]
            ik = isc * mk_ref[t, ks + k : ks + k + 1, :]
            for vt in range(V_TILES):
                new = sc[k, vt_slice(vt), :] * fs_b + mv[vt] * ik
                sc[k, vt_slice(vt), :] = new
                num[vt] = num[vt] + new * q
        scale = 1.0 / jnp.maximum(jnp.abs(den), jnp.exp(-m_new))
        for vt in range(V_TILES):
            o_ref[t, 2 * HEAD + vt * SUBLANES : 2 * HEAD + (vt + 1) * SUBLANES, :] = fold(num[vt]) * scale
        return tuple(fold(x) for x in san)

    lax.fori_loop(0, tb, step, rwkv_sa(0))

    @pl.when(i == pl.num_programs(1) - 1)
    def _():
        sg_o[...] = sg[...]
        sr_o[...] = sr[...]
        sc_o[...] = sc[...]
        sn_o[...] = sn[...]
        sm_o[...] = sm[...]


def _scan(gk, rk, mk, vops, states, t_len, groups):
    ks = HEAD // groups
    ksg = GLA_K // groups
    n_lanes = gk.shape[2]
    n_lt = n_lanes // LANES
    tb = min(32, t_len)
    op = lambda rows: pl.BlockSpec((tb, rows, LANES), lambda j, i: (i, 0, j))
    st3 = lambda k: pl.BlockSpec((k, HEAD, LANES), lambda j, i: (0, 0, j))
    st2 = lambda k: pl.BlockSpec((k, LANES), lambda j, i: (0, j))
    v_rows = vops.shape[1]
    st_specs = [st3(ksg), st3(ks), st3(ks), st2(ks), st2(SUBLANES)]
    st_shapes = [(ksg, HEAD, n_lanes), (ks, HEAD, n_lanes), (ks, HEAD, n_lanes), (ks, n_lanes), (SUBLANES, n_lanes)]
    blk = _nbytes((tb, 3 * ksg + 5 * ks + 2 * ks + v_rows + 3 * HEAD, LANES))
    st_bytes = _nbytes(((ksg + 2 * ks) * HEAD + ks + SUBLANES, LANES))
    return pl.pallas_call(
        functools.partial(_scan_kernel, tb=tb, ksg=ksg, ks=ks, groups=groups),
        grid=(n_lt, t_len // tb),
        in_specs=[op(3 * ksg), op(5 * ks), op(2 * ks), op(v_rows)] + st_specs,
        out_specs=[op(3 * HEAD)] + st_specs,
        out_shape=[jax.ShapeDtypeStruct((t_len, 3 * HEAD, n_lanes), F32)]
        + [jax.ShapeDtypeStruct(s, F32) for s in st_shapes],
        scratch_shapes=[pltpu.VMEM((ksg, HEAD, LANES), F32), pltpu.VMEM((ks, HEAD, LANES), F32),
                        pltpu.VMEM((ks, HEAD, LANES), F32), pltpu.VMEM((ks, LANES), F32),
                        pltpu.VMEM((SUBLANES, LANES), F32)],
        compiler_params=_params(("parallel", "arbitrary"), 2 * blk + 5 * st_bytes + (4 << 20)),
    )(gk, rk, mk, vops, *states)


def _head_ln(x, ones_bd, eps):
    mean = _group_sum(x, ones_bd) * (1.0 / HEAD)
    xc = x - mean
    var = _group_sum(xc * xc, ones_bd) * (1.0 / HEAD)
    return xc * lax.rsqrt(var + eps)


def _gla_out(o, gate_r, ones_bd, gnorm):
    ms = _group_sum(o * o, ones_bd) * (1.0 / HEAD)
    return o * lax.rsqrt(ms + EPS) * gnorm * jax.nn.silu(gate_r)


def _rwkv_out(y, bonus, g, ones_bd, lnw, lnb):
    return (_head_ln(y, ones_bd, RWKV_LN_EPS) * lnw + lnb + bonus) * g


def _mlstm_out(h, gate_o, ones_bd, mnorm):
    return _head_ln(h, ones_bd, EPS) * mnorm * jax.nn.sigmoid(gate_o)


def _post_kernel(scan_ref, pg_ref, aux_ref, pm_ref, ones_ref, gnorm_ref, lnw_ref, lnb_ref, mnorm_ref,
                 og_ref, or_ref, om_ref):
    ones_bd = ones_ref[...]
    sc = scan_ref[...]
    aux = aux_ref[...]
    og_ref[...] = _gla_out(sc[:, 0:256], pg_ref[:, 512:768], ones_bd, gnorm_ref[...])
    or_ref[...] = _rwkv_out(sc[:, 256:512], aux[:, 256:512], aux[:, 0:256], ones_bd, lnw_ref[...], lnb_ref[...])
    om_ref[...] = _mlstm_out(sc[:, 512:768], pm_ref[:, 768:1024], ones_bd, mnorm_ref[...])


def _post(scan_rows, p_gla, aux, p_mlstm, wts, tm):
    n = scan_rows.shape[0]
    row = lambda w: pl.BlockSpec((tm, w), lambda i: (i, 0))
    const = lambda a: pl.BlockSpec(a.shape, lambda i: (0,) * a.ndim)
    widths = (768, W_GLA, 512, W_MLSTM)
    vmem = 2 * _nbytes((tm, sum(widths) + 768)) + 12 * _nbytes((tm, BRANCH)) + (4 << 20)
    return pl.pallas_call(
        _post_kernel,
        grid=(n // tm,),
        in_specs=[row(w) for w in widths] + [const(a) for a in wts],
        out_specs=[row(BRANCH)] * 3,
        out_shape=[jax.ShapeDtypeStruct((n, BRANCH), F32)] * 3,
        compiler_params=_params(("parallel",), vmem),
    )(scan_rows, p_gla, aux, p_mlstm, *wts)


def _merge_kernel(x_ref, og_ref, olat_ref, or_ref, om_ref, gate_ref, wuv_ref, wbr_ref, wout_ref, gpost_ref, o_ref):
    o_mla = _bdot(olat_ref[...], wuv_ref[...])
    mixed = jnp.zeros((x_ref.shape[0], D_MODEL), F32)
    for n, br in enumerate((og_ref[...], o_mla, or_ref[...], om_ref[...])):
        up = _bdot(br, wbr_ref[n])
        mixed = mixed + gate_ref[:, n * D_MODEL : (n + 1) * D_MODEL].astype(F32) * up
    out = _bdot(mixed, wout_ref[...])
    o_ref[...] = x_ref[...] + _rms(out, gpost_ref[...])


def _merge(x, o_gla, o_lat, o_rwkv, o_mlstm, p_gate, wts, tm):
    n = x.shape[0]
    row = lambda w: pl.BlockSpec((tm, w), lambda i: (i, 0))
    const = lambda a: pl.BlockSpec(a.shape, lambda i: (0,) * a.ndim)
    widths = (D_MODEL, BRANCH, 512, BRANCH, BRANCH, W_GATE)
    vmem = 2 * _nbytes((tm, sum(widths) + D_MODEL)) + 10 * _nbytes((tm, D_MODEL)) + (12 << 20)
    return pl.pallas_call(
        _merge_kernel,
        grid=(n // tm,),
        in_specs=[row(w) for w in widths] + [const(a) for a in wts],
        out_specs=row(D_MODEL),
        out_shape=jax.ShapeDtypeStruct((n, D_MODEL), F32),
        compiler_params=_params(("parallel",), vmem),
    )(x, o_gla, o_lat, o_rwkv, o_mlstm, p_gate, *wts)


GLA_CHUNK = 32
MIX_CHUNK = 64


def _bdot_tn(a, b):
    return lax.dot_general(a.astype(BF16), b.astype(BF16), (((0,), (0,)), ((), ())), preferred_element_type=F32)


def _cumsum_rows(tri, x):
    hi = x.astype(BF16)
    lo = (x - hi.astype(F32)).astype(BF16)
    return jnp.dot(tri, hi, preferred_element_type=F32) + jnp.dot(tri, lo, preferred_element_type=F32)


def _rep_rows(x, mask):
    return jnp.concatenate([x] * N_HEADS, axis=0) * mask


def _gla_chunk_kernel(pg_ref, gate2_ref, bgate_ref, gnorm_ref, ones_ref, tri_ref, causal_ref, kmask_ref, vmask_ref,
                      smask_ref, o_ref, st_ref, st, *, tc):
    i = pl.program_id(1)

    @pl.when(i == 0)
    def _():
        st[...] = jnp.zeros(st.shape, F32)

    L = GLA_CHUNK
    pg = pg_ref[...]
    log_a = _log_sigmoid(_bdot(pg[:, 768:896], gate2_ref[...]) + bgate_ref[...]) / GLA_TAU
    tri, causal = tri_ref[...], causal_ref[...]
    kmask, vmask, smask = kmask_ref[...], vmask_ref[...], smask_ref[...]
    outs = []
    for c in range(tc // L):
        rows = slice(c * L, (c + 1) * L)
        b = _cumsum_rows(tri, log_a[rows])
        b_last = b[L - 1 : L, :]
        q_dec = pg[rows, 0:128] * (GLA_K ** -0.5) * jnp.exp(b)
        k = pg[rows, 128:256]
        v = pg[rows, 256:512]
        k_rep = _rep_rows(k * jnp.exp(-b), kmask)
        v_rep = _rep_rows(v, vmask)
        att = _bdot_t(q_dec, k_rep) * causal
        s_t = st[...]
        outs.append(_bdot_t(q_dec, s_t) + _bdot(att, v_rep))
        st[...] = s_t * jnp.exp(b_last) + _bdot_tn(v, k * jnp.exp(b_last - b)) * smask
    o = jnp.concatenate(outs, axis=0)
    o_ref[...] = _gla_out(o, pg[:, 512:768], ones_ref[...], gnorm_ref[...])

    @pl.when(i == pl.num_programs(1) - 1)
    def _():
        st_ref[0] = st[...]


def _gla_chunk(p_gla, wts, n_b, t_len):
    tc = min(256, t_len)
    nt = t_len // tc
    const = lambda a: pl.BlockSpec(a.shape, lambda b, i: (0,) * a.ndim)
    vmem = 2 * _nbytes((tc, W_GLA + BRANCH)) + 12 * _nbytes((tc, BRANCH)) + (6 << 20)
    return pl.pallas_call(
        functools.partial(_gla_chunk_kernel, tc=tc),
        grid=(n_b, nt),
        in_specs=[pl.BlockSpec((tc, W_GLA), lambda b, i: (b * nt + i, 0))] + [const(a) for a in wts],
        out_specs=[pl.BlockSpec((tc, BRANCH), lambda b, i: (b * nt + i, 0)),
                   pl.BlockSpec((1, BRANCH, N_HEADS * GLA_K), lambda b, i: (b, 0, 0))],
        out_shape=[jax.ShapeDtypeStruct((n_b * t_len, BRANCH), F32),
                   jax.ShapeDtypeStruct((n_b, BRANCH, N_HEADS * GLA_K), F32)],
        scratch_shapes=[pltpu.VMEM((BRANCH, N_HEADS * GLA_K), F32)],
        compiler_params=_params(("parallel", "arbitrary"), vmem),
    )(p_gla, *wts)


def _mlstm_chunk_kernel(pm_ref, ibias_ref, fbias_ref, mnorm_ref, ones_ref, tri_ref,
                        o_ref, c_out, n_out, m_out, cs, ns, ms, *, tc):
    i = pl.program_id(1)

    @pl.when(i == 0)
    def _():
        cs[...] = jnp.zeros(cs.shape, F32)
        ns[...] = jnp.zeros(ns.shape, F32)
        ms[...] = jnp.zeros(ms.shape, F32)

    L = MIX_CHUNK
    pm = pm_ref[...]
    ifp = pm[:, 1024:1152]
    lane = lax.broadcasted_iota(jnp.int32, ifp.shape, 1)
    gates = jnp.where(lane < N_HEADS, ifp + ibias_ref[...], _log_sigmoid(ifp + fbias_ref[...]))
    tri = tri_ref[...]
    causal = lax.broadcasted_iota(jnp.int32, (L, L), 1) <= lax.broadcasted_iota(jnp.int32, (L, L), 0)
    lane_c = lax.broadcasted_iota(jnp.int32, (L, LANES), 1)
    row_t = lax.broadcasted_iota(jnp.int32, (LANES, L), 0)
    last_lane = lax.broadcasted_iota(jnp.int32, (1, L), 1) == L - 1

    def pick_col(x, j):
        return jnp.sum(jnp.where(lane_c == j, x, 0.0), axis=1, keepdims=True)

    def pick_row(x_t, j):
        return jnp.sum(jnp.where(row_t == j, x_t, 0.0), axis=0, keepdims=True)

    chunks = []
    for c in range(tc // L):
        rows = slice(c * L, (c + 1) * L)
        g_c = gates[rows]
        z = jnp.where(lane_c < N_HEADS, g_c, _cumsum_rows(tri, g_c))
        z_t = z.T
        heads = []
        for h in range(N_HEADS):
            cols = slice(h * HEAD, (h + 1) * HEAD)
            q = pm[rows, cols] * (HEAD ** -0.5)
            k = pm[rows, 256 + h * HEAD : 256 + (h + 1) * HEAD]
            v = pm[rows, 512 + h * HEAD : 512 + (h + 1) * HEAD]
            i_col, b_col = pick_col(z, h), pick_col(z, N_HEADS + h)
            i_row, b_row = pick_row(z_t, h), pick_row(z_t, N_HEADS + h)
            m0 = jnp.max(ms[h : h + 1, :], axis=1, keepdims=True)
            d = jnp.where(causal, b_col - b_row + i_row, -jnp.inf)
            inter = b_col + m0
            m_t = jnp.maximum(inter, jnp.max(d, axis=1, keepdims=True))
            w_inter = jnp.exp(inter - m_t)
            w_intra = jnp.exp(d - m_t) * _bdot_t(q, k)
            c_h, n_h = cs[h], ns[h : h + 1, :]
            num = w_inter * _bdot(q, c_h) + _bdot(w_intra, v)
            den = w_inter * jnp.sum(q * n_h, axis=1, keepdims=True) + jnp.sum(w_intra, axis=1, keepdims=True)
            heads.append(num / jnp.maximum(jnp.abs(den), jnp.exp(-m_t)))
            b_last = jnp.sum(jnp.where(last_lane, b_row, 0.0), axis=1, keepdims=True)
            m_new = jnp.maximum(b_last + m0, jnp.max(b_last - b_row + i_row, axis=1, keepdims=True))
            scale_old = jnp.exp(b_last + m0 - m_new)
            kw = k * jnp.exp(b_last - b_col + i_col - m_new)
            cs[h] = scale_old * c_h + _bdot_tn(kw, v)
            ns[h : h + 1, :] = scale_old * n_h + jnp.sum(kw, axis=0, keepdims=True)
            ms[h : h + 1, :] = jnp.broadcast_to(m_new, (1, LANES))
        chunks.append(jnp.concatenate(heads, axis=1))
    hm = jnp.concatenate(chunks, axis=0)
    o_ref[...] = _mlstm_out(hm, pm[:, 768:1024], ones_ref[...], mnorm_ref[...])

    @pl.when(i == pl.num_programs(1) - 1)
    def _():
        c_out[0] = cs[...]
        n_out[0] = ns[...]
        m_out[0] = ms[...]


def _mlstm_chunk(p_mlstm, wts, n_b, t_len):
    tc = min(256, t_len)
    nt = t_len // tc
    const = lambda a: pl.BlockSpec(a.shape, lambda b, i: (0,) * a.ndim)
    vmem = 2 * _nbytes((tc, W_MLSTM + BRANCH)) + 12 * _nbytes((tc, BRANCH)) + (6 << 20)
    return pl.pallas_call(
        functools.partial(_mlstm_chunk_kernel, tc=tc),
        grid=(n_b, nt),
        in_specs=[pl.BlockSpec((tc, W_MLSTM), lambda b, i: (b * nt + i, 0))] + [const(a) for a in wts],
        out_specs=[pl.BlockSpec((tc, BRANCH), lambda b, i: (b * nt + i, 0)),
                   pl.BlockSpec((1, N_HEADS, HEAD, HEAD), lambda b, i: (b, 0, 0, 0)),
                   pl.BlockSpec((1, SUBLANES, HEAD), lambda b, i: (b, 0, 0)),
                   pl.BlockSpec((1, SUBLANES, LANES), lambda b, i: (b, 0, 0))],
        out_shape=[jax.ShapeDtypeStruct((n_b * t_len, BRANCH), F32),
                   jax.ShapeDtypeStruct((n_b, N_HEADS, HEAD, HEAD), F32),
                   jax.ShapeDtypeStruct((n_b, SUBLANES, HEAD), F32),
                   jax.ShapeDtypeStruct((n_b, SUBLANES, LANES), F32)],
        scratch_shapes=[pltpu.VMEM((N_HEADS, HEAD, HEAD), F32), pltpu.VMEM((SUBLANES, HEAD), F32),
                        pltpu.VMEM((SUBLANES, LANES), F32)],
        compiler_params=_params(("parallel", "arbitrary"), vmem),
    )(p_mlstm, *wts)


def _rwkv_mix(pr, prev, mu, w0, w2, a0, a2, g2, kkw, ka, rkw, ones_bd):
    pm = pr + (prev - pr) * mu
    rr, rk, rv = pm[:, 0:256], pm[:, 256:512], pm[:, 512:768]
    wlo, alo, rglo = pm[:, 768:832], pm[:, 832:896], pm[:, 896:1024]
    log_w = -jnp.exp(-_softplus(-(w0 + _bdot(jnp.tanh(wlo), w2))) - 0.5)
    a = jax.nn.sigmoid(a0 + _bdot(alo, a2))
    g = _bdot(jax.nn.sigmoid(rglo), g2)
    kkr = rk * kkw
    kk = kkr / jnp.maximum(jnp.sqrt(_group_sum(kkr * kkr, ones_bd)), 1e-12)
    k_mod = rk * (1.0 + (a - 1.0) * ka)
    bonus = _group_sum(rr * k_mod * rkw, ones_bd) * rv
    return log_w, kk, a, k_mod, rr, rv, g, bonus


def _rwkv_chunk_kernel(pr_ref, mu_ref, w0_ref, w2_ref, a0_ref, a2_ref, g2_ref, kk_ref, ka_ref, rk_ref, ones_ref,
                       lnw_ref, lnb_ref, tri_ref, strict_ref, incl_ref, o_ref, s_out, carry, sv, *, tc):
    i = pl.program_id(1)

    @pl.when(i == 0)
    def _():
        carry[...] = jnp.zeros(carry.shape, F32)
        sv[...] = jnp.zeros(sv.shape, F32)

    L = MIX_CHUNK
    pr = pr_ref[...]
    prev = _shift_rows(pr, carry[...], 1)
    carry[...] = pr[tc - 1 :, :]
    ones_bd = ones_ref[...]
    log_w, kk, a, k_mod, rr, rv, g, bonus = _rwkv_mix(
        pr, prev, mu_ref[...], w0_ref[...], w2_ref[...], a0_ref[...], a2_ref[...], g2_ref[...], kk_ref[...],
        ka_ref[...], rk_ref[...], ones_bd)
    bd = ones_bd.astype(F32)
    tri, strict, incl = tri_ref[...], strict_ref[...], incl_ref[...]
    outs = []
    for c in range(tc // L):
        rows = slice(c * L, (c + 1) * L)
        lw = log_w[rows]
        cum = _cumsum_rows(tri, lw)
        end = jnp.exp(cum[L - 1 : L, :] - cum)
        w_inc, w_inv = jnp.exp(cum), jnp.exp(-cum)
        kk_c, v_c = kk[rows], rv[rows]
        ka_c = kk_c * a[rows]
        al = -kk_c * jnp.exp(cum - lw)
        rt = rr[rows] * w_inc
        be, kt = ka_c * w_inv, k_mod[rows] * w_inv
        gm = _bdot_t(jnp.concatenate([al, rt], axis=0),
                     jnp.concatenate([_rep_rows(be, bd), _rep_rows(kt, bd)], axis=0))
        a_ab = jnp.where(strict > 0, gm[0:L, 0:256], 0.0)
        a_ak = jnp.where(strict > 0, gm[0:L, 256:512], 0.0)
        r_b = jnp.where(incl > 0, gm[L : 2 * L, 0:256], 0.0)
        r_k = jnp.where(incl > 0, gm[L : 2 * L, 256:512], 0.0)
        t_cols = (incl - strict) + a_ab
        p_cols = _bdot(a_ab, _rep_rows(a_ab, bd))
        for _ in range(4):
            prod = _bdot(jnp.concatenate([t_cols, p_cols], axis=0), _rep_rows(p_cols, bd))
            t_cols = t_cols + prod[0:L]
            p_cols = prod[L : 2 * L]
        t_cols = t_cols + _bdot(t_cols, _rep_rows(p_cols, bd))
        s_vk = sv[...]
        v_rep = _rep_rows(v_c, bd)
        x = _bdot_t(al, s_vk) + _bdot(a_ak, v_rep)
        u = _bdot(t_cols, _rep_rows(x, bd))
        outs.append(_bdot_t(rt, s_vk) + _bdot(r_b, _rep_rows(u, bd)) + _bdot(r_k, v_rep))
        sv[...] = s_vk * jnp.exp(cum[L - 1 : L, :]) + _bdot_tn(
            jnp.concatenate([u, v_c], axis=0), jnp.concatenate([ka_c * end, k_mod[rows] * end], axis=0)) * bd
    y = jnp.concatenate(outs, axis=0)
    o_ref[...] = _rwkv_out(y, bonus, g, ones_bd, lnw_ref[...], lnb_ref[...])

    @pl.when(i == pl.num_programs(1) - 1)
    def _():
        s_out[0] = sv[...]


def _rwkv_chunk(p_rwkv, wts, n_b, t_len):
    tc = min(256, t_len)
    nt = t_len // tc
    const = lambda a: pl.BlockSpec(a.shape, lambda b, i: (0,) * a.ndim)
    vmem = 2 * _nbytes((tc, W_RWKV + BRANCH)) + 24 * _nbytes((tc, BRANCH)) + 16 * _nbytes((BRANCH, BRANCH)) + (6 << 20)
    return pl.pallas_call(
        functools.partial(_rwkv_chunk_kernel, tc=tc),
        grid=(n_b, nt),
        in_specs=[pl.BlockSpec((tc, W_RWKV), lambda b, i: (b * nt + i, 0))] + [const(a) for a in wts],
        out_specs=[pl.BlockSpec((tc, BRANCH), lambda b, i: (b * nt + i, 0)),
                   pl.BlockSpec((1, BRANCH, BRANCH), lambda b, i: (b, 0, 0))],
        out_shape=[jax.ShapeDtypeStruct((n_b * t_len, BRANCH), F32), jax.ShapeDtypeStruct((n_b, BRANCH, BRANCH), F32)],
        scratch_shapes=[pltpu.VMEM((1, W_RWKV), F32), pltpu.VMEM((BRANCH, BRANCH), F32)],
        compiler_params=_params(("parallel", "arbitrary"), vmem),
    )(p_rwkv, *wts)


def _norm_matmul_kernel(x_ref, g_ref, w_ref, *outs):
    h = _rms(x_ref[...], g_ref[...]).astype(BF16)
    off = 0
    for o_ref in outs:
        width = o_ref.shape[1]
        o_ref[...] = jnp.dot(h, w_ref[:, off : off + width], preferred_element_type=F32)
        off += width


def _norm_matmul(x, g, w, widths, tm):
    n = x.shape[0]
    total = sum(widths)
    vmem = 2 * _nbytes((tm, D_MODEL + total)) + 2 * _nbytes((D_MODEL, total), BF16) + (6 << 20)
    return pl.pallas_call(
        _norm_matmul_kernel,
        grid=(n // tm,),
        in_specs=[pl.BlockSpec((tm, D_MODEL), lambda i: (i, 0)), pl.BlockSpec((1, D_MODEL), lambda i: (0, 0)),
                  pl.BlockSpec((D_MODEL, total), lambda i: (0, 0))],
        out_specs=[pl.BlockSpec((tm, w_), lambda i: (i, 0)) for w_ in widths],
        out_shape=[jax.ShapeDtypeStruct((n, w_), F32) for w_ in widths],
        compiler_params=_params(("parallel",), vmem),
    )(x, g, w)


def _proj_res_kernel(x_ref, a_ref, w_ref, g_ref, o_ref):
    o_ref[...] = x_ref[...] + _rms(_bdot(a_ref[...], w_ref[...]), g_ref[...])


def _proj_res(x, a, w, g, tm):
    n = x.shape[0]
    row = pl.BlockSpec((tm, D_MODEL), lambda i: (i, 0))
    vmem = 6 * _nbytes((tm, D_MODEL)) + 2 * _nbytes((D_MODEL, D_MODEL), BF16) + (6 << 20)
    return pl.pallas_call(
        _proj_res_kernel,
        grid=(n // tm,),
        in_specs=[row, row, pl.BlockSpec((D_MODEL, D_MODEL), lambda i: (0, 0)), pl.BlockSpec((1, D_MODEL), lambda i: (0, 0))],
        out_specs=row,
        out_shape=jax.ShapeDtypeStruct((n, D_MODEL), F32),
        compiler_params=_params(("parallel",), vmem),
    )(x, a, w, g)


def _cross_attn_kernel(q_ref, k_ref, v_ref, o_ref):
    q = q_ref[...]
    outs = []
    for h in range(N_HEADS):
        cols = slice(h * CA_DIM, (h + 1) * CA_DIM)
        s = _bdot_t(q[:, cols], k_ref[:, cols]) * (CA_DIM ** -0.5)
        s = s - jnp.max(s, axis=1, keepdims=True)
        p = jnp.exp(s)
        p = p / jnp.sum(p, axis=1, keepdims=True)
        outs.append(_bdot(p, v_ref[:, cols]))
    o_ref[...] = jnp.concatenate(outs, axis=1)


def _cross_attn(q, mem_k, mem_v, layer, tq):
    n_b, t_len, _ = q.shape
    mem = mem_k.shape[2]
    qspec = pl.BlockSpec((None, tq, D_MODEL), lambda b, i: (b, i, 0))
    mspec = pl.BlockSpec((None, None, mem, D_MODEL), lambda b, i: (layer, b, 0, 0))
    vmem = 4 * _nbytes((tq, D_MODEL)) + 4 * _nbytes((mem, D_MODEL)) + 8 * _nbytes((tq, D_MODEL)) + (6 << 20)
    return pl.pallas_call(
        _cross_attn_kernel,
        grid=(n_b, t_len // tq),
        in_specs=[qspec, mspec, mspec],
        out_specs=qspec,
        out_shape=jax.ShapeDtypeStruct(q.shape, F32),
        compiler_params=_params(("parallel", "parallel"), vmem),
    )(q, mem_k, mem_v)


def _gelu(x):
    return 0.5 * x * (1.0 + lax.erf(x * (2.0 ** -0.5)))


def _ffn_kernel(x_ref, gpre_ref, wa_ref, wg_ref, cw_ref, cb_ref, wd_ref, st_ref, gpost_ref,
                o_ref, cnew_ref, hn, acc, carry, *, stride):
    i = pl.program_id(1)
    c = pl.program_id(2)
    tm = x_ref.shape[0]

    @pl.when(c == 0)
    def _():
        hn[...] = _rms(x_ref[...], gpre_ref[...]).astype(BF16)
        acc[...] = jnp.zeros(acc.shape, F32)

    @pl.when(i == 0)
    def _():
        carry[c] = st_ref[0]

    h = hn[...]
    a = jnp.dot(h, wa_ref[...], preferred_element_type=F32)
    gate = jnp.dot(h, wg_ref[...], preferred_element_type=F32)
    st = carry[c]
    prev1 = _shift_rows(a, st[stride:, :], stride)
    prev2 = _shift_rows(a, st, 2 * stride)
    cw = cw_ref[...]
    conv = cb_ref[...] + prev2 * cw[0:1, :] + prev1 * cw[1:2, :] + a * cw[2:3, :]
    tail = a[tm - 2 * stride :, :]
    carry[c] = tail
    fc = a.shape[1]
    for cc in range(FF_CHUNKS):
        @pl.when((c == cc) & (i == pl.num_programs(1) - 1))
        def _(cc=cc):
            cnew_ref[0, :, cc * fc : (cc + 1) * fc] = tail
    acc[...] += _bdot(_gelu(conv) * gate, wd_ref[...])

    @pl.when(c == pl.num_programs(2) - 1)
    def _():
        o_ref[...] = x_ref[...] + _rms(acc[...], gpost_ref[...])


def _ffn(x, gpre, wa, wg, cw, cb, wd, state, gpost, n_seq, tm, stride):
    n = x.shape[0]
    n_tiles = n // n_seq // tm
    fc = D_FF // FF_CHUNKS
    row = pl.BlockSpec((tm, D_MODEL), lambda q, i, c: (q * n_tiles + i, 0))
    vec = pl.BlockSpec((1, D_MODEL), lambda q, i, c: (0, 0))
    st = pl.BlockSpec((1, 2 * stride, fc), lambda q, i, c: (q, 0, c))
    vmem = (4 * _nbytes((tm, D_MODEL)) + 4 * _nbytes((D_MODEL, fc), BF16) + 2 * _nbytes((fc, D_MODEL), BF16)
            + 2 * _nbytes((tm, D_MODEL)) + 8 * _nbytes((tm, fc)) + 6 * _nbytes((2 * stride, fc)) + (4 << 20))
    return pl.pallas_call(
        functools.partial(_ffn_kernel, stride=stride),
        grid=(n_seq, n_tiles, FF_CHUNKS),
        in_specs=[row, vec,
                  pl.BlockSpec((D_MODEL, fc), lambda q, i, c: (0, c)),
                  pl.BlockSpec((D_MODEL, fc), lambda q, i, c: (0, c)),
                  pl.BlockSpec((3, fc), lambda q, i, c: (0, c)),
                  pl.BlockSpec((1, fc), lambda q, i, c: (0, c)),
                  pl.BlockSpec((fc, D_MODEL), lambda q, i, c: (c, 0)),
                  st, vec],
        out_specs=[row, pl.BlockSpec((1, 2 * stride, D_FF), lambda q, i, c: (q, 0, 0))],
        out_shape=[jax.ShapeDtypeStruct((n, D_MODEL), F32), jax.ShapeDtypeStruct((n_seq, 2 * stride, D_FF), F32)],
        scratch_shapes=[pltpu.VMEM((tm, D_MODEL), BF16), pltpu.VMEM((tm, D_MODEL), F32),
                        pltpu.VMEM((FF_CHUNKS, 2 * stride, fc), F32)],
        compiler_params=_params(("arbitrary", "arbitrary", "arbitrary"), vmem),
    )(x, gpre, wa, wg, cw, cb, wd, state, gpost)


def _pad_cols(w, width):
    return jnp.pad(w, ((0, 0), (0, width - w.shape[1])))


def _row(v):
    return v.reshape(1, -1).astype(F32)


def _block_diag(blocks):
    rows = sum(b.shape[0] for b in blocks)
    cols = sum(b.shape[1] for b in blocks)
    out = jnp.zeros((rows, cols), blocks[0].dtype)
    r = c = 0
    for b in blocks:
        out = lax.dynamic_update_slice(out, b, (r, c))
        r += b.shape[0]
        c += b.shape[1]
    return out


def _layer_weights(p, l):
    w = p["w_in"][l]
    gla, mla, rw, ml = w[:, 0:784], w[:, 784:1200], w[:, 1200:2224], w[:, 2224:3256]
    gate = w[:, 3256:]
    w_gla = jnp.concatenate([gla[:, 0:512], gla[:, 528:784], _pad_cols(gla[:, 512:528], LANES)], axis=1)
    w_mla = _pad_cols(mla, W_MLA)
    w_ml = jnp.concatenate([ml[:, 0:768], ml[:, 776:1032], _pad_cols(ml[:, 768:776], LANES)], axis=1)
    w_all = jnp.concatenate([w_gla, w_mla, rw, w_ml, gate], axis=1).astype(BF16)

    ones_bd = _block_diag([jnp.ones((HEAD, HEAD), BF16)] * N_HEADS)
    ibias = jnp.zeros((LANES,), F32).at[0:N_HEADS].set(p["mlstm_i_bias"][l])
    fbias = jnp.zeros((LANES,), F32).at[N_HEADS : 2 * N_HEADS].set(p["mlstm_f_bias"][l])
    prep = [
        jnp.pad(p["gla_w_gate2"][l], ((0, LANES - 16), (0, 0))).astype(BF16), _row(p["gla_b_gate"][l]),
        _row(p["rwkv_mu"][l]), _row(p["rwkv_w0"][l]), p["rwkv_w2"][l].astype(BF16), _row(p["rwkv_a0"][l]),
        p["rwkv_a2"][l].astype(BF16), p["rwkv_g2"][l].astype(BF16), _row(p["rwkv_k_k"][l]), _row(p["rwkv_k_a"][l]),
        _row(p["rwkv_r_k"][l]), ones_bd, _row(ibias), _row(fbias),
    ]
    wuq = p["mla_w_uq"][l].reshape(256, N_HEADS, MLA_NOPE + MLA_ROPE)
    mla_w = [
        _row(p["mla_norm_q"][l]), _row(p["mla_norm_kv"][l]),
        wuq[:, :, :MLA_NOPE].reshape(256, N_HEADS * MLA_NOPE).astype(BF16),
        wuq[:, :, MLA_NOPE:].reshape(256, N_HEADS * MLA_ROPE).astype(BF16),
        _block_diag([p["mla_w_uk"][l][h].T for h in range(N_HEADS)]).astype(BF16),
    ]
    gnorm = _row(jnp.tile(p["gla_norm"][l], N_HEADS))
    post = [ones_bd, gnorm, _row(p["rwkv_ln_w"][l]), _row(p["rwkv_ln_b"][l]), _row(p["mlstm_norm"][l])]
    merge = [
        _block_diag([p["mla_w_uv"][l][h] for h in range(N_HEADS)]).astype(BF16),
        p["w_branch"][l].astype(BF16), p["w_out"][l].astype(BF16), _row(p["norm_post_mix"][l]),
    ]
    tri = lambda n: jnp.tril(jnp.ones((n, n), F32))
    blocks = lambda r, c: _block_diag([jnp.ones((r, c), F32)] * N_HEADS)
    gla_c = prep[0:2] + [gnorm, ones_bd, tri(GLA_CHUNK).astype(BF16), jnp.tile(tri(GLA_CHUNK), (1, N_HEADS)),
                         blocks(GLA_CHUNK, GLA_K), blocks(GLA_CHUNK, HEAD), blocks(HEAD, GLA_K)]
    mlstm_c = [_row(ibias), _row(fbias), _row(p["mlstm_norm"][l]), ones_bd, tri(MIX_CHUNK).astype(BF16)]
    rwkv_c = prep[2:12] + [_row(p["rwkv_ln_w"][l]), _row(p["rwkv_ln_b"][l]), tri(MIX_CHUNK).astype(BF16),
                           jnp.tile(jnp.tril(jnp.ones((MIX_CHUNK, MIX_CHUNK), F32), -1), (1, N_HEADS)),
                           jnp.tile(tri(MIX_CHUNK), (1, N_HEADS))]
    up = p["ffn_w_up"][l]
    return dict(
        w_all=w_all, prep=prep, mla=mla_w, post=post, merge=merge, gla_c=gla_c, mlstm_c=mlstm_c, rwkv_c=rwkv_c,
        g_pre_mix=_row(p["norm_pre_mix"][l]),
        g_pre_ca=_row(p["norm_pre_ca"][l]), wq=p["ca_w_q"][l].astype(BF16), wo=p["ca_w_o"][l].astype(BF16),
        g_post_ca=_row(p["norm_post_ca"][l]),
        g_mem=_row(p["ca_norm_mem"][l]),
        wkv=jnp.concatenate([p["ca_w_k"][l], p["ca_w_v"][l]], axis=1).astype(BF16),
        g_pre_ffn=_row(p["norm_pre_ffn"][l]), wa=up[:, :D_FF].astype(BF16), wg=up[:, D_FF:].astype(BF16),
        cw=p["ffn_conv_w"][l].astype(F32), cb=_row(p["ffn_conv_b"][l]), wd=p["ffn_w_down"][l].astype(BF16),
        g_post_ffn=_row(p["norm_post_ffn"][l]),
    )


def _rope_tables(pos):
    half = MLA_ROPE // 2
    inv_freq = ROPE_THETA ** (-jnp.arange(half, dtype=F32) / half)
    ang = pos[:, None] * inv_freq[None, :]
    cos, sin = jnp.cos(ang), jnp.sin(ang)
    return jnp.concatenate([cos, cos], axis=1), jnp.concatenate([-sin, sin], axis=1)


class _Group:
    def __init__(self, n_b, t_len, time_major):
        self.n_b, self.t_len, self.time_major = n_b, t_len, time_major
        self.n_seq = 1 if time_major else n_b
        self.stride = n_b if time_major else 1
        self.rows = n_b * t_len
        self.tm = min(512, self.rows // self.n_seq)
        pairs = n_b * N_HEADS
        self.groups = max(1, LANES // pairs)
        assert not time_major or (self.groups in (1, 2) and (pairs * self.groups) % LANES == 0)

    def to_btc(self, x):
        c = x.shape[-1]
        if self.time_major:
            return x.reshape(self.t_len, self.n_b, c).transpose(1, 0, 2)
        return x.reshape(self.n_b, self.t_len, c)

    def from_btc(self, x):
        c = x.shape[-1]
        if self.time_major:
            return x.transpose(1, 0, 2).reshape(self.rows, c)
        return x.reshape(self.rows, c)

    def key_operand(self, x, dim):
        g = self.groups
        x = self.to_btc(x).reshape(self.n_b, self.t_len, N_HEADS, g, dim // g)
        return x.transpose(1, 4, 3, 0, 2).reshape(self.t_len, dim // g, g * self.n_b * N_HEADS)

    def val_operand(self, x, dim):
        x = self.to_btc(x).reshape(self.n_b, self.t_len, N_HEADS, dim)
        x = x.transpose(1, 3, 0, 2).reshape(self.t_len, dim, self.n_b * N_HEADS)
        return jnp.concatenate([x] * self.groups, axis=2)

    def scan_to_rows(self, o):
        pairs = self.n_b * N_HEADS
        o = o[:, :, :pairs].reshape(self.t_len, 3, HEAD, self.n_b, N_HEADS)
        return self.from_btc(o.transpose(3, 0, 1, 4, 2).reshape(self.n_b, self.t_len, 3 * BRANCH))

    def state_kv_in(self, s, key_axis_last):
        g = self.groups
        if key_axis_last:
            s = jnp.swapaxes(s, 2, 3)
        k, v = s.shape[2], s.shape[3]
        s = s.reshape(self.n_b, N_HEADS, g, k // g, v)
        return s.transpose(3, 4, 2, 0, 1).reshape(k // g, v, g * self.n_b * N_HEADS)

    def state_kv_out(self, s, key_axis_last):
        g = self.groups
        kg, v, _ = s.shape
        s = s.reshape(kg, v, g, self.n_b, N_HEADS).transpose(3, 4, 2, 0, 1).reshape(self.n_b, N_HEADS, g * kg, v)
        return jnp.swapaxes(s, 2, 3) if key_axis_last else s

    def state_k_in(self, s):
        g = self.groups
        k = s.shape[2]
        s = s.reshape(self.n_b, N_HEADS, g, k // g)
        return s.transpose(3, 2, 0, 1).reshape(k // g, g * self.n_b * N_HEADS)

    def state_k_out(self, s):
        g = self.groups
        kg = s.shape[0]
        return s.reshape(kg, g, self.n_b, N_HEADS).transpose(2, 3, 1, 0).reshape(self.n_b, N_HEADS, g * kg)

    def state_m_in(self, s):
        row = jnp.concatenate([s.reshape(1, self.n_b * N_HEADS)] * self.groups, axis=1)
        return jnp.concatenate([row, jnp.zeros((SUBLANES - 1, row.shape[1]), F32)], axis=0)

    def state_m_out(self, s):
        return s[0, : self.n_b * N_HEADS].reshape(self.n_b, N_HEADS)


def _diag_blocks(s, rows, cols):
    n_b = s.shape[0]
    s = s.reshape(n_b, N_HEADS, rows, N_HEADS, cols)
    return jnp.stack([s[:, h, :, h, :] for h in range(N_HEADS)], axis=1)


def _mixer(grp, x, lw, states, pos0, mla_attend):
    p_gla, p_mla, p_rwkv, p_mlstm, p_gate = _mix_in(x, lw["g_pre_mix"], lw["w_all"], min(256, grp.tm))

    pos = pos0 + jnp.arange(grp.t_len, dtype=F32)
    cos2, sin2 = _rope_tables(pos)
    if grp.time_major:
        cos2, sin2 = jnp.repeat(cos2, grp.n_b, axis=0), jnp.repeat(sin2, grp.n_b, axis=0)
    q_lat, q_pe, ckv, kpe = _mla_prep(p_mla, cos2, sin2, lw["mla"], grp.n_seq, grp.tm)
    o_lat = mla_attend(q_lat, q_pe, ckv, kpe)

    recur = _recurrent_scan if grp.time_major else _recurrent_chunked
    o_gla, o_rwkv, o_mlstm, rec_state = recur(grp, p_gla, p_rwkv, p_mlstm, lw, states)
    x_new = _merge(x, o_gla, o_lat, o_rwkv, o_mlstm, p_gate, lw["merge"], min(256, grp.tm))
    gla_new, rwkv_new, mc_new, mn_new, mm_new = rec_state
    shift_new = grp.to_btc(p_rwkv)[:, -1]
    return x_new, (ckv, kpe, gla_new, rwkv_new, shift_new, mc_new, mn_new, mm_new)


def _recurrent_chunked(grp, p_gla, p_rwkv, p_mlstm, lw, states):
    del states
    o_gla, s_gla = _gla_chunk(p_gla, lw["gla_c"], grp.n_b, grp.t_len)
    o_ml, c_new, n_new, m_new = _mlstm_chunk(p_mlstm, lw["mlstm_c"], grp.n_b, grp.t_len)
    o_rw, s_rw = _rwkv_chunk(p_rwkv, lw["rwkv_c"], grp.n_b, grp.t_len)
    gla_new = jnp.swapaxes(_diag_blocks(s_gla, HEAD, GLA_K), 2, 3)
    rwkv_new = _diag_blocks(s_rw, HEAD, HEAD)
    return o_gla, o_rw, o_ml, (gla_new, rwkv_new, c_new, n_new[:, :N_HEADS, :], m_new[:, :N_HEADS, 0])


def _recurrent_scan(grp, p_gla, p_rwkv, p_mlstm, lw, states):
    gla_s, rwkv_s, shift_s, mc_s, mn_s, mm_s = states
    shift0 = shift_s.reshape(grp.n_seq, grp.stride, W_RWKV)
    gla_ops, rw_ops, aux, ml_ops = _prep(p_gla, p_rwkv, p_mlstm, shift0, lw["prep"], grp.n_seq, grp.tm, grp.stride)
    g = grp.groups
    gk = jnp.concatenate([grp.key_operand(gla_ops[:, 0:128], GLA_K), grp.key_operand(p_gla[:, 128:256], GLA_K),
                          grp.key_operand(gla_ops[:, 128:256], GLA_K)], axis=1)
    rk = jnp.concatenate([grp.key_operand(rw_ops[:, c * 256 : (c + 1) * 256], HEAD) for c in range(5)], axis=1)
    mk = jnp.concatenate([grp.key_operand(ml_ops[:, 0:256], HEAD), grp.key_operand(p_mlstm[:, 256:512], HEAD)], axis=1)
    gates = grp.to_btc(ml_ops[:, 256 : 256 + 2 * N_HEADS]).reshape(grp.n_b, grp.t_len, 2, N_HEADS)
    gates = gates.transpose(1, 2, 0, 3).reshape(grp.t_len, 2, grp.n_b * N_HEADS)
    gates = jnp.concatenate([gates] * g, axis=2)
    vops = jnp.concatenate([grp.val_operand(p_gla[:, 256:512], HEAD), grp.val_operand(rw_ops[:, 1280:1536], HEAD),
                            grp.val_operand(p_mlstm[:, 512:768], HEAD), gates,
                            jnp.zeros((grp.t_len, SUBLANES - 2, gates.shape[2]), F32)], axis=1)
    st_in = [grp.state_kv_in(gla_s, False), grp.state_kv_in(rwkv_s, True), grp.state_kv_in(mc_s, False),
             grp.state_k_in(mn_s), grp.state_m_in(mm_s)]
    o_scan, sg, sr, sc, sn, sm = _scan(gk, rk, mk, vops, st_in, grp.t_len, g)
    scan_rows = grp.scan_to_rows(o_scan)
    o_gla, o_rwkv, o_mlstm = _post(scan_rows, p_gla, aux, p_mlstm, lw["post"], min(256, grp.tm))
    return o_gla, o_rwkv, o_mlstm, (grp.state_kv_out(sg, False), grp.state_kv_out(sr, True),
                                    grp.state_kv_out(sc, False), grp.state_k_out(sn), grp.state_m_out(sm))


def _cross_and_ffn(grp, x, lw, mem_k, mem_v, layer, conv_state):
    (q,) = _norm_matmul(x, lw["g_pre_ca"], lw["wq"], (D_MODEL,), grp.tm)
    q = grp.to_btc(q)
    ca = _cross_attn(q, mem_k, mem_v, layer, min(512, grp.t_len))
    x = _proj_res(x, grp.from_btc(ca), lw["wo"], lw["g_post_ca"], grp.tm)
    x, conv_new = _ffn(x, lw["g_pre_ffn"], lw["wa"], lw["wg"], lw["cw"], lw["cb"], lw["wd"], conv_state,
                       lw["g_post_ffn"], grp.n_seq, grp.tm, grp.stride)
    return x, conv_new


def kernel(x_prompt, x_sample, mem_prompt, cache_ckv, cache_kpe, page_table, cache_mem_k, cache_mem_v, state_gla, state_rwkv, state_rwkv_shift, state_mlstm_c, state_mlstm_n, state_mlstm_m, state_conv, norm_pre_mix, norm_post_mix, norm_pre_ca, norm_post_ca, norm_pre_ffn, norm_post_ffn, w_in, gla_w_gate2, gla_b_gate, gla_norm, mla_norm_q, mla_norm_kv, mla_w_uq, mla_w_uk, mla_w_uv, rwkv_mu, rwkv_w0, rwkv_w2, rwkv_a0, rwkv_a2, rwkv_g2, rwkv_k_k, rwkv_k_a, rwkv_r_k, rwkv_ln_w, rwkv_ln_b, mlstm_i_bias, mlstm_f_bias, mlstm_norm, w_branch, w_out, ca_norm_mem, ca_w_q, ca_w_k, ca_w_v, ca_w_o, ffn_w_up, ffn_conv_w, ffn_conv_b, ffn_w_down):
    p = dict(norm_pre_mix=norm_pre_mix, norm_post_mix=norm_post_mix, norm_pre_ca=norm_pre_ca, norm_post_ca=norm_post_ca,
             norm_pre_ffn=norm_pre_ffn, norm_post_ffn=norm_post_ffn, w_in=w_in, gla_w_gate2=gla_w_gate2,
             gla_b_gate=gla_b_gate, gla_norm=gla_norm, mla_norm_q=mla_norm_q, mla_norm_kv=mla_norm_kv,
             mla_w_uq=mla_w_uq, mla_w_uk=mla_w_uk, mla_w_uv=mla_w_uv, rwkv_mu=rwkv_mu, rwkv_w0=rwkv_w0,
             rwkv_w2=rwkv_w2, rwkv_a0=rwkv_a0, rwkv_a2=rwkv_a2, rwkv_g2=rwkv_g2, rwkv_k_k=rwkv_k_k,
             rwkv_k_a=rwkv_k_a, rwkv_r_k=rwkv_r_k, rwkv_ln_w=rwkv_ln_w, rwkv_ln_b=rwkv_ln_b,
             mlstm_i_bias=mlstm_i_bias, mlstm_f_bias=mlstm_f_bias, mlstm_norm=mlstm_norm, w_branch=w_branch,
             w_out=w_out, ca_norm_mem=ca_norm_mem, ca_w_q=ca_w_q, ca_w_k=ca_w_k, ca_w_v=ca_w_v, ca_w_o=ca_w_o,
             ffn_w_up=ffn_w_up, ffn_conv_w=ffn_conv_w, ffn_conv_b=ffn_conv_b, ffn_w_down=ffn_w_down)
    depth = w_in.shape[0]
    b_p, t_p, _ = x_prompt.shape
    b_s, t_s, _ = x_sample.shape
    n_pages, page = page_table.shape[1], cache_ckv.shape[2]
    past_len = n_pages * page
    mem_len = mem_prompt.shape[1]
    gp = _Group(b_p, t_p, time_major=False)
    gs = _Group(b_s, t_s, time_major=True)

    xp = gp.from_btc(x_prompt)
    xs = gs.from_btc(x_sample)
    mem_rows = mem_prompt.reshape(b_p * mem_len, D_MODEL)
    mem_k_c = cache_mem_k.reshape(depth, b_s, mem_len, D_MODEL)
    mem_v_c = cache_mem_v.reshape(depth, b_s, mem_len, D_MODEL)
    cache_kpe_t = jnp.swapaxes(cache_kpe, 2, 3)

    conv0_p = jnp.zeros((b_p, 2, D_FF), F32)

    new_p, new_s, mem_k_new, mem_v_new = [], [], [], []
    for l in range(depth):
        lw = _layer_weights(p, l)

        mk, mv = _norm_matmul(mem_rows, lw["g_mem"], lw["wkv"], (D_MODEL, D_MODEL), min(512, mem_rows.shape[0]))
        attend_p = lambda q_lat, q_pe, ckv, kpe: _mla_prompt(q_lat, q_pe, ckv, kpe, b_p, t_p)
        xp, st = _mixer(gp, xp, lw, None, 0.0, attend_p)
        xp, conv_new = _cross_and_ffn(gp, xp, lw, mk.reshape(1, b_p, mem_len, D_MODEL),
                                      mv.reshape(1, b_p, mem_len, D_MODEL), 0, conv0_p)
        new_p.append(st + (conv_new,))
        mem_k_new.append(mk.reshape(b_p, mem_len, N_HEADS, CA_DIM))
        mem_v_new.append(mv.reshape(b_p, mem_len, N_HEADS, CA_DIM))

        def attend_s(q_lat, q_pe, ckv, kpe, l=l):
            rows = t_s * N_HEADS
            ql = gs.to_btc(q_lat).reshape(b_s, rows, MLA_LORA)
            qp = gs.to_btc(q_pe).reshape(b_s, rows, MLA_ROPE)
            cn = jnp.pad(gs.to_btc(ckv), ((0, 0), (0, page - t_s), (0, 0)))
            pn = jnp.pad(gs.to_btc(kpe), ((0, 0), (0, page - t_s), (0, 0)))
            o = _mla_sample(page_table, ql, qp, cn, pn, cache_ckv, cache_kpe_t, l)
            return gs.from_btc(o.reshape(b_s, t_s, N_HEADS * MLA_LORA))

        st_s = (state_gla[l], state_rwkv[l], state_rwkv_shift[l], state_mlstm_c[l], state_mlstm_n[l], state_mlstm_m[l])
        xs, st = _mixer(gs, xs, lw, st_s, float(past_len), attend_s)
        conv_s = state_conv[l].transpose(1, 0, 2).reshape(1, 2 * b_s, D_FF)
        xs, conv_new = _cross_and_ffn(gs, xs, lw, mem_k_c, mem_v_c, l, conv_s)
        new_s.append(st + (conv_new.reshape(2, b_s, D_FF).transpose(1, 0, 2),))

    def stack(per_layer, i, grp):
        vals = [st[i] for st in per_layer]
        if i in (0, 1):
            vals = [grp.to_btc(v) for v in vals]
        return jnp.stack(vals, axis=0)

    outs = [gp.to_btc(xp), gs.to_btc(xs), stack(new_p, 0, gp), stack(new_p, 1, gp), stack(new_s, 0, gs), stack(new_s, 1, gs),
            jnp.stack(mem_k_new, axis=0), jnp.stack(mem_v_new, axis=0)]
    for i in range(2, 9):
        outs.append(stack(new_p, i, gp))
        outs.append(stack(new_s, i, gs))
    return tuple(outs)
```

```python
import functools
import math

import jax
import jax.numpy as jnp
from jax import lax
from jax.experimental import pallas as pl
from jax.experimental.pallas import tpu as pltpu

F32 = jnp.float32
BF16 = jnp.bfloat16

D_MODEL = 1024
BRANCH = 256
N_HEADS = 4
EPS = 1e-6
GLA_K = 32
GLA_TAU = 16.0
MLA_NOPE = 64
MLA_ROPE = 32
MLA_LORA = 128
ROPE_THETA = 10000.0
RWKV_LN_EPS = 64e-5
HEAD = 64
CA_DIM = 256
D_FF = 2816
FF_CHUNKS = 2

LANES = 128
SUBLANES = 8
VMEM_BUDGET = 56 * 1024 * 1024

W_GLA, W_MLA, W_RWKV, W_MLSTM, W_GATE = 896, 512, 1024, 1152, 4096
PIECES = (W_GLA, W_MLA, W_RWKV, W_MLSTM, W_GATE)


def _params(sem, vmem_bytes):
    return pltpu.CompilerParams(dimension_semantics=sem, vmem_limit_bytes=int(min(max(vmem_bytes, 16 << 20), VMEM_BUDGET)))


def _nbytes(shape, dtype=F32):
    return math.prod(shape) * jnp.dtype(dtype).itemsize


def _rms(x, g):
    return x * lax.rsqrt(jnp.mean(x * x, axis=-1, keepdims=True) + EPS) * g


def _bdot(a, b):
    return jnp.dot(a.astype(BF16), b.astype(BF16), preferred_element_type=F32)


def _bdot_t(a, b):
    return lax.dot_general(a.astype(BF16), b.astype(BF16), (((1,), (1,)), ((), ())), preferred_element_type=F32)


def _group_sum(x, ones_bd):
    hi = x.astype(BF16)
    lo = (x - hi.astype(F32)).astype(BF16)
    return jnp.dot(hi, ones_bd, preferred_element_type=F32) + jnp.dot(lo, ones_bd, preferred_element_type=F32)


def _softplus(x):
    return jnp.maximum(x, 0.0) + jnp.log1p(jnp.exp(-jnp.abs(x)))


def _log_sigmoid(x):
    return -_softplus(-x)


def _shift_rows(x, first, s):
    n = x.shape[0]
    if s % SUBLANES == 0:
        return jnp.concatenate([first, x[: n - s]], axis=0)
    rolled = pltpu.roll(x, s, 0)
    row = lax.broadcasted_iota(jnp.int32, x.shape, 0)
    for r in range(s):
        rolled = jnp.where(row == r, first[r : r + 1, :], rolled)
    return rolled


def _mix_in_kernel(x_ref, g_ref, w_ref, *outs):
    h = _rms(x_ref[...], g_ref[...]).astype(BF16)
    off = 0
    for o_ref, width in zip(outs, PIECES):
        piece = jnp.dot(h, w_ref[:, off : off + width], preferred_element_type=F32)
        if o_ref.dtype == BF16:
            piece = jax.nn.sigmoid(piece)
        o_ref[...] = piece.astype(o_ref.dtype)
        off += width


def _mix_in(x, g, w_all, tm):
    n = x.shape[0]
    total = sum(PIECES)
    vmem = 2 * _nbytes((tm, D_MODEL)) + 2 * _nbytes((D_MODEL, total), BF16) + 2 * _nbytes((tm, total)) + (4 << 20)
    return pl.pallas_call(
        _mix_in_kernel,
        grid=(n // tm,),
        in_specs=[
            pl.BlockSpec((tm, D_MODEL), lambda i: (i, 0)),
            pl.BlockSpec((1, D_MODEL), lambda i: (0, 0)),
            pl.BlockSpec((D_MODEL, total), lambda i: (0, 0)),
        ],
        out_specs=[pl.BlockSpec((tm, w), lambda i: (i, 0)) for w in PIECES],
        out_shape=[jax.ShapeDtypeStruct((n, w), BF16 if w == W_GATE else F32) for w in PIECES],
        compiler_params=_params(("parallel",), vmem),
    )(x, g, w_all)


def _prep_kernel(pg_ref, pr_ref, pm_ref, shift_ref,
                 gate2_ref, bgate_ref, mu_ref, w0_ref, w2_ref, a0_ref, a2_ref, g2_ref, kk_ref, ka_ref, rk_ref,
                 ones_ref, ibias_ref, fbias_ref,
                 gla_ref, rw_ref, aux_ref, ml_ref, carry, *, stride):
    i = pl.program_id(1)
    tm = pr_ref.shape[0]

    @pl.when(i == 0)
    def _():
        carry[...] = shift_ref[0]

    pr = pr_ref[...]
    prev = _shift_rows(pr, carry[...], stride)
    carry[...] = pr[tm - stride :, :]
    log_w, kk, a, k_mod, rr, rv, g, bonus = _rwkv_mix(
        pr, prev, mu_ref[...], w0_ref[...], w2_ref[...], a0_ref[...], a2_ref[...], g2_ref[...], kk_ref[...],
        ka_ref[...], rk_ref[...], ones_ref[...])
    rw_ref[:, 0:256] = jnp.exp(log_w)
    rw_ref[:, 256:512] = kk
    rw_ref[:, 512:768] = kk * a
    rw_ref[:, 768:1024] = k_mod
    rw_ref[:, 1024:1280] = rr
    rw_ref[:, 1280:1536] = rv
    aux_ref[:, 0:256] = g
    aux_ref[:, 256:512] = bonus

    pg = pg_ref[...]
    z = _bdot(pg[:, 768:896], gate2_ref[...]) + bgate_ref[...]
    gla_ref[:, 0:128] = pg[:, 0:128] * (GLA_K ** -0.5)
    gla_ref[:, 128:256] = jnp.exp(_log_sigmoid(z) / GLA_TAU)

    pml = pm_ref[...]
    ifp = pml[:, 1024:1152]
    lane = lax.broadcasted_iota(jnp.int32, ifp.shape, 1)
    ml_ref[:, 0:256] = pml[:, 0:256] * (HEAD ** -0.5)
    ml_ref[:, 256:384] = jnp.where(lane < N_HEADS, ifp + ibias_ref[...], _log_sigmoid(ifp + fbias_ref[...]))


def _prep(p_gla, p_rwkv, p_mlstm, shift0, wts, n_seq, tm, stride):
    n = p_rwkv.shape[0]
    n_tiles = n // n_seq // tm
    row = lambda w: pl.BlockSpec((tm, w), lambda q, i: (q * n_tiles + i, 0))
    const = lambda a: pl.BlockSpec(a.shape, lambda q, i: (0,) * a.ndim)
    out_w = (256, 1536, 512, 384)
    vmem = 2 * _nbytes((tm, W_GLA + W_RWKV + W_MLSTM + sum(out_w))) + 8 * _nbytes((tm, W_RWKV)) + (6 << 20)
    return pl.pallas_call(
        functools.partial(_prep_kernel, stride=stride),
        grid=(n_seq, n_tiles),
        in_specs=[row(W_GLA), row(W_RWKV), row(W_MLSTM),
                  pl.BlockSpec((1, stride, W_RWKV), lambda q, i: (q, 0, 0))] + [const(a) for a in wts],
        out_specs=[row(w) for w in out_w],
        out_shape=[jax.ShapeDtypeStruct((n, w), F32) for w in out_w],
        scratch_shapes=[pltpu.VMEM((stride, W_RWKV), F32)],
        compiler_params=_params(("arbitrary", "arbitrary"), vmem),
    )(p_gla, p_rwkv, p_mlstm, shift0, *wts)


def _rope32(x, cos2, sin2):
    half = MLA_ROPE // 2
    swapped = jnp.concatenate([x[:, half:], x[:, :half]], axis=1)
    return x * cos2 + swapped * sin2


def _mla_prep_kernel(p_ref, cos_ref, sin_ref, nq_ref, nkv_ref, wn_ref, wp_ref, wuk_ref,
                     qlat_ref, qpe_ref, ckv_ref, kpe_ref):
    p = p_ref[...]
    cos2, sin2 = cos_ref[...], sin_ref[...]
    cq = _rms(p[:, 0:256], nq_ref[...])
    q_nope = _bdot(cq, wn_ref[...])
    q_pe = _bdot(cq, wp_ref[...])
    qlat_ref[...] = _bdot(q_nope, wuk_ref[...])
    qpe_ref[...] = jnp.concatenate(
        [_rope32(q_pe[:, h * MLA_ROPE : (h + 1) * MLA_ROPE], cos2, sin2) for h in range(N_HEADS)], axis=1)
    ckv_ref[...] = _rms(p[:, 256:384], nkv_ref[...])
    kpe_ref[...] = _rope32(p[:, 384:416], cos2, sin2)


def _mla_prep(p_mla, cos2, sin2, wts, n_seq, tm):
    n = p_mla.shape[0]
    n_tiles = n // n_seq // tm
    row = lambda w: pl.BlockSpec((tm, w), lambda q, i: (q * n_tiles + i, 0))
    tab = pl.BlockSpec((tm, MLA_ROPE), lambda q, i: (i, 0))
    const = lambda a: pl.BlockSpec(a.shape, lambda q, i: (0,) * a.ndim)
    out_w = (N_HEADS * MLA_LORA, N_HEADS * MLA_ROPE, MLA_LORA, MLA_ROPE)
    vmem = 2 * _nbytes((tm, W_MLA + sum(out_w) + 2 * LANES)) + 6 * _nbytes((tm, 512)) + (4 << 20)
    return pl.pallas_call(
        _mla_prep_kernel,
        grid=(n_seq, n_tiles),
        in_specs=[row(W_MLA), tab, tab] + [const(a) for a in wts],
        out_specs=[row(w) for w in out_w],
        out_shape=[jax.ShapeDtypeStruct((n, w), F32) for w in out_w],
        compiler_params=_params(("parallel", "parallel"), vmem),
    )(p_mla, cos2, sin2, *wts)


MLA_SCALE = (MLA_NOPE + MLA_ROPE) ** -0.5


def _stack_heads(x, width):
    return jnp.concatenate([x[:, h * width : (h + 1) * width] for h in range(N_HEADS)], axis=0)


def _mla_prompt_kernel(ql_ref, qp_ref, ckv_ref, kpe_ref, o_ref, *, tq, tk):
    qi = pl.program_id(1)
    q = (_stack_heads(ql_ref[...], MLA_LORA) * MLA_SCALE).astype(BF16)
    qp = (_stack_heads(qp_ref[...], MLA_ROPE) * MLA_SCALE).astype(BF16)
    rows = N_HEADS * tq

    def block(kb, carry, causal):
        m, l, acc = carry
        k0 = pl.multiple_of(kb * tk, tk)
        kc = ckv_ref[pl.ds(k0, tk), :].astype(BF16)
        kp = kpe_ref[pl.ds(k0, tk), :].astype(BF16)
        s = _bdot_t(q, kc) + _bdot_t(qp, kp)
        if causal:
            tpos = lax.broadcasted_iota(jnp.int32, (tq, tk), 0)
            qpos = jnp.concatenate([tpos] * N_HEADS, axis=0)
            s = jnp.where(lax.broadcasted_iota(jnp.int32, (rows, tk), 1) <= qpos, s, -jnp.inf)
        m_new = jnp.maximum(m, jnp.max(s, axis=1, keepdims=True))
        p = jnp.exp(s - m_new)
        alpha = jnp.exp(m - m_new)
        l = alpha * l + jnp.sum(p, axis=1, keepdims=True)
        acc = alpha * acc + jnp.dot(p.astype(BF16), kc, preferred_element_type=F32)
        return m_new, l, acc

    init = (jnp.full((rows, 1), -jnp.inf, F32), jnp.zeros((rows, 1), F32), jnp.zeros((rows, MLA_LORA), F32))
    carry = lax.fori_loop(0, qi, functools.partial(block, causal=False), init)
    _, l, acc = block(qi, carry, causal=True)
    out = acc / l
    o_ref[...] = jnp.concatenate([out[h * tq : (h + 1) * tq, :] for h in range(N_HEADS)], axis=1)


def _mla_prompt(q_lat, q_pe, ckv, kpe, n_b, t_len):
    tq = tk = min(256, t_len)
    nq = t_len // tq
    vmem = 2 * _nbytes((tq, 1024 + 128)) + 2 * _nbytes((t_len, 256)) + 12 * _nbytes((N_HEADS * tq, tk)) + (4 << 20)
    return pl.pallas_call(
        functools.partial(_mla_prompt_kernel, tq=tq, tk=tk),
        grid=(n_b, nq),
        in_specs=[
            pl.BlockSpec((tq, N_HEADS * MLA_LORA), lambda b, i: (b * nq + i, 0)),
            pl.BlockSpec((tq, N_HEADS * MLA_ROPE), lambda b, i: (b * nq + i, 0)),
            pl.BlockSpec((t_len, MLA_LORA), lambda b, i: (b, 0)),
            pl.BlockSpec((t_len, MLA_ROPE), lambda b, i: (b, 0)),
        ],
        out_specs=pl.BlockSpec((tq, N_HEADS * MLA_LORA), lambda b, i: (b * nq + i, 0)),
        out_shape=jax.ShapeDtypeStruct((n_b * t_len, N_HEADS * MLA_LORA), F32),
        compiler_params=_params(("parallel", "parallel"), vmem),
    )(q_lat, q_pe, ckv, kpe)


def _mla_sample_kernel(pt_ref, ql_ref, qp_ref, cnew_ref, pnew_ref, *rest, n_pg, page):
    ckv_refs = rest[:n_pg]
    kpe_refs = rest[n_pg : 2 * n_pg]
    o_ref = rest[2 * n_pg]
    m_s, l_s, acc_s = rest[2 * n_pg + 1 :]
    g = pl.program_id(1)
    rows = ql_ref.shape[1]

    @pl.when(g == 0)
    def _():
        m_s[...] = jnp.full(m_s.shape, -jnp.inf, F32)
        l_s[...] = jnp.zeros(l_s.shape, F32)
        acc_s[...] = jnp.zeros(acc_s.shape, F32)

    q = (ql_ref[0] * MLA_SCALE).astype(BF16)
    qp = (qp_ref[0] * MLA_SCALE).astype(BF16)

    def update(s, v):
        m = m_s[...]
        m_new = jnp.maximum(m, jnp.max(s, axis=1, keepdims=True))
        alpha = jnp.exp(m - m_new)
        p = jnp.exp(s - m_new)
        m_s[...] = m_new
        l_s[...] = alpha * l_s[...] + jnp.sum(p, axis=1, keepdims=True)
        acc_s[...] = alpha * acc_s[...] + jnp.dot(p.astype(BF16), v, preferred_element_type=F32)

    kc = jnp.concatenate([c_ref[0, 0].astype(BF16) for c_ref in ckv_refs], axis=0)
    kp_t = jnp.concatenate([p_ref[0, 0].astype(BF16) for p_ref in kpe_refs], axis=1)
    update(_bdot_t(q, kc) + jnp.dot(qp, kp_t, preferred_element_type=F32), kc)

    @pl.when(g == pl.num_programs(1) - 1)
    def _():
        kn = cnew_ref[0].astype(BF16)
        kpn = pnew_ref[0].astype(BF16)
        s = _bdot_t(q, kn) + _bdot_t(qp, kpn)
        row = lax.broadcasted_iota(jnp.int32, (rows, page), 0)
        tk = lax.broadcasted_iota(jnp.int32, (rows, page), 1)
        s = jnp.where(tk * N_HEADS <= row, s, -jnp.inf)
        update(s, kn)
        o_ref[0] = acc_s[...] / l_s[...]


def _mla_sample(page_table, q_lat, q_pe, ckv_new, kpe_new, cache_ckv, cache_kpe_t, layer):
    n_b, n_pages = page_table.shape
    page = cache_ckv.shape[2]
    rows = q_lat.shape[1]
    n_pg = math.gcd(n_pages, 32)
    n_groups = n_pages // n_pg
    pt_flat = page_table.reshape(-1)

    def page_spec(shape, p):
        return pl.BlockSpec((1, 1) + shape, lambda b, g, pt: (layer, pt[b * n_pages + g * n_pg + p], 0, 0))

    per_b = lambda shape: pl.BlockSpec((1,) + shape, lambda b, g, pt: (b, 0, 0))
    grid_spec = pltpu.PrefetchScalarGridSpec(
        num_scalar_prefetch=1,
        grid=(n_b, n_groups),
        in_specs=[per_b((rows, MLA_LORA)), per_b((rows, MLA_ROPE)), per_b((page, MLA_LORA)), per_b((page, MLA_ROPE))]
        + [page_spec((page, MLA_LORA), p) for p in range(n_pg)] + [page_spec((MLA_ROPE, page), p) for p in range(n_pg)],
        out_specs=per_b((rows, MLA_LORA)),
        scratch_shapes=[pltpu.VMEM((rows, 1), F32), pltpu.VMEM((rows, 1), F32), pltpu.VMEM((rows, MLA_LORA), F32)],
    )
    vmem = 2 * n_pg * 2 * _nbytes((page, LANES)) + 4 * _nbytes((page, 2 * LANES)) + (8 << 20)
    return pl.pallas_call(
        functools.partial(_mla_sample_kernel, n_pg=n_pg, page=page),
        grid_spec=grid_spec,
        out_shape=jax.ShapeDtypeStruct((n_b, rows, MLA_LORA), F32),
        compiler_params=_params(("parallel", "arbitrary"), vmem),
    )(pt_flat, q_lat, q_pe, ckv_new, kpe_new, *([cache_ckv] * n_pg), *([cache_kpe_t] * n_pg))


V_TILES = HEAD // SUBLANES


def _scan_kernel(gk_ref, rk_ref, mk_ref, v_ref, sg0, sr0, sc0, sn0, sm0,
                 o_ref, sg_o, sr_o, sc_o, sn_o, sm_o, sg, sr, sc, sn, sm, *, tb, ksg, ks, groups):
    i = pl.program_id(1)

    @pl.when(i == 0)
    def _():
        sg[...] = sg0[...]
        sr[...] = sr0[...]
        sc[...] = sc0[...]
        sn[...] = sn0[...]
        sm[...] = sm0[...]

    def fold(x):
        return x + pltpu.roll(x, LANES // 2, 1) if groups == 2 else x

    def vt_slice(vt):
        return slice(vt * SUBLANES, (vt + 1) * SUBLANES)

    def rwkv_sa(t):
        acc = [jnp.zeros((SUBLANES, LANES), F32)] * V_TILES
        for k in range(ks):
            kk = rk_ref[t, ks + k : ks + k + 1, :]
            for vt in range(V_TILES):
                acc[vt] = acc[vt] + sr[k, vt_slice(vt), :] * kk
        return tuple(fold(a) for a in acc)

    def step(t, sa):
        tn = jnp.minimum(t + 1, tb - 1)
        zeros = [jnp.zeros((SUBLANES, LANES), F32)] * V_TILES

        gv = [v_ref[t, vt * SUBLANES : (vt + 1) * SUBLANES, :] for vt in range(V_TILES)]
        o = list(zeros)
        for k in range(ksg):
            q = gk_ref[t, k : k + 1, :]
            kr = gk_ref[t, ksg + k : ksg + k + 1, :]
            a = gk_ref[t, 2 * ksg + k : 2 * ksg + k + 1, :]
            for vt in range(V_TILES):
                new = sg[k, vt_slice(vt), :] * a + gv[vt] * kr
                sg[k, vt_slice(vt), :] = new
                o[vt] = o[vt] + new * q
        for vt in range(V_TILES):
            o_ref[t, vt_slice(vt), :] = fold(o[vt])

        rv = [v_ref[t, HEAD + vt * SUBLANES : HEAD + (vt + 1) * SUBLANES, :] for vt in range(V_TILES)]
        y = list(zeros)
        san = list(zeros)
        for k in range(ks):
            w = rk_ref[t, k : k + 1, :]
            b = rk_ref[t, 2 * ks + k : 2 * ks + k + 1, :]
            kr = rk_ref[t, 3 * ks + k : 3 * ks + k + 1, :]
            r = rk_ref[t, 4 * ks + k : 4 * ks + k + 1, :]
            kkn = rk_ref[tn, ks + k : ks + k + 1, :]
            for vt in range(V_TILES):
                new = sr[k, vt_slice(vt), :] * w - sa[vt] * b + rv[vt] * kr
                sr[k, vt_slice(vt), :] = new
                y[vt] = y[vt] + new * r
                san[vt] = san[vt] + new * kkn
        for vt in range(V_TILES):
            o_ref[t, HEAD + vt * SUBLANES : HEAD + (vt + 1) * SUBLANES, :] = fold(y[vt])

        mv = [v_ref[t, 2 * HEAD + vt * SUBLANES : 2 * HEAD + (vt + 1) * SUBLANES, :] for vt in range(V_TILES)]
        i_t = v_ref[t, 3 * HEAD : 3 * HEAD + 1, :]
        lf = v_ref[t, 3 * HEAD + 1 : 3 * HEAD + 2, :]
        m_old = sm[0:1, :]
        m_new = jnp.maximum(lf + m_old, i_t)
        fs = jnp.exp(lf + m_old - m_new)
        isc = jnp.exp(i_t - m_new)
        sm[0:1, :] = m_new
        q_tile = mk_ref[t, 0:ks, :]
        n_new = fs * sn[...] + isc * mk_ref[t, ks : 2 * ks, :]
        sn[...] = n_new
        den = fold(jnp.sum(n_new * q_tile, axis=0, keepdims=True))
        fs_b = jnp.broadcast_to(fs, (SUBLANES, LANES))
        num = list(zeros)
        for k in range(ks):
            q = mk_ref[t, k : k + 1, :]
            ik = isc * mk_ref[t, ks + k : ks + k + 1, :]
            for vt in range(V_TILES):
                new = sc[k, vt_slice(vt), :] * fs_b + mv[vt] * ik
                sc[k, vt_slice(vt), :] = new
                num[vt] = num[vt] + new * q
        scale = 1.0 / jnp.maximum(jnp.abs(den), jnp.exp(-m_new))
        for vt in range(V_TILES):
            o_ref[t, 2 * HEAD + vt * SUBLANES : 2 * HEAD + (vt + 1) * SUBLANES, :] = fold(num[vt]) * scale
        return tuple(fold(x) for x in san)

    lax.fori_loop(0, tb, step, rwkv_sa(0))

    @pl.when(i == pl.num_programs(1) - 1)
    def _():
        sg_o[...] = sg[...]
        sr_o[...] = sr[...]
        sc_o[...] = sc[...]
        sn_o[...] = sn[...]
        sm_o[...] = sm[...]


def _scan(gk, rk, mk, vops, states, t_len, groups):
    ks = HEAD // groups
    ksg = GLA_K // groups
    n_lanes = gk.shape[2]
    n_lt = n_lanes // LANES
    tb = min(32, t_len)
    op = lambda rows: pl.BlockSpec((tb, rows, LANES), lambda j, i: (i, 0, j))
    st3 = lambda k: pl.BlockSpec((k, HEAD, LANES), lambda j, i: (0, 0, j))
    st2 = lambda k: pl.BlockSpec((k, LANES), lambda j, i: (0, j))
    v_rows = vops.shape[1]
    st_specs = [st3(ksg), st3(ks), st3(ks), st2(ks), st2(SUBLANES)]
    st_shapes = [(ksg, HEAD, n_lanes), (ks, HEAD, n_lanes), (ks, HEAD, n_lanes), (ks, n_lanes), (SUBLANES, n_lanes)]
    blk = _nbytes((tb, 3 * ksg + 5 * ks + 2 * ks + v_rows + 3 * HEAD, LANES))
    st_bytes = _nbytes(((ksg + 2 * ks) * HEAD + ks + SUBLANES, LANES))
    return pl.pallas_call(
        functools.partial(_scan_kernel, tb=tb, ksg=ksg, ks=ks, groups=groups),
        grid=(n_lt, t_len // tb),
        in_specs=[op(3 * ksg), op(5 * ks), op(2 * ks), op(v_rows)] + st_specs,
        out_specs=[op(3 * HEAD)] + st_specs,
        out_shape=[jax.ShapeDtypeStruct((t_len, 3 * HEAD, n_lanes), F32)]
        + [jax.ShapeDtypeStruct(s, F32) for s in st_shapes],
        scratch_shapes=[pltpu.VMEM((ksg, HEAD, LANES), F32), pltpu.VMEM((ks, HEAD, LANES), F32),
                        pltpu.VMEM((ks, HEAD, LANES), F32), pltpu.VMEM((ks, LANES), F32),
                        pltpu.VMEM((SUBLANES, LANES), F32)],
        compiler_params=_params(("parallel", "arbitrary"), 2 * blk + 5 * st_bytes + (4 << 20)),
    )(gk, rk, mk, vops, *states)


def _head_ln(x, ones_bd, eps):
    mean = _group_sum(x, ones_bd) * (1.0 / HEAD)
    xc = x - mean
    var = _group_sum(xc * xc, ones_bd) * (1.0 / HEAD)
    return xc * lax.rsqrt(var + eps)


def _gla_out(o, gate_r, ones_bd, gnorm):
    ms = _group_sum(o * o, ones_bd) * (1.0 / HEAD)
    return o * lax.rsqrt(ms + EPS) * gnorm * jax.nn.silu(gate_r)


def _rwkv_out(y, bonus, g, ones_bd, lnw, lnb):
    return (_head_ln(y, ones_bd, RWKV_LN_EPS) * lnw + lnb + bonus) * g


def _mlstm_out(h, gate_o, ones_bd, mnorm):
    return _head_ln(h, ones_bd, EPS) * mnorm * jax.nn.sigmoid(gate_o)


def _post_kernel(scan_ref, pg_ref, aux_ref, pm_ref, ones_ref, gnorm_ref, lnw_ref, lnb_ref, mnorm_ref,
                 og_ref, or_ref, om_ref):
    ones_bd = ones_ref[...]
    sc = scan_ref[...]
    aux = aux_ref[...]
    og_ref[...] = _gla_out(sc[:, 0:256], pg_ref[:, 512:768], ones_bd, gnorm_ref[...])
    or_ref[...] = _rwkv_out(sc[:, 256:512], aux[:, 256:512], aux[:, 0:256], ones_bd, lnw_ref[...], lnb_ref[...])
    om_ref[...] = _mlstm_out(sc[:, 512:768], pm_ref[:, 768:1024], ones_bd, mnorm_ref[...])


def _post(scan_rows, p_gla, aux, p_mlstm, wts, tm):
    n = scan_rows.shape[0]
    row = lambda w: pl.BlockSpec((tm, w), lambda i: (i, 0))
    const = lambda a: pl.BlockSpec(a.shape, lambda i: (0,) * a.ndim)
    widths = (768, W_GLA, 512, W_MLSTM)
    vmem = 2 * _nbytes((tm, sum(widths) + 768)) + 12 * _nbytes((tm, BRANCH)) + (4 << 20)
    return pl.pallas_call(
        _post_kernel,
        grid=(n // tm,),
        in_specs=[row(w) for w in widths] + [const(a) for a in wts],
        out_specs=[row(BRANCH)] * 3,
        out_shape=[jax.ShapeDtypeStruct((n, BRANCH), F32)] * 3,
        compiler_params=_params(("parallel",), vmem),
    )(scan_rows, p_gla, aux, p_mlstm, *wts)


def _merge_kernel(x_ref, og_ref, olat_ref, or_ref, om_ref, gate_ref, wuv_ref, wbr_ref, wout_ref, gpost_ref, o_ref):
    o_mla = _bdot(olat_ref[...], wuv_ref[...])
    mixed = jnp.zeros((x_ref.shape[0], D_MODEL), F32)
    for n, br in enumerate((og_ref[...], o_mla, or_ref[...], om_ref[...])):
        up = _bdot(br, wbr_ref[n])
        mixed = mixed + gate_ref[:, n * D_MODEL : (n + 1) * D_MODEL].astype(F32) * up
    out = _bdot(mixed, wout_ref[...])
    o_ref[...] = x_ref[...] + _rms(out, gpost_ref[...])


def _merge(x, o_gla, o_lat, o_rwkv, o_mlstm, p_gate, wts, tm):
    n = x.shape[0]
    row = lambda w: pl.BlockSpec((tm, w), lambda i: (i, 0))
    const = lambda a: pl.BlockSpec(a.shape, lambda i: (0,) * a.ndim)
    widths = (D_MODEL, BRANCH, 512, BRANCH, BRANCH, W_GATE)
    vmem = 2 * _nbytes((tm, sum(widths) + D_MODEL)) + 10 * _nbytes((tm, D_MODEL)) + (12 << 20)
    return pl.pallas_call(
        _merge_kernel,
        grid=(n // tm,),
        in_specs=[row(w) for w in widths] + [const(a) for a in wts],
        out_specs=row(D_MODEL),
        out_shape=jax.ShapeDtypeStruct((n, D_MODEL), F32),
        compiler_params=_params(("parallel",), vmem),
    )(x, o_gla, o_lat, o_rwkv, o_mlstm, p_gate, *wts)


GLA_CHUNK = 32
MIX_CHUNK = 64


def _bdot_tn(a, b):
    return lax.dot_general(a.astype(BF16), b.astype(BF16), (((0,), (0,)), ((), ())), preferred_element_type=F32)


def _cumsum_rows(tri, x):
    hi = x.astype(BF16)
    lo = (x - hi.astype(F32)).astype(BF16)
    return jnp.dot(tri, hi, preferred_element_type=F32) + jnp.dot(tri, lo, preferred_element_type=F32)


def _rep_rows(x, mask):
    return jnp.concatenate([x] * N_HEADS, axis=0) * mask


def _gla_chunk_kernel(pg_ref, gate2_ref, bgate_ref, gnorm_ref, ones_ref, tri_ref, causal_ref, kmask_ref, vmask_ref,
                      smask_ref, o_ref, st_ref, st, *, tc):
    i = pl.program_id(1)

    @pl.when(i == 0)
    def _():
        st[...] = jnp.zeros(st.shape, F32)

    L = GLA_CHUNK
    pg = pg_ref[...]
    log_a = _log_sigmoid(_bdot(pg[:, 768:896], gate2_ref[...]) + bgate_ref[...]) / GLA_TAU
    tri, causal = tri_ref[...], causal_ref[...]
    kmask, vmask, smask = kmask_ref[...], vmask_ref[...], smask_ref[...]
    rows = [slice(c * L, (c + 1) * L) for c in range(tc // L)]
    b = [_cumsum_rows(tri, log_a[r]) for r in rows]
    b_last = [x[L - 1 : L, :] for x in b]
    q_dec = [pg[r, 0:128] * (GLA_K ** -0.5) * jnp.exp(x) for r, x in zip(rows, b)]
    k_rep = [_rep_rows(pg[r, 128:256] * jnp.exp(-x), kmask) for r, x in zip(rows, b)]
    v_rep = [_rep_rows(pg[r, 256:512], vmask) for r in rows]
    att = [_bdot_t(qd, kr) * causal for qd, kr in zip(q_dec, k_rep)]
    o_intra = [_bdot(x, vr) for x, vr in zip(att, v_rep)]
    upd = [_bdot_tn(pg[r, 256:512], pg[r, 128:256] * jnp.exp(bl - x)) * smask for r, x, bl in zip(rows, b, b_last)]
    states = []
    s_t = st[...]
    for bl, ud in zip(b_last, upd):
        states.append(s_t)
        s_t = s_t * jnp.exp(bl) + ud
    st[...] = s_t
    o = jnp.concatenate([_bdot_t(qd, s) + oi for qd, s, oi in zip(q_dec, states, o_intra)], axis=0)
    o_ref[...] = _gla_out(o, pg[:, 512:768], ones_ref[...], gnorm_ref[...])

    @pl.when(i == pl.num_programs(1) - 1)
    def _():
        st_ref[0] = st[...]


def _gla_chunk(p_gla, wts, n_b, t_len):
    tc = min(256, t_len)
    nt = t_len // tc
    const = lambda a: pl.BlockSpec(a.shape, lambda b, i: (0,) * a.ndim)
    vmem = 2 * _nbytes((tc, W_GLA + BRANCH)) + 12 * _nbytes((tc, BRANCH)) + (6 << 20)
    return pl.pallas_call(
        functools.partial(_gla_chunk_kernel, tc=tc),
        grid=(n_b, nt),
        in_specs=[pl.BlockSpec((tc, W_GLA), lambda b, i: (b * nt + i, 0))] + [const(a) for a in wts],
        out_specs=[pl.BlockSpec((tc, BRANCH), lambda b, i: (b * nt + i, 0)),
                   pl.BlockSpec((1, BRANCH, N_HEADS * GLA_K), lambda b, i: (b, 0, 0))],
        out_shape=[jax.ShapeDtypeStruct((n_b * t_len, BRANCH), F32),
                   jax.ShapeDtypeStruct((n_b, BRANCH, N_HEADS * GLA_K), F32)],
        scratch_shapes=[pltpu.VMEM((BRANCH, N_HEADS * GLA_K), F32)],
        compiler_params=_params(("parallel", "arbitrary"), vmem),
    )(p_gla, *wts)


def _mlstm_chunk_kernel(pm_ref, ibias_ref, fbias_ref, mnorm_ref, ones_ref, tri_ref,
                        o_ref, c_out, n_out, m_out, cs, ns, ms, *, tc):
    i = pl.program_id(1)

    @pl.when(i == 0)
    def _():
        cs[...] = jnp.zeros(cs.shape, F32)
        ns[...] = jnp.zeros(ns.shape, F32)
        ms[...] = jnp.zeros(ms.shape, F32)

    L = MIX_CHUNK
    pm = pm_ref[...]
    ifp = pm[:, 1024:1152]
    lane = lax.broadcasted_iota(jnp.int32, ifp.shape, 1)
    gates = jnp.where(lane < N_HEADS, ifp + ibias_ref[...], _log_sigmoid(ifp + fbias_ref[...]))
    tri = tri_ref[...]
    causal = lax.broadcasted_iota(jnp.int32, (L, L), 1) <= lax.broadcasted_iota(jnp.int32, (L, L), 0)
    lane_c = lax.broadcasted_iota(jnp.int32, (L, LANES), 1)
    row_t = lax.broadcasted_iota(jnp.int32, (LANES, L), 0)
    last_lane = lax.broadcasted_iota(jnp.int32, (1, L), 1) == L - 1

    def pick_col(x, j):
        return jnp.sum(jnp.where(lane_c == j, x, 0.0), axis=1, keepdims=True)

    def pick_row(x_t, j):
        return jnp.sum(jnp.where(row_t == j, x_t, 0.0), axis=0, keepdims=True)

    n_ch = tc // L
    rows = [slice(c * L, (c + 1) * L) for c in range(n_ch)]
    units = [(c, h) for c in range(n_ch) for h in range(N_HEADS)]
    z = [jnp.where(lane_c < N_HEADS, gates[r], _cumsum_rows(tri, gates[r])) for r in rows]
    z_t = [x.T for x in z]
    i_col = [pick_col(z[c], h) for c, h in units]
    b_col = [pick_col(z[c], N_HEADS + h) for c, h in units]
    i_row = [pick_row(z_t[c], h) for c, h in units]
    b_row = [pick_row(z_t[c], N_HEADS + h) for c, h in units]
    b_last = [jnp.sum(jnp.where(last_lane, x, 0.0), axis=1, keepdims=True) for x in b_row]
    g_max = [jnp.max(bl - br + ir, axis=1, keepdims=True) for bl, br, ir in zip(b_last, b_row, i_row)]
    m0, m1 = [None] * len(units), [None] * len(units)
    for h in range(N_HEADS):
        m = jnp.max(ms[h : h + 1, :], axis=1, keepdims=True)
        for c in range(n_ch):
            u = c * N_HEADS + h
            m0[u] = m
            m = jnp.maximum(b_last[u] + m, g_max[u])
            m1[u] = m
        ms[h : h + 1, :] = jnp.broadcast_to(m, (1, LANES))
    q = [pm[rows[c], h * HEAD : (h + 1) * HEAD] * (HEAD ** -0.5) for c, h in units]
    k = [pm[rows[c], 256 + h * HEAD : 256 + (h + 1) * HEAD] for c, h in units]
    v = [pm[rows[c], 512 + h * HEAD : 512 + (h + 1) * HEAD] for c, h in units]
    d = [jnp.where(causal, bc - br + ir, -jnp.inf) for bc, br, ir in zip(b_col, b_row, i_row)]
    inter = [bc + m for bc, m in zip(b_col, m0)]
    m_t = [jnp.maximum(x, jnp.max(dd, axis=1, keepdims=True)) for x, dd in zip(inter, d)]
    w_inter = [jnp.exp(x - mt) for x, mt in zip(inter, m_t)]
    s_qk = [_bdot_t(a_, b_) for a_, b_ in zip(q, k)]
    w_intra = [jnp.exp(dd - mt) * s for dd, mt, s in zip(d, m_t, s_qk)]
    num_intra = [_bdot(w, x) for w, x in zip(w_intra, v)]
    den_intra = [jnp.sum(w, axis=1, keepdims=True) for w in w_intra]
    floor = [jnp.exp(-mt) for mt in m_t]
    scale_old = [jnp.exp(bl + a_ - b_) for bl, a_, b_ in zip(b_last, m0, m1)]
    kw = [x * jnp.exp(bl - bc + ic - m) for x, bl, bc, ic, m in zip(k, b_last, b_col, i_col, m1)]
    c_upd = [_bdot_tn(x, y) for x, y in zip(kw, v)]
    n_upd = [jnp.sum(x, axis=0, keepdims=True) for x in kw]
    chunks = []
    for c in range(n_ch):
        heads = []
        for h in range(N_HEADS):
            u = c * N_HEADS + h
            c_h, n_h = cs[h], ns[h : h + 1, :]
            num = w_inter[u] * _bdot(q[u], c_h) + num_intra[u]
            den = w_inter[u] * jnp.sum(q[u] * n_h, axis=1, keepdims=True) + den_intra[u]
            heads.append(num / jnp.maximum(jnp.abs(den), floor[u]))
            cs[h] = scale_old[u] * c_h + c_upd[u]
            ns[h : h + 1, :] = scale_old[u] * n_h + n_upd[u]
        chunks.append(jnp.concatenate(heads, axis=1))
    hm = jnp.concatenate(chunks, axis=0)
    o_ref[...] = _mlstm_out(hm, pm[:, 768:1024], ones_ref[...], mnorm_ref[...])

    @pl.when(i == pl.num_programs(1) - 1)
    def _():
        c_out[0] = cs[...]
        n_out[0] = ns[...]
        m_out[0] = ms[...]


def _mlstm_chunk(p_mlstm, wts, n_b, t_len):
    tc = min(256, t_len)
    nt = t_len // tc
    const = lambda a: pl.BlockSpec(a.shape, lambda b, i: (0,) * a.ndim)
    vmem = 2 * _nbytes((tc, W_MLSTM + BRANCH)) + 12 * _nbytes((tc, BRANCH)) + (6 << 20)
    return pl.pallas_call(
        functools.partial(_mlstm_chunk_kernel, tc=tc),
        grid=(n_b, nt),
        in_specs=[pl.BlockSpec((tc, W_MLSTM), lambda b, i: (b * nt + i, 0))] + [const(a) for a in wts],
        out_specs=[pl.BlockSpec((tc, BRANCH), lambda b, i: (b * nt + i, 0)),
                   pl.BlockSpec((1, N_HEADS, HEAD, HEAD), lambda b, i: (b, 0, 0, 0)),
                   pl.BlockSpec((1, SUBLANES, HEAD), lambda b, i: (b, 0, 0)),
                   pl.BlockSpec((1, SUBLANES, LANES), lambda b, i: (b, 0, 0))],
        out_shape=[jax.ShapeDtypeStruct((n_b * t_len, BRANCH), F32),
                   jax.ShapeDtypeStruct((n_b, N_HEADS, HEAD, HEAD), F32),
                   jax.ShapeDtypeStruct((n_b, SUBLANES, HEAD), F32),
                   jax.ShapeDtypeStruct((n_b, SUBLANES, LANES), F32)],
        scratch_shapes=[pltpu.VMEM((N_HEADS, HEAD, HEAD), F32), pltpu.VMEM((SUBLANES, HEAD), F32),
                        pltpu.VMEM((SUBLANES, LANES), F32)],
        compiler_params=_params(("parallel", "arbitrary"), vmem),
    )(p_mlstm, *wts)


def _rwkv_mix(pr, prev, mu, w0, w2, a0, a2, g2, kkw, ka, rkw, ones_bd):
    pm = pr + (prev - pr) * mu
    rr, rk, rv = pm[:, 0:256], pm[:, 256:512], pm[:, 512:768]
    wlo, alo, rglo = pm[:, 768:832], pm[:, 832:896], pm[:, 896:1024]
    log_w = -jnp.exp(-_softplus(-(w0 + _bdot(jnp.tanh(wlo), w2))) - 0.5)
    a = jax.nn.sigmoid(a0 + _bdot(alo, a2))
    g = _bdot(jax.nn.sigmoid(rglo), g2)
    kkr = rk * kkw
    kk = kkr / jnp.maximum(jnp.sqrt(_group_sum(kkr * kkr, ones_bd)), 1e-12)
    k_mod = rk * (1.0 + (a - 1.0) * ka)
    bonus = _group_sum(rr * k_mod * rkw, ones_bd) * rv
    return log_w, kk, a, k_mod, rr, rv, g, bonus


def _rwkv_chunk_kernel(pr_ref, mu_ref, w0_ref, w2_ref, a0_ref, a2_ref, g2_ref, kk_ref, ka_ref, rk_ref, ones_ref,
                       lnw_ref, lnb_ref, tri_ref, strict_ref, incl_ref, o_ref, s_out, carry, sv, *, tc):
    i = pl.program_id(1)

    @pl.when(i == 0)
    def _():
        carry[...] = jnp.zeros(carry.shape, F32)
        sv[...] = jnp.zeros(sv.shape, F32)

    L = MIX_CHUNK
    pr = pr_ref[...]
    prev = _shift_rows(pr, carry[...], 1)
    carry[...] = pr[tc - 1 :, :]
    ones_bd = ones_ref[...]
    log_w, kk, a, k_mod, rr, rv, g, bonus = _rwkv_mix(
        pr, prev, mu_ref[...], w0_ref[...], w2_ref[...], a0_ref[...], a2_ref[...], g2_ref[...], kk_ref[...],
        ka_ref[...], rk_ref[...], ones_bd)
    bd = ones_bd.astype(F32)
    tri, strict, incl = tri_ref[...], strict_ref[...], incl_ref[...]
    chunks = range(tc // L)
    rows = [slice(c * L, (c + 1) * L) for c in chunks]
    cum = [_cumsum_rows(tri, log_w[r]) for r in rows]
    w_end = [jnp.exp(cm[L - 1 : L, :]) for cm in cum]
    v_c = [rv[r] for r in rows]
    v_rep = [_rep_rows(v, bd) for v in v_c]
    al, rt, k_end, gm = [], [], [], []
    for c, r in zip(chunks, rows):
        end = jnp.exp(cum[c][L - 1 : L, :] - cum[c])
        w_inv = jnp.exp(-cum[c])
        ka_c = kk[r] * a[r]
        al.append(-kk[r] * jnp.exp(cum[c] - log_w[r]))
        rt.append(rr[r] * jnp.exp(cum[c]))
        k_end.append(jnp.concatenate([ka_c * end, k_mod[r] * end], axis=0))
        gm.append(_bdot_t(jnp.concatenate([al[c], rt[c]], axis=0),
                          jnp.concatenate([_rep_rows(ka_c * w_inv, bd), _rep_rows(k_mod[r] * w_inv, bd)], axis=0)))
    a_ab = [jnp.where(strict > 0, g_[0:L, 0:256], 0.0) for g_ in gm]
    a_ak = [jnp.where(strict > 0, g_[0:L, 256:512], 0.0) for g_ in gm]
    r_b = [jnp.where(incl > 0, g_[L : 2 * L, 0:256], 0.0) for g_ in gm]
    r_k = [jnp.where(incl > 0, g_[L : 2 * L, 256:512], 0.0) for g_ in gm]
    t_cols = [(incl - strict) + x for x in a_ab]
    p_cols = [_bdot(x, _rep_rows(x, bd)) for x in a_ab]
    for _ in range(4):
        prod = [_bdot(jnp.concatenate([t, p], axis=0), _rep_rows(p, bd)) for t, p in zip(t_cols, p_cols)]
        t_cols = [t + pr_[0:L] for t, pr_ in zip(t_cols, prod)]
        p_cols = [pr_[L : 2 * L] for pr_ in prod]
    t_cols = [t + _bdot(t, _rep_rows(p, bd)) for t, p in zip(t_cols, p_cols)]
    al2 = [_bdot(t, _rep_rows(x, bd)) for t, x in zip(t_cols, al)]
    u_fix = [_bdot(t, _rep_rows(_bdot(x, v), bd)) for t, x, v in zip(t_cols, a_ak, v_rep)]
    y_fix = [_bdot(x, v) for x, v in zip(r_k, v_rep)]
    outs = []
    for c in chunks:
        s_vk = sv[...]
        u = _bdot_t(al2[c], s_vk) + u_fix[c]
        outs.append(_bdot_t(rt[c], s_vk) + _bdot(r_b[c], _rep_rows(u, bd)) + y_fix[c])
        sv[...] = s_vk * w_end[c] + _bdot_tn(jnp.concatenate([u, v_c[c]], axis=0), k_end[c]) * bd
    y = jnp.concatenate(outs, axis=0)
    o_ref[...] = _rwkv_out(y, bonus, g, ones_bd, lnw_ref[...], lnb_ref[...])

    @pl.when(i == pl.num_programs(1) - 1)
    def _():
        s_out[0] = sv[...]


def _rwkv_chunk(p_rwkv, wts, n_b, t_len):
    tc = min(256, t_len)
    nt = t_len // tc
    const = lambda a: pl.BlockSpec(a.shape, lambda b, i: (0,) * a.ndim)
    vmem = 2 * _nbytes((tc, W_RWKV + BRANCH)) + 24 * _nbytes((tc, BRANCH)) + 16 * _nbytes((BRANCH, BRANCH)) + (6 << 20)
    return pl.pallas_call(
        functools.partial(_rwkv_chunk_kernel, tc=tc),
        grid=(n_b, nt),
        in_specs=[pl.BlockSpec((tc, W_RWKV), lambda b, i: (b * nt + i, 0))] + [const(a) for a in wts],
        out_specs=[pl.BlockSpec((tc, BRANCH), lambda b, i: (b * nt + i, 0)),
                   pl.BlockSpec((1, BRANCH, BRANCH), lambda b, i: (b, 0, 0))],
        out_shape=[jax.ShapeDtypeStruct((n_b * t_len, BRANCH), F32), jax.ShapeDtypeStruct((n_b, BRANCH, BRANCH), F32)],
        scratch_shapes=[pltpu.VMEM((1, W_RWKV), F32), pltpu.VMEM((BRANCH, BRANCH), F32)],
        compiler_params=_params(("parallel", "arbitrary"), vmem),
    )(p_rwkv, *wts)


def _norm_matmul_kernel(x_ref, g_ref, w_ref, *outs):
    h = _rms(x_ref[...], g_ref[...]).astype(BF16)
    off = 0
    for o_ref in outs:
        width = o_ref.shape[1]
        o_ref[...] = jnp.dot(h, w_ref[:, off : off + width], preferred_element_type=F32)
        off += width


def _norm_matmul(x, g, w, widths, tm):
    n = x.shape[0]
    total = sum(widths)
    vmem = 2 * _nbytes((tm, D_MODEL + total)) + 2 * _nbytes((D_MODEL, total), BF16) + (6 << 20)
    return pl.pallas_call(
        _norm_matmul_kernel,
        grid=(n // tm,),
        in_specs=[pl.BlockSpec((tm, D_MODEL), lambda i: (i, 0)), pl.BlockSpec((1, D_MODEL), lambda i: (0, 0)),
                  pl.BlockSpec((D_MODEL, total), lambda i: (0, 0))],
        out_specs=[pl.BlockSpec((tm, w_), lambda i: (i, 0)) for w_ in widths],
        out_shape=[jax.ShapeDtypeStruct((n, w_), F32) for w_ in widths],
        compiler_params=_params(("parallel",), vmem),
    )(x, g, w)


def _proj_res_kernel(x_ref, a_ref, w_ref, g_ref, o_ref):
    o_ref[...] = x_ref[...] + _rms(_bdot(a_ref[...], w_ref[...]), g_ref[...])


def _proj_res(x, a, w, g, tm):
    n = x.shape[0]
    row = pl.BlockSpec((tm, D_MODEL), lambda i: (i, 0))
    vmem = 6 * _nbytes((tm, D_MODEL)) + 2 * _nbytes((D_MODEL, D_MODEL), BF16) + (6 << 20)
    return pl.pallas_call(
        _proj_res_kernel,
        grid=(n // tm,),
        in_specs=[row, row, pl.BlockSpec((D_MODEL, D_MODEL), lambda i: (0, 0)), pl.BlockSpec((1, D_MODEL), lambda i: (0, 0))],
        out_specs=row,
        out_shape=jax.ShapeDtypeStruct((n, D_MODEL), F32),
        compiler_params=_params(("parallel",), vmem),
    )(x, a, w, g)


def _cross_attn_kernel(q_ref, k_ref, v_ref, o_ref):
    q = q_ref[...]
    outs = []
    for h in range(N_HEADS):
        cols = slice(h * CA_DIM, (h + 1) * CA_DIM)
        s = _bdot_t(q[:, cols], k_ref[:, cols]) * (CA_DIM ** -0.5)
        s = s - jnp.max(s, axis=1, keepdims=True)
        p = jnp.exp(s)
        p = p / jnp.sum(p, axis=1, keepdims=True)
        outs.append(_bdot(p, v_ref[:, cols]))
    o_ref[...] = jnp.concatenate(outs, axis=1)


def _cross_attn(q, mem_k, mem_v, layer, tq):
    n_b, t_len, _ = q.shape
    mem = mem_k.shape[2]
    qspec = pl.BlockSpec((None, tq, D_MODEL), lambda b, i: (b, i, 0))
    mspec = pl.BlockSpec((None, None, mem, D_MODEL), lambda b, i: (layer, b, 0, 0))
    vmem = 4 * _nbytes((tq, D_MODEL)) + 4 * _nbytes((mem, D_MODEL)) + 8 * _nbytes((tq, D_MODEL)) + (6 << 20)
    return pl.pallas_call(
        _cross_attn_kernel,
        grid=(n_b, t_len // tq),
        in_specs=[qspec, mspec, mspec],
        out_specs=qspec,
        out_shape=jax.ShapeDtypeStruct(q.shape, F32),
        compiler_params=_params(("parallel", "parallel"), vmem),
    )(q, mem_k, mem_v)


def _gelu(x):
    return 0.5 * x * (1.0 + lax.erf(x * (2.0 ** -0.5)))


def _ffn_kernel(x_ref, gpre_ref, wa_ref, wg_ref, cw_ref, cb_ref, wd_ref, st_ref, gpost_ref,
                o_ref, cnew_ref, hn, acc, carry, *, stride):
    i = pl.program_id(1)
    c = pl.program_id(2)
    tm = x_ref.shape[0]

    @pl.when(c == 0)
    def _():
        hn[...] = _rms(x_ref[...], gpre_ref[...]).astype(BF16)
        acc[...] = jnp.zeros(acc.shape, F32)

    @pl.when(i == 0)
    def _():
        carry[c] = st_ref[0]

    h = hn[...]
    a = jnp.dot(h, wa_ref[...], preferred_element_type=F32)
    gate = jnp.dot(h, wg_ref[...], preferred_element_type=F32)
    st = carry[c]
    prev1 = _shift_rows(a, st[stride:, :], stride)
    prev2 = _shift_rows(a, st, 2 * stride)
    cw = cw_ref[...]
    conv = cb_ref[...] + prev2 * cw[0:1, :] + prev1 * cw[1:2, :] + a * cw[2:3, :]
    tail = a[tm - 2 * stride :, :]
    carry[c] = tail
    fc = a.shape[1]
    for cc in range(FF_CHUNKS):
        @pl.when((c == cc) & (i == pl.num_programs(1) - 1))
        def _(cc=cc):
            cnew_ref[0, :, cc * fc : (cc + 1) * fc] = tail
    acc[...] += _bdot(_gelu(conv) * gate, wd_ref[...])

    @pl.when(c == pl.num_programs(2) - 1)
    def _():
        o_ref[...] = x_ref[...] + _rms(acc[...], gpost_ref[...])


def _ffn(x, gpre, wa, wg, cw, cb, wd, state, gpost, n_seq, tm, stride):
    n = x.shape[0]
    n_tiles = n // n_seq // tm
    fc = D_FF // FF_CHUNKS
    row = pl.BlockSpec((tm, D_MODEL), lambda q, i, c: (q * n_tiles + i, 0))
    vec = pl.BlockSpec((1, D_MODEL), lambda q, i, c: (0, 0))
    st = pl.BlockSpec((1, 2 * stride, fc), lambda q, i, c: (q, 0, c))
    vmem = (4 * _nbytes((tm, D_MODEL)) + 4 * _nbytes((D_MODEL, fc), BF16) + 2 * _nbytes((fc, D_MODEL), BF16)
            + 2 * _nbytes((tm, D_MODEL)) + 8 * _nbytes((tm, fc)) + 6 * _nbytes((2 * stride, fc)) + (4 << 20))
    return pl.pallas_call(
        functools.partial(_ffn_kernel, stride=stride),
        grid=(n_seq, n_tiles, FF_CHUNKS),
        in_specs=[row, vec,
                  pl.BlockSpec((D_MODEL, fc), lambda q, i, c: (0, c)),
                  pl.BlockSpec((D_MODEL, fc), lambda q, i, c: (0, c)),
                  pl.BlockSpec((3, fc), lambda q, i, c: (0, c)),
                  pl.BlockSpec((1, fc), lambda q, i, c: (0, c)),
                  pl.BlockSpec((fc, D_MODEL), lambda q, i, c: (c, 0)),
                  st, vec],
        out_specs=[row, pl.BlockSpec((1, 2 * stride, D_FF), lambda q, i, c: (q, 0, 0))],
        out_shape=[jax.ShapeDtypeStruct((n, D_MODEL), F32), jax.ShapeDtypeStruct((n_seq, 2 * stride, D_FF), F32)],
        scratch_shapes=[pltpu.VMEM((tm, D_MODEL), BF16), pltpu.VMEM((tm, D_MODEL), F32),
                        pltpu.VMEM((FF_CHUNKS, 2 * stride, fc), F32)],
        compiler_params=_params(("arbitrary", "arbitrary", "arbitrary"), vmem),
    )(x, gpre, wa, wg, cw, cb, wd, state, gpost)


def _pad_cols(w, width):
    return jnp.pad(w, ((0, 0), (0, width - w.shape[1])))


def _row(v):
    return v.reshape(1, -1).astype(F32)


def _block_diag(blocks):
    rows = sum(b.shape[0] for b in blocks)
    cols = sum(b.shape[1] for b in blocks)
    out = jnp.zeros((rows, cols), blocks[0].dtype)
    r = c = 0
    for b in blocks:
        out = lax.dynamic_update_slice(out, b, (r, c))
        r += b.shape[0]
        c += b.shape[1]
    return out


def _layer_weights(p, l):
    w = p["w_in"][l]
    gla, mla, rw, ml = w[:, 0:784], w[:, 784:1200], w[:, 1200:2224], w[:, 2224:3256]
    gate = w[:, 3256:]
    w_gla = jnp.concatenate([gla[:, 0:512], gla[:, 528:784], _pad_cols(gla[:, 512:528], LANES)], axis=1)
    w_mla = _pad_cols(mla, W_MLA)
    w_ml = jnp.concatenate([ml[:, 0:768], ml[:, 776:1032], _pad_cols(ml[:, 768:776], LANES)], axis=1)
    w_all = jnp.concatenate([w_gla, w_mla, rw, w_ml, gate], axis=1).astype(BF16)

    ones_bd = _block_diag([jnp.ones((HEAD, HEAD), BF16)] * N_HEADS)
    ibias = jnp.zeros((LANES,), F32).at[0:N_HEADS].set(p["mlstm_i_bias"][l])
    fbias = jnp.zeros((LANES,), F32).at[N_HEADS : 2 * N_HEADS].set(p["mlstm_f_bias"][l])
    prep = [
        jnp.pad(p["gla_w_gate2"][l], ((0, LANES - 16), (0, 0))).astype(BF16), _row(p["gla_b_gate"][l]),
        _row(p["rwkv_mu"][l]), _row(p["rwkv_w0"][l]), p["rwkv_w2"][l].astype(BF16), _row(p["rwkv_a0"][l]),
        p["rwkv_a2"][l].astype(BF16), p["rwkv_g2"][l].astype(BF16), _row(p["rwkv_k_k"][l]), _row(p["rwkv_k_a"][l]),
        _row(p["rwkv_r_k"][l]), ones_bd, _row(ibias), _row(fbias),
    ]
    wuq = p["mla_w_uq"][l].reshape(256, N_HEADS, MLA_NOPE + MLA_ROPE)
    mla_w = [
        _row(p["mla_norm_q"][l]), _row(p["mla_norm_kv"][l]),
        wuq[:, :, :MLA_NOPE].reshape(256, N_HEADS * MLA_NOPE).astype(BF16),
        wuq[:, :, MLA_NOPE:].reshape(256, N_HEADS * MLA_ROPE).astype(BF16),
        _block_diag([p["mla_w_uk"][l][h].T for h in range(N_HEADS)]).astype(BF16),
    ]
    gnorm = _row(jnp.tile(p["gla_norm"][l], N_HEADS))
    post = [ones_bd, gnorm, _row(p["rwkv_ln_w"][l]), _row(p["rwkv_ln_b"][l]), _row(p["mlstm_norm"][l])]
    merge = [
        _block_diag([p["mla_w_uv"][l][h] for h in range(N_HEADS)]).astype(BF16),
        p["w_branch"][l].astype(BF16), p["w_out"][l].astype(BF16), _row(p["norm_post_mix"][l]),
    ]
    tri = lambda n: jnp.tril(jnp.ones((n, n), F32))
    blocks = lambda r, c: _block_diag([jnp.ones((r, c), F32)] * N_HEADS)
    gla_c = prep[0:2] + [gnorm, ones_bd, tri(GLA_CHUNK).astype(BF16), jnp.tile(tri(GLA_CHUNK), (1, N_HEADS)),
                         blocks(GLA_CHUNK, GLA_K), blocks(GLA_CHUNK, HEAD), blocks(HEAD, GLA_K)]
    mlstm_c = [_row(ibias), _row(fbias), _row(p["mlstm_norm"][l]), ones_bd, tri(MIX_CHUNK).astype(BF16)]
    rwkv_c = prep[2:12] + [_row(p["rwkv_ln_w"][l]), _row(p["rwkv_ln_b"][l]), tri(MIX_CHUNK).astype(BF16),
                           jnp.tile(jnp.tril(jnp.ones((MIX_CHUNK, MIX_CHUNK), F32), -1), (1, N_HEADS)),
                           jnp.tile(tri(MIX_CHUNK), (1, N_HEADS))]
    up = p["ffn_w_up"][l]
    return dict(
        w_all=w_all, prep=prep, mla=mla_w, post=post, merge=merge, gla_c=gla_c, mlstm_c=mlstm_c, rwkv_c=rwkv_c,
        g_pre_mix=_row(p["norm_pre_mix"][l]),
        g_pre_ca=_row(p["norm_pre_ca"][l]), wq=p["ca_w_q"][l].astype(BF16), wo=p["ca_w_o"][l].astype(BF16),
        g_post_ca=_row(p["norm_post_ca"][l]),
        g_mem=_row(p["ca_norm_mem"][l]),
        wkv=jnp.concatenate([p["ca_w_k"][l], p["ca_w_v"][l]], axis=1).astype(BF16),
        g_pre_ffn=_row(p["norm_pre_ffn"][l]), wa=up[:, :D_FF].astype(BF16), wg=up[:, D_FF:].astype(BF16),
        cw=p["ffn_conv_w"][l].astype(F32), cb=_row(p["ffn_conv_b"][l]), wd=p["ffn_w_down"][l].astype(BF16),
        g_post_ffn=_row(p["norm_post_ffn"][l]),
    )


def _rope_tables(pos):
    half = MLA_ROPE // 2
    inv_freq = ROPE_THETA ** (-jnp.arange(half, dtype=F32) / half)
    ang = pos[:, None] * inv_freq[None, :]
    cos, sin = jnp.cos(ang), jnp.sin(ang)
    return jnp.concatenate([cos, cos], axis=1), jnp.concatenate([-sin, sin], axis=1)


class _Group:
    def __init__(self, n_b, t_len, time_major):
        self.n_b, self.t_len, self.time_major = n_b, t_len, time_major
        self.n_seq = 1 if time_major else n_b
        self.stride = n_b if time_major else 1
        self.rows = n_b * t_len
        self.tm = min(512, self.rows // self.n_seq)
        pairs = n_b * N_HEADS
        self.groups = max(1, LANES // pairs)
        assert not time_major or (self.groups in (1, 2) and (pairs * self.groups) % LANES == 0)

    def to_btc(self, x):
        c = x.shape[-1]
        if self.time_major:
            return x.reshape(self.t_len, self.n_b, c).transpose(1, 0, 2)
        return x.reshape(self.n_b, self.t_len, c)

    def from_btc(self, x):
        c = x.shape[-1]
        if self.time_major:
            return x.transpose(1, 0, 2).reshape(self.rows, c)
        return x.reshape(self.rows, c)

    def key_operand(self, x, dim):
        g = self.groups
        x = self.to_btc(x).reshape(self.n_b, self.t_len, N_HEADS, g, dim // g)
        return x.transpose(1, 4, 3, 0, 2).reshape(self.t_len, dim // g, g * self.n_b * N_HEADS)

    def val_operand(self, x, dim):
        x = self.to_btc(x).reshape(self.n_b, self.t_len, N_HEADS, dim)
        x = x.transpose(1, 3, 0, 2).reshape(self.t_len, dim, self.n_b * N_HEADS)
        return jnp.concatenate([x] * self.groups, axis=2)

    def scan_to_rows(self, o):
        pairs = self.n_b * N_HEADS
        o = o[:, :, :pairs].reshape(self.t_len, 3, HEAD, self.n_b, N_HEADS)
        return self.from_btc(o.transpose(3, 0, 1, 4, 2).reshape(self.n_b, self.t_len, 3 * BRANCH))

    def state_kv_in(self, s, key_axis_last):
        g = self.groups
        if key_axis_last:
            s = jnp.swapaxes(s, 2, 3)
        k, v = s.shape[2], s.shape[3]
        s = s.reshape(self.n_b, N_HEADS, g, k // g, v)
        return s.transpose(3, 4, 2, 0, 1).reshape(k // g, v, g * self.n_b * N_HEADS)

    def state_kv_out(self, s, key_axis_last):
        g = self.groups
        kg, v, _ = s.shape
        s = s.reshape(kg, v, g, self.n_b, N_HEADS).transpose(3, 4, 2, 0, 1).reshape(self.n_b, N_HEADS, g * kg, v)
        return jnp.swapaxes(s, 2, 3) if key_axis_last else s

    def state_k_in(self, s):
        g = self.groups
        k = s.shape[2]
        s = s.reshape(self.n_b, N_HEADS, g, k // g)
        return s.transpose(3, 2, 0, 1).reshape(k // g, g * self.n_b * N_HEADS)

    def state_k_out(self, s):
        g = self.groups
        kg = s.shape[0]
        return s.reshape(kg, g, self.n_b, N_HEADS).transpose(2, 3, 1, 0).reshape(self.n_b, N_HEADS, g * kg)

    def state_m_in(self, s):
        row = jnp.concatenate([s.reshape(1, self.n_b * N_HEADS)] * self.groups, axis=1)
        return jnp.concatenate([row, jnp.zeros((SUBLANES - 1, row.shape[1]), F32)], axis=0)

    def state_m_out(self, s):
        return s[0, : self.n_b * N_HEADS].reshape(self.n_b, N_HEADS)


def _diag_blocks(s, rows, cols):
    n_b = s.shape[0]
    s = s.reshape(n_b, N_HEADS, rows, N_HEADS, cols)
    return jnp.stack([s[:, h, :, h, :] for h in range(N_HEADS)], axis=1)


def _mixer(grp, x, lw, states, pos0, mla_attend):
    p_gla, p_mla, p_rwkv, p_mlstm, p_gate = _mix_in(x, lw["g_pre_mix"], lw["w_all"], min(256, grp.tm))

    pos = pos0 + jnp.arange(grp.t_len, dtype=F32)
    cos2, sin2 = _rope_tables(pos)
    if grp.time_major:
        cos2, sin2 = jnp.repeat(cos2, grp.n_b, axis=0), jnp.repeat(sin2, grp.n_b, axis=0)
    q_lat, q_pe, ckv, kpe = _mla_prep(p_mla, cos2, sin2, lw["mla"], grp.n_seq, grp.tm)
    o_lat = mla_attend(q_lat, q_pe, ckv, kpe)

    recur = _recurrent_scan if grp.time_major else _recurrent_chunked
    o_gla, o_rwkv, o_mlstm, rec_state = recur(grp, p_gla, p_rwkv, p_mlstm, lw, states)
    x_new = _merge(x, o_gla, o_lat, o_rwkv, o_mlstm, p_gate, lw["merge"], min(256, grp.tm))
    gla_new, rwkv_new, mc_new, mn_new, mm_new = rec_state
    shift_new = grp.to_btc(p_rwkv)[:, -1]
    return x_new, (ckv, kpe, gla_new, rwkv_new, shift_new, mc_new, mn_new, mm_new)


def _recurrent_chunked(grp, p_gla, p_rwkv, p_mlstm, lw, states):
    del states
    o_gla, s_gla = _gla_chunk(p_gla, lw["gla_c"], grp.n_b, grp.t_len)
    o_ml, c_new, n_new, m_new = _mlstm_chunk(p_mlstm, lw["mlstm_c"], grp.n_b, grp.t_len)
    o_rw, s_rw = _rwkv_chunk(p_rwkv, lw["rwkv_c"], grp.n_b, grp.t_len)
    gla_new = jnp.swapaxes(_diag_blocks(s_gla, HEAD, GLA_K), 2, 3)
    rwkv_new = _diag_blocks(s_rw, HEAD, HEAD)
    return o_gla, o_rw, o_ml, (gla_new, rwkv_new, c_new, n_new[:, :N_HEADS, :], m_new[:, :N_HEADS, 0])


def _recurrent_scan(grp, p_gla, p_rwkv, p_mlstm, lw, states):
    gla_s, rwkv_s, shift_s, mc_s, mn_s, mm_s = states
    shift0 = shift_s.reshape(grp.n_seq, grp.stride, W_RWKV)
    gla_ops, rw_ops, aux, ml_ops = _prep(p_gla, p_rwkv, p_mlstm, shift0, lw["prep"], grp.n_seq, grp.tm, grp.stride)
    g = grp.groups
    gk = jnp.concatenate([grp.key_operand(gla_ops[:, 0:128], GLA_K), grp.key_operand(p_gla[:, 128:256], GLA_K),
                          grp.key_operand(gla_ops[:, 128:256], GLA_K)], axis=1)
    rk = jnp.concatenate([grp.key_operand(rw_ops[:, c * 256 : (c + 1) * 256], HEAD) for c in range(5)], axis=1)
    mk = jnp.concatenate([grp.key_operand(ml_ops[:, 0:256], HEAD), grp.key_operand(p_mlstm[:, 256:512], HEAD)], axis=1)
    gates = grp.to_btc(ml_ops[:, 256 : 256 + 2 * N_HEADS]).reshape(grp.n_b, grp.t_len, 2, N_HEADS)
    gates = gates.transpose(1, 2, 0, 3).reshape(grp.t_len, 2, grp.n_b * N_HEADS)
    gates = jnp.concatenate([gates] * g, axis=2)
    vops = jnp.concatenate([grp.val_operand(p_gla[:, 256:512], HEAD), grp.val_operand(rw_ops[:, 1280:1536], HEAD),
                            grp.val_operand(p_mlstm[:, 512:768], HEAD), gates,
                            jnp.zeros((grp.t_len, SUBLANES - 2, gates.shape[2]), F32)], axis=1)
    st_in = [grp.state_kv_in(gla_s, False), grp.state_kv_in(rwkv_s, True), grp.state_kv_in(mc_s, False),
             grp.state_k_in(mn_s), grp.state_m_in(mm_s)]
    o_scan, sg, sr, sc, sn, sm = _scan(gk, rk, mk, vops, st_in, grp.t_len, g)
    scan_rows = grp.scan_to_rows(o_scan)
    o_gla, o_rwkv, o_mlstm = _post(scan_rows, p_gla, aux, p_mlstm, lw["post"], min(256, grp.tm))
    return o_gla, o_rwkv, o_mlstm, (grp.state_kv_out(sg, False), grp.state_kv_out(sr, True),
                                    grp.state_kv_out(sc, False), grp.state_k_out(sn), grp.state_m_out(sm))


def _cross_and_ffn(grp, x, lw, mem_k, mem_v, layer, conv_state):
    (q,) = _norm_matmul(x, lw["g_pre_ca"], lw["wq"], (D_MODEL,), grp.tm)
    q = grp.to_btc(q)
    ca = _cross_attn(q, mem_k, mem_v, layer, min(512, grp.t_len))
    x = _proj_res(x, grp.from_btc(ca), lw["wo"], lw["g_post_ca"], grp.tm)
    x, conv_new = _ffn(x, lw["g_pre_ffn"], lw["wa"], lw["wg"], lw["cw"], lw["cb"], lw["wd"], conv_state,
                       lw["g_post_ffn"], grp.n_seq, grp.tm, grp.stride)
    return x, conv_new


def kernel(x_prompt, x_sample, mem_prompt, cache_ckv, cache_kpe, page_table, cache_mem_k, cache_mem_v, state_gla, state_rwkv, state_rwkv_shift, state_mlstm_c, state_mlstm_n, state_mlstm_m, state_conv, norm_pre_mix, norm_post_mix, norm_pre_ca, norm_post_ca, norm_pre_ffn, norm_post_ffn, w_in, gla_w_gate2, gla_b_gate, gla_norm, mla_norm_q, mla_norm_kv, mla_w_uq, mla_w_uk, mla_w_uv, rwkv_mu, rwkv_w0, rwkv_w2, rwkv_a0, rwkv_a2, rwkv_g2, rwkv_k_k, rwkv_k_a, rwkv_r_k, rwkv_ln_w, rwkv_ln_b, mlstm_i_bias, mlstm_f_bias, mlstm_norm, w_branch, w_out, ca_norm_mem, ca_w_q, ca_w_k, ca_w_v, ca_w_o, ffn_w_up, ffn_conv_w, ffn_conv_b, ffn_w_down):
    p = dict(norm_pre_mix=norm_pre_mix, norm_post_mix=norm_post_mix, norm_pre_ca=norm_pre_ca, norm_post_ca=norm_post_ca,
             norm_pre_ffn=norm_pre_ffn, norm_post_ffn=norm_post_ffn, w_in=w_in, gla_w_gate2=gla_w_gate2,
             gla_b_gate=gla_b_gate, gla_norm=gla_norm, mla_norm_q=mla_norm_q, mla_norm_kv=mla_norm_kv,
             mla_w_uq=mla_w_uq, mla_w_uk=mla_w_uk, mla_w_uv=mla_w_uv, rwkv_mu=rwkv_mu, rwkv_w0=rwkv_w0,
             rwkv_w2=rwkv_w2, rwkv_a0=rwkv_a0, rwkv_a2=rwkv_a2, rwkv_g2=rwkv_g2, rwkv_k_k=rwkv_k_k,
             rwkv_k_a=rwkv_k_a, rwkv_r_k=rwkv_r_k, rwkv_ln_w=rwkv_ln_w, rwkv_ln_b=rwkv_ln_b,
             mlstm_i_bias=mlstm_i_bias, mlstm_f_bias=mlstm_f_bias, mlstm_norm=mlstm_norm, w_branch=w_branch,
             w_out=w_out, ca_norm_mem=ca_norm_mem, ca_w_q=ca_w_q, ca_w_k=ca_w_k, ca_w_v=ca_w_v, ca_w_o=ca_w_o,
             ffn_w_up=ffn_w_up, ffn_conv_w=ffn_conv_w, ffn_conv_b=ffn_conv_b, ffn_w_down=ffn_w_down)
    depth = w_in.shape[0]
    b_p, t_p, _ = x_prompt.shape
    b_s, t_s, _ = x_sample.shape
    n_pages, page = page_table.shape[1], cache_ckv.shape[2]
    past_len = n_pages * page
    mem_len = mem_prompt.shape[1]
    gp = _Group(b_p, t_p, time_major=False)
    gs = _Group(b_s, t_s, time_major=True)

    xp = gp.from_btc(x_prompt)
    xs = gs.from_btc(x_sample)
    mem_rows = mem_prompt.reshape(b_p * mem_len, D_MODEL)
    mem_k_c = cache_mem_k.reshape(depth, b_s, mem_len, D_MODEL)
    mem_v_c = cache_mem_v.reshape(depth, b_s, mem_len, D_MODEL)
    cache_kpe_t = jnp.swapaxes(cache_kpe, 2, 3)

    conv0_p = jnp.zeros((b_p, 2, D_FF), F32)

    new_p, new_s, mem_k_new, mem_v_new = [], [], [], []
    for l in range(depth):
        lw = _layer_weights(p, l)

        mk, mv = _norm_matmul(mem_rows, lw["g_mem"], lw["wkv"], (D_MODEL, D_MODEL), min(512, mem_rows.shape[0]))
        attend_p = lambda q_lat, q_pe, ckv, kpe: _mla_prompt(q_lat, q_pe, ckv, kpe, b_p, t_p)
        xp, st = _mixer(gp, xp, lw, None, 0.0, attend_p)
        xp, conv_new = _cross_and_ffn(gp, xp, lw, mk.reshape(1, b_p, mem_len, D_MODEL),
                                      mv.reshape(1, b_p, mem_len, D_MODEL), 0, conv0_p)
        new_p.append(st + (conv_new,))
        mem_k_new.append(mk.reshape(b_p, mem_len, N_HEADS, CA_DIM))
        mem_v_new.append(mv.reshape(b_p, mem_len, N_HEADS, CA_DIM))

        def attend_s(q_lat, q_pe, ckv, kpe, l=l):
            rows = t_s * N_HEADS
            ql = gs.to_btc(q_lat).reshape(b_s, rows, MLA_LORA)
            qp = gs.to_btc(q_pe).reshape(b_s, rows, MLA_ROPE)
            cn = jnp.pad(gs.to_btc(ckv), ((0, 0), (0, page - t_s), (0, 0)))
            pn = jnp.pad(gs.to_btc(kpe), ((0, 0), (0, page - t_s), (0, 0)))
            o = _mla_sample(page_table, ql, qp, cn, pn, cache_ckv, cache_kpe_t, l)
            return gs.from_btc(o.reshape(b_s, t_s, N_HEADS * MLA_LORA))

        st_s = (state_gla[l], state_rwkv[l], state_rwkv_shift[l], state_mlstm_c[l], state_mlstm_n[l], state_mlstm_m[l])
        xs, st = _mixer(gs, xs, lw, st_s, float(past_len), attend_s)
        conv_s = state_conv[l].transpose(1, 0, 2).reshape(1, 2 * b_s, D_FF)
        xs, conv_new = _cross_and_ffn(gs, xs, lw, mem_k_c, mem_v_c, l, conv_s)
        new_s.append(st + (conv_new.reshape(2, b_s, D_FF).transpose(1, 0, 2),))

    def stack(per_layer, i, grp):
        vals = [st[i] for st in per_layer]
        if i in (0, 1):
            vals = [grp.to_btc(v) for v in vals]
        return jnp.stack(vals, axis=0)

    outs = [gp.to_btc(xp), gs.to_btc(xs), stack(new_p, 0, gp), stack(new_p, 1, gp), stack(new_s, 0, gs), stack(new_s, 1, gs),
            jnp.stack(mem_k_new, axis=0), jnp.stack(mem_v_new, axis=0)]
    for i in range(2, 9):
        outs.append(stack(new_p, i, gp))
        outs.append(stack(new_s, i, gs))
    return tuple(outs)
```

```python
import functools
import math

import jax
import jax.numpy as jnp
from jax import lax
from jax.experimental import pallas as pl
from jax.experimental.pallas import tpu as pltpu

F32 = jnp.float32
BF16 = jnp.bfloat16

D_MODEL = 1024
BRANCH = 256
N_HEADS = 4
EPS = 1e-6
GLA_K = 32
GLA_TAU = 16.0
MLA_NOPE = 64
MLA_ROPE = 32
MLA_LORA = 128
ROPE_THETA = 10000.0
RWKV_LN_EPS = 64e-5
HEAD = 64
CA_DIM = 256
D_FF = 2816
FF_CHUNKS = 1

LANES = 128
SUBLANES = 8
VMEM_BUDGET = 56 * 1024 * 1024

W_GLA, W_MLA, W_RWKV, W_MLSTM, W_GATE = 896, 512, 1024, 1152, 4096
PIECES = (W_GLA, W_MLA, W_RWKV, W_MLSTM, W_GATE)


def _params(sem, vmem_bytes):
    return pltpu.CompilerParams(dimension_semantics=sem, vmem_limit_bytes=int(min(max(vmem_bytes, 16 << 20), VMEM_BUDGET)))


def _nbytes(shape, dtype=F32):
    return math.prod(shape) * jnp.dtype(dtype).itemsize


def _rms(x, g):
    return x * lax.rsqrt(jnp.mean(x * x, axis=-1, keepdims=True) + EPS) * g


def _bdot(a, b):
    return jnp.dot(a.astype(BF16), b.astype(BF16), preferred_element_type=F32)


def _bdot_t(a, b):
    return lax.dot_general(a.astype(BF16), b.astype(BF16), (((1,), (1,)), ((), ())), preferred_element_type=F32)


def _group_sum(x, ones_bd):
    hi = x.astype(BF16)
    lo = (x - hi.astype(F32)).astype(BF16)
    return jnp.dot(hi, ones_bd, preferred_element_type=F32) + jnp.dot(lo, ones_bd, preferred_element_type=F32)


def _softplus(x):
    return jnp.maximum(x, 0.0) + jnp.log1p(jnp.exp(-jnp.abs(x)))


def _log_sigmoid(x):
    return -_softplus(-x)


def _shift_rows(x, first, s):
    n = x.shape[0]
    if s % SUBLANES == 0:
        return jnp.concatenate([first, x[: n - s]], axis=0)
    rolled = pltpu.roll(x, s, 0)
    row = lax.broadcasted_iota(jnp.int32, x.shape, 0)
    for r in range(s):
        rolled = jnp.where(row == r, first[r : r + 1, :], rolled)
    return rolled


def _mix_in_kernel(x_ref, g_ref, w_ref, *outs):
    h = _rms(x_ref[...], g_ref[...]).astype(BF16)
    off = 0
    for o_ref, width in zip(outs, PIECES):
        piece = jnp.dot(h, w_ref[:, off : off + width], preferred_element_type=F32)
        if o_ref.dtype == BF16:
            piece = jax.nn.sigmoid(piece)
        o_ref[...] = piece.astype(o_ref.dtype)
        off += width


def _mix_in(x, g, w_all, tm):
    n = x.shape[0]
    total = sum(PIECES)
    vmem = 2 * _nbytes((tm, D_MODEL)) + 2 * _nbytes((D_MODEL, total), BF16) + 2 * _nbytes((tm, total)) + (4 << 20)
    return pl.pallas_call(
        _mix_in_kernel,
        grid=(n // tm,),
        in_specs=[
            pl.BlockSpec((tm, D_MODEL), lambda i: (i, 0)),
            pl.BlockSpec((1, D_MODEL), lambda i: (0, 0)),
            pl.BlockSpec((D_MODEL, total), lambda i: (0, 0)),
        ],
        out_specs=[pl.BlockSpec((tm, w), lambda i: (i, 0)) for w in PIECES],
        out_shape=[jax.ShapeDtypeStruct((n, w), BF16 if w == W_GATE else F32) for w in PIECES],
        compiler_params=_params(("parallel",), vmem),
    )(x, g, w_all)


def _prep_kernel(pg_ref, pr_ref, pm_ref, shift_ref,
                 gate2_ref, bgate_ref, mu_ref, w0_ref, w2_ref, a0_ref, a2_ref, g2_ref, kk_ref, ka_ref, rk_ref,
                 ones_ref, ibias_ref, fbias_ref,
                 gla_ref, rw_ref, aux_ref, ml_ref, carry, *, stride):
    i = pl.program_id(1)
    tm = pr_ref.shape[0]

    @pl.when(i == 0)
    def _():
        carry[...] = shift_ref[0]

    pr = pr_ref[...]
    prev = _shift_rows(pr, carry[...], stride)
    carry[...] = pr[tm - stride :, :]
    log_w, kk, a, k_mod, rr, rv, g, bonus = _rwkv_mix(
        pr, prev, mu_ref[...], w0_ref[...], w2_ref[...], a0_ref[...], a2_ref[...], g2_ref[...], kk_ref[...],
        ka_ref[...], rk_ref[...], ones_ref[...])
    rw_ref[:, 0:256] = jnp.exp(log_w)
    rw_ref[:, 256:512] = kk
    rw_ref[:, 512:768] = kk * a
    rw_ref[:, 768:1024] = k_mod
    rw_ref[:, 1024:1280] = rr
    rw_ref[:, 1280:1536] = rv
    aux_ref[:, 0:256] = g
    aux_ref[:, 256:512] = bonus

    pg = pg_ref[...]
    z = _bdot(pg[:, 768:896], gate2_ref[...]) + bgate_ref[...]
    gla_ref[:, 0:128] = pg[:, 0:128] * (GLA_K ** -0.5)
    gla_ref[:, 128:256] = jnp.exp(_log_sigmoid(z) / GLA_TAU)

    pml = pm_ref[...]
    ifp = pml[:, 1024:1152]
    lane = lax.broadcasted_iota(jnp.int32, ifp.shape, 1)
    ml_ref[:, 0:256] = pml[:, 0:256] * (HEAD ** -0.5)
    ml_ref[:, 256:384] = jnp.where(lane < N_HEADS, ifp + ibias_ref[...], _log_sigmoid(ifp + fbias_ref[...]))


def _prep(p_gla, p_rwkv, p_mlstm, shift0, wts, n_seq, tm, stride):
    n = p_rwkv.shape[0]
    n_tiles = n // n_seq // tm
    row = lambda w: pl.BlockSpec((tm, w), lambda q, i: (q * n_tiles + i, 0))
    const = lambda a: pl.BlockSpec(a.shape, lambda q, i: (0,) * a.ndim)
    out_w = (256, 1536, 512, 384)
    vmem = 2 * _nbytes((tm, W_GLA + W_RWKV + W_MLSTM + sum(out_w))) + 8 * _nbytes((tm, W_RWKV)) + (6 << 20)
    return pl.pallas_call(
        functools.partial(_prep_kernel, stride=stride),
        grid=(n_seq, n_tiles),
        in_specs=[row(W_GLA), row(W_RWKV), row(W_MLSTM),
                  pl.BlockSpec((1, stride, W_RWKV), lambda q, i: (q, 0, 0))] + [const(a) for a in wts],
        out_specs=[row(w) for w in out_w],
        out_shape=[jax.ShapeDtypeStruct((n, w), F32) for w in out_w],
        scratch_shapes=[pltpu.VMEM((stride, W_RWKV), F32)],
        compiler_params=_params(("arbitrary", "arbitrary"), vmem),
    )(p_gla, p_rwkv, p_mlstm, shift0, *wts)


def _rope32(x, cos2, sin2):
    half = MLA_ROPE // 2
    swapped = jnp.concatenate([x[:, half:], x[:, :half]], axis=1)
    return x * cos2 + swapped * sin2


def _mla_prep_kernel(p_ref, cos_ref, sin_ref, nq_ref, nkv_ref, wn_ref, wp_ref, wuk_ref,
                     qlat_ref, qpe_ref, ckv_ref, kpe_ref):
    p = p_ref[...]
    cos2, sin2 = cos_ref[...], sin_ref[...]
    cq = _rms(p[:, 0:256], nq_ref[...])
    q_nope = _bdot(cq, wn_ref[...])
    q_pe = _bdot(cq, wp_ref[...])
    qlat_ref[...] = _bdot(q_nope, wuk_ref[...])
    qpe_ref[...] = jnp.concatenate(
        [_rope32(q_pe[:, h * MLA_ROPE : (h + 1) * MLA_ROPE], cos2, sin2) for h in range(N_HEADS)], axis=1)
    ckv_ref[...] = _rms(p[:, 256:384], nkv_ref[...])
    kpe_ref[...] = _rope32(p[:, 384:416], cos2, sin2)


def _mla_prep(p_mla, cos2, sin2, wts, n_seq, tm):
    n = p_mla.shape[0]
    n_tiles = n // n_seq // tm
    row = lambda w: pl.BlockSpec((tm, w), lambda q, i: (q * n_tiles + i, 0))
    tab = pl.BlockSpec((tm, MLA_ROPE), lambda q, i: (i, 0))
    const = lambda a: pl.BlockSpec(a.shape, lambda q, i: (0,) * a.ndim)
    out_w = (N_HEADS * MLA_LORA, N_HEADS * MLA_ROPE, MLA_LORA, MLA_ROPE)
    vmem = 2 * _nbytes((tm, W_MLA + sum(out_w) + 2 * LANES)) + 6 * _nbytes((tm, 512)) + (4 << 20)
    return pl.pallas_call(
        _mla_prep_kernel,
        grid=(n_seq, n_tiles),
        in_specs=[row(W_MLA), tab, tab] + [const(a) for a in wts],
        out_specs=[row(w) for w in out_w],
        out_shape=[jax.ShapeDtypeStruct((n, w), F32) for w in out_w],
        compiler_params=_params(("parallel", "parallel"), vmem),
    )(p_mla, cos2, sin2, *wts)


MLA_SCALE = (MLA_NOPE + MLA_ROPE) ** -0.5


def _stack_heads(x, width):
    return jnp.concatenate([x[:, h * width : (h + 1) * width] for h in range(N_HEADS)], axis=0)


def _mla_prompt_kernel(ql_ref, qp_ref, ckv_ref, kpe_ref, o_ref, *, tq, tk):
    qi = pl.program_id(1)
    q = (_stack_heads(ql_ref[...], MLA_LORA) * MLA_SCALE).astype(BF16)
    qp = (_stack_heads(qp_ref[...], MLA_ROPE) * MLA_SCALE).astype(BF16)
    rows = N_HEADS * tq

    def block(kb, carry, causal):
        m, l, acc = carry
        k0 = pl.multiple_of(kb * tk, tk)
        kc = ckv_ref[pl.ds(k0, tk), :].astype(BF16)
        kp = kpe_ref[pl.ds(k0, tk), :].astype(BF16)
        s = _bdot_t(q, kc) + _bdot_t(qp, kp)
        if causal:
            tpos = lax.broadcasted_iota(jnp.int32, (tq, tk), 0)
            qpos = jnp.concatenate([tpos] * N_HEADS, axis=0)
            s = jnp.where(lax.broadcasted_iota(jnp.int32, (rows, tk), 1) <= qpos, s, -jnp.inf)
        m_new = jnp.maximum(m, jnp.max(s, axis=1, keepdims=True))
        p = jnp.exp(s - m_new)
        alpha = jnp.exp(m - m_new)
        l = alpha * l + jnp.sum(p, axis=1, keepdims=True)
        acc = alpha * acc + jnp.dot(p.astype(BF16), kc, preferred_element_type=F32)
        return m_new, l, acc

    init = (jnp.full((rows, 1), -jnp.inf, F32), jnp.zeros((rows, 1), F32), jnp.zeros((rows, MLA_LORA), F32))
    carry = lax.fori_loop(0, qi, functools.partial(block, causal=False), init)
    _, l, acc = block(qi, carry, causal=True)
    out = acc / l
    o_ref[...] = jnp.concatenate([out[h * tq : (h + 1) * tq, :] for h in range(N_HEADS)], axis=1)


def _mla_prompt(q_lat, q_pe, ckv, kpe, n_b, t_len):
    tq = tk = min(256, t_len)
    nq = t_len // tq
    vmem = 2 * _nbytes((tq, 1024 + 128)) + 2 * _nbytes((t_len, 256)) + 12 * _nbytes((N_HEADS * tq, tk)) + (4 << 20)
    return pl.pallas_call(
        functools.partial(_mla_prompt_kernel, tq=tq, tk=tk),
        grid=(n_b, nq),
        in_specs=[
            pl.BlockSpec((tq, N_HEADS * MLA_LORA), lambda b, i: (b * nq + i, 0)),
            pl.BlockSpec((tq, N_HEADS * MLA_ROPE), lambda b, i: (b * nq + i, 0)),
            pl.BlockSpec((t_len, MLA_LORA), lambda b, i: (b, 0)),
            pl.BlockSpec((t_len, MLA_ROPE), lambda b, i: (b, 0)),
        ],
        out_specs=pl.BlockSpec((tq, N_HEADS * MLA_LORA), lambda b, i: (b * nq + i, 0)),
        out_shape=jax.ShapeDtypeStruct((n_b * t_len, N_HEADS * MLA_LORA), F32),
        compiler_params=_params(("parallel", "parallel"), vmem),
    )(q_lat, q_pe, ckv, kpe)


def _mla_sample_kernel(pt_ref, ql_ref, qp_ref, cnew_ref, pnew_ref, *rest, n_pg, page):
    ckv_refs = rest[:n_pg]
    kpe_refs = rest[n_pg : 2 * n_pg]
    o_ref = rest[2 * n_pg]
    m_s, l_s, acc_s = rest[2 * n_pg + 1 :]
    g = pl.program_id(1)
    rows = ql_ref.shape[1]

    @pl.when(g == 0)
    def _():
        m_s[...] = jnp.full(m_s.shape, -jnp.inf, F32)
        l_s[...] = jnp.zeros(l_s.shape, F32)
        acc_s[...] = jnp.zeros(acc_s.shape, F32)

    q = (ql_ref[0] * MLA_SCALE).astype(BF16)
    qp = (qp_ref[0] * MLA_SCALE).astype(BF16)

    def update(s, v):
        m = m_s[...]
        m_new = jnp.maximum(m, jnp.max(s, axis=1, keepdims=True))
        alpha = jnp.exp(m - m_new)
        p = jnp.exp(s - m_new)
        m_s[...] = m_new
        l_s[...] = alpha * l_s[...] + jnp.sum(p, axis=1, keepdims=True)
        acc_s[...] = alpha * acc_s[...] + jnp.dot(p.astype(BF16), v, preferred_element_type=F32)

    kc = jnp.concatenate([c_ref[0, 0].astype(BF16) for c_ref in ckv_refs], axis=0)
    kp_t = jnp.concatenate([p_ref[0, 0].astype(BF16) for p_ref in kpe_refs], axis=1)
    update(_bdot_t(q, kc) + jnp.dot(qp, kp_t, preferred_element_type=F32), kc)

    @pl.when(g == pl.num_programs(1) - 1)
    def _():
        kn = cnew_ref[0].astype(BF16)
        kpn = pnew_ref[0].astype(BF16)
        s = _bdot_t(q, kn) + _bdot_t(qp, kpn)
        row = lax.broadcasted_iota(jnp.int32, (rows, page), 0)
        tk = lax.broadcasted_iota(jnp.int32, (rows, page), 1)
        s = jnp.where(tk * N_HEADS <= row, s, -jnp.inf)
        update(s, kn)
        o_ref[0] = acc_s[...] / l_s[...]


def _mla_sample(page_table, q_lat, q_pe, ckv_new, kpe_new, cache_ckv, cache_kpe_t, layer):
    n_b, n_pages = page_table.shape
    page = cache_ckv.shape[2]
    rows = q_lat.shape[1]
    n_pg = math.gcd(n_pages, 32)
    n_groups = n_pages // n_pg
    pt_flat = page_table.reshape(-1)

    def page_spec(shape, p):
        return pl.BlockSpec((1, 1) + shape, lambda b, g, pt: (layer, pt[b * n_pages + g * n_pg + p], 0, 0))

    per_b = lambda shape: pl.BlockSpec((1,) + shape, lambda b, g, pt: (b, 0, 0))
    grid_spec = pltpu.PrefetchScalarGridSpec(
        num_scalar_prefetch=1,
        grid=(n_b, n_groups),
        in_specs=[per_b((rows, MLA_LORA)), per_b((rows, MLA_ROPE)), per_b((page, MLA_LORA)), per_b((page, MLA_ROPE))]
        + [page_spec((page, MLA_LORA), p) for p in range(n_pg)] + [page_spec((MLA_ROPE, page), p) for p in range(n_pg)],
        out_specs=per_b((rows, MLA_LORA)),
        scratch_shapes=[pltpu.VMEM((rows, 1), F32), pltpu.VMEM((rows, 1), F32), pltpu.VMEM((rows, MLA_LORA), F32)],
    )
    vmem = 2 * n_pg * 2 * _nbytes((page, LANES)) + 4 * _nbytes((page, 2 * LANES)) + (8 << 20)
    return pl.pallas_call(
        functools.partial(_mla_sample_kernel, n_pg=n_pg, page=page),
        grid_spec=grid_spec,
        out_shape=jax.ShapeDtypeStruct((n_b, rows, MLA_LORA), F32),
        compiler_params=_params(("parallel", "arbitrary"), vmem),
    )(pt_flat, q_lat, q_pe, ckv_new, kpe_new, *([cache_ckv] * n_pg), *([cache_kpe_t] * n_pg))


V_TILES = HEAD // SUBLANES


def _scan_kernel(gk_ref, rk_ref, mk_ref, v_ref, sg0, sr0, sc0, sn0, sm0,
                 o_ref, sg_o, sr_o, sc_o, sn_o, sm_o, sg, sr, sc, sn, sm, *, tb, ksg, ks, groups):
    i = pl.program_id(1)

    @pl.when(i == 0)
    def _():
        sg[...] = sg0[...]
        sr[...] = sr0[...]
        sc[...] = sc0[...]
        sn[...] = sn0[...]
        sm[...] = sm0[...]

    def fold(x):
        return x + pltpu.roll(x, LANES // 2, 1) if groups == 2 else x

    def vt_slice(vt):
        return slice(vt * SUBLANES, (vt + 1) * SUBLANES)

    def rwkv_sa(t):
        acc = [jnp.zeros((SUBLANES, LANES), F32)] * V_TILES
        for k in range(ks):
            kk = rk_ref[t, ks + k : ks + k + 1, :]
            for vt in range(V_TILES):
                acc[vt] = acc[vt] + sr[k, vt_slice(vt), :] * kk
        return tuple(fold(a) for a in acc)

    def step(t, sa):
        tn = jnp.minimum(t + 1, tb - 1)
        zeros = [jnp.zeros((SUBLANES, LANES), F32)] * V_TILES

        gv = [v_ref[t, vt * SUBLANES : (vt + 1) * SUBLANES, :] for vt in range(V_TILES)]
        o = list(zeros)
        for k in range(ksg):
            q = gk_ref[t, k : k + 1, :]
            kr = gk_ref[t, ksg + k : ksg + k + 1, :]
            a = gk_ref[t, 2 * ksg + k : 2 * ksg + k + 1, :]
            for vt in range(V_TILES):
                new = sg[k, vt_slice(vt), :] * a + gv[vt] * kr
                sg[k, vt_slice(vt), :] = new
                o[vt] = o[vt] + new * q
        for vt in range(V_TILES):
            o_ref[t, vt_slice(vt), :] = fold(o[vt])

        rv = [v_ref[t, HEAD + vt * SUBLANES : HEAD + (vt + 1) * SUBLANES, :] for vt in range(V_TILES)]
        y = list(zeros)
        san = list(zeros)
        for k in range(ks):
            w = rk_ref[t, k : k + 1, :]
            b = rk_ref[t, 2 * ks + k : 2 * ks + k + 1, :]
            kr = rk_ref[t, 3 * ks + k : 3 * ks + k + 1, :]
            r = rk_ref[t, 4 * ks + k : 4 * ks + k + 1, :]
            kkn = rk_ref[tn, ks + k : ks + k + 1, :]
            for vt in range(V_TILES):
                new = sr[k, vt_slice(vt), :] * w - sa[vt] * b + rv[vt] * kr
                sr[k, vt_slice(vt), :] = new
                y[vt] = y[vt] + new * r
                san[vt] = san[vt] + new * kkn
        for vt in range(V_TILES):
            o_ref[t, HEAD + vt * SUBLANES : HEAD + (vt + 1) * SUBLANES, :] = fold(y[vt])

        mv = [v_ref[t, 2 * HEAD + vt * SUBLANES : 2 * HEAD + (vt + 1) * SUBLANES, :] for vt in range(V_TILES)]
        i_t = v_ref[t, 3 * HEAD : 3 * HEAD + 1, :]
        lf = v_ref[t, 3 * HEAD + 1 : 3 * HEAD + 2, :]
        m_old = sm[0:1, :]
        m_new = jnp.maximum(lf + m_old, i_t)
        fs = jnp.exp(lf + m_old - m_new)
        isc = jnp.exp(i_t - m_new)
        sm[0:1, :] = m_new
        q_tile = mk_ref[t, 0:ks, :]
        n_new = fs * sn[...] + isc * mk_ref[t, ks : 2 * ks, :]
        sn[...] = n_new
        den = fold(jnp.sum(n_new * q_tile, axis=0, keepdims=True))
        fs_b = jnp.broadcast_to(fs, (SUBLANES, LANES))
        num = list(zeros)
        for k in range(ks):
            q = mk_ref[t, k : k + 1, :]
            ik = isc * mk_ref[t, ks + k : ks + k + 1, :]
            for vt in range(V_TILES):
                new = sc[k, vt_slice(vt), :] * fs_b + mv[vt] * ik
                sc[k, vt_slice(vt), :] = new
                num[vt] = num[vt] + new * q
        scale = 1.0 / jnp.maximum(jnp.abs(den), jnp.exp(-m_new))
        for vt in range(V_TILES):
            o_ref[t, 2 * HEAD + vt * SUBLANES : 2 * HEAD + (vt + 1) * SUBLANES, :] = fold(num[vt]) * scale
        return tuple(fold(x) for x in san)

    lax.fori_loop(0, tb, step, rwkv_sa(0))

    @pl.when(i == pl.num_programs(1) - 1)
    def _():
        sg_o[...] = sg[...]
        sr_o[...] = sr[...]
        sc_o[...] = sc[...]
        sn_o[...] = sn[...]
        sm_o[...] = sm[...]


def _scan(gk, rk, mk, vops, states, t_len, groups):
    ks = HEAD // groups
    ksg = GLA_K // groups
    n_lanes = gk.shape[2]
    n_lt = n_lanes // LANES
    tb = min(32, t_len)
    op = lambda rows: pl.BlockSpec((tb, rows, LANES), lambda j, i: (i, 0, j))
    st3 = lambda k: pl.BlockSpec((k, HEAD, LANES), lambda j, i: (0, 0, j))
    st2 = lambda k: pl.BlockSpec((k, LANES), lambda j, i: (0, j))
    v_rows = vops.shape[1]
    st_specs = [st3(ksg), st3(ks), st3(ks), st2(ks), st2(SUBLANES)]
    st_shapes = [(ksg, HEAD, n_lanes), (ks, HEAD, n_lanes), (ks, HEAD, n_lanes), (ks, n_lanes), (SUBLANES, n_lanes)]
    blk = _nbytes((tb, 3 * ksg + 5 * ks + 2 * ks + v_rows + 3 * HEAD, LANES))
    st_bytes = _nbytes(((ksg + 2 * ks) * HEAD + ks + SUBLANES, LANES))
    return pl.pallas_call(
        functools.partial(_scan_kernel, tb=tb, ksg=ksg, ks=ks, groups=groups),
        grid=(n_lt, t_len // tb),
        in_specs=[op(3 * ksg), op(5 * ks), op(2 * ks), op(v_rows)] + st_specs,
        out_specs=[op(3 * HEAD)] + st_specs,
        out_shape=[jax.ShapeDtypeStruct((t_len, 3 * HEAD, n_lanes), F32)]
        + [jax.ShapeDtypeStruct(s, F32) for s in st_shapes],
        scratch_shapes=[pltpu.VMEM((ksg, HEAD, LANES), F32), pltpu.VMEM((ks, HEAD, LANES), F32),
                        pltpu.VMEM((ks, HEAD, LANES), F32), pltpu.VMEM((ks, LANES), F32),
                        pltpu.VMEM((SUBLANES, LANES), F32)],
        compiler_params=_params(("parallel", "arbitrary"), 2 * blk + 5 * st_bytes + (4 << 20)),
    )(gk, rk, mk, vops, *states)


def _head_ln(x, ones_bd, eps):
    mean = _group_sum(x, ones_bd) * (1.0 / HEAD)
    xc = x - mean
    var = _group_sum(xc * xc, ones_bd) * (1.0 / HEAD)
    return xc * lax.rsqrt(var + eps)


def _gla_out(o, gate_r, ones_bd, gnorm):
    ms = _group_sum(o * o, ones_bd) * (1.0 / HEAD)
    return o * lax.rsqrt(ms + EPS) * gnorm * jax.nn.silu(gate_r)


def _rwkv_out(y, bonus, g, ones_bd, lnw, lnb):
    return (_head_ln(y, ones_bd, RWKV_LN_EPS) * lnw + lnb + bonus) * g


def _mlstm_out(h, gate_o, ones_bd, mnorm):
    return _head_ln(h, ones_bd, EPS) * mnorm * jax.nn.sigmoid(gate_o)


def _post_kernel(scan_ref, pg_ref, aux_ref, pm_ref, ones_ref, gnorm_ref, lnw_ref, lnb_ref, mnorm_ref,
                 og_ref, or_ref, om_ref):
    ones_bd = ones_ref[...]
    sc = scan_ref[...]
    aux = aux_ref[...]
    og_ref[...] = _gla_out(sc[:, 0:256], pg_ref[:, 512:768], ones_bd, gnorm_ref[...])
    or_ref[...] = _rwkv_out(sc[:, 256:512], aux[:, 256:512], aux[:, 0:256], ones_bd, lnw_ref[...], lnb_ref[...])
    om_ref[...] = _mlstm_out(sc[:, 512:768], pm_ref[:, 768:1024], ones_bd, mnorm_ref[...])


def _post(scan_rows, p_gla, aux, p_mlstm, wts, tm):
    n = scan_rows.shape[0]
    row = lambda w: pl.BlockSpec((tm, w), lambda i: (i, 0))
    const = lambda a: pl.BlockSpec(a.shape, lambda i: (0,) * a.ndim)
    widths = (768, W_GLA, 512, W_MLSTM)
    vmem = 2 * _nbytes((tm, sum(widths) + 768)) + 12 * _nbytes((tm, BRANCH)) + (4 << 20)
    return pl.pallas_call(
        _post_kernel,
        grid=(n // tm,),
        in_specs=[row(w) for w in widths] + [const(a) for a in wts],
        out_specs=[row(BRANCH)] * 3,
        out_shape=[jax.ShapeDtypeStruct((n, BRANCH), F32)] * 3,
        compiler_params=_params(("parallel",), vmem),
    )(scan_rows, p_gla, aux, p_mlstm, *wts)


def _merge_kernel(x_ref, og_ref, olat_ref, or_ref, om_ref, gate_ref, wuv_ref, wbr_ref, wout_ref, gpost_ref, o_ref):
    o_mla = _bdot(olat_ref[...], wuv_ref[...])
    mixed = jnp.zeros((x_ref.shape[0], D_MODEL), F32)
    for n, br in enumerate((og_ref[...], o_mla, or_ref[...], om_ref[...])):
        up = _bdot(br, wbr_ref[n])
        mixed = mixed + gate_ref[:, n * D_MODEL : (n + 1) * D_MODEL].astype(F32) * up
    out = _bdot(mixed, wout_ref[...])
    o_ref[...] = x_ref[...] + _rms(out, gpost_ref[...])


def _merge(x, o_gla, o_lat, o_rwkv, o_mlstm, p_gate, wts, tm):
    n = x.shape[0]
    row = lambda w: pl.BlockSpec((tm, w), lambda i: (i, 0))
    const = lambda a: pl.BlockSpec(a.shape, lambda i: (0,) * a.ndim)
    widths = (D_MODEL, BRANCH, 512, BRANCH, BRANCH, W_GATE)
    vmem = 2 * _nbytes((tm, sum(widths) + D_MODEL)) + 10 * _nbytes((tm, D_MODEL)) + (12 << 20)
    return pl.pallas_call(
        _merge_kernel,
        grid=(n // tm,),
        in_specs=[row(w) for w in widths] + [const(a) for a in wts],
        out_specs=row(D_MODEL),
        out_shape=jax.ShapeDtypeStruct((n, D_MODEL), F32),
        compiler_params=_params(("parallel",), vmem),
    )(x, o_gla, o_lat, o_rwkv, o_mlstm, p_gate, *wts)


GLA_CHUNK = 32
MIX_CHUNK = 64


def _bdot_tn(a, b):
    return lax.dot_general(a.astype(BF16), b.astype(BF16), (((0,), (0,)), ((), ())), preferred_element_type=F32)


def _cumsum_rows(tri, x):
    hi = x.astype(BF16)
    lo = (x - hi.astype(F32)).astype(BF16)
    return jnp.dot(tri, hi, preferred_element_type=F32) + jnp.dot(tri, lo, preferred_element_type=F32)


def _rep_rows(x, mask):
    return jnp.concatenate([x] * N_HEADS, axis=0) * mask


def _gla_chunk_kernel(pg_ref, gate2_ref, bgate_ref, gnorm_ref, ones_ref, tri_ref, causal_ref, kmask_ref, vmask_ref,
                      smask_ref, o_ref, st_ref, st, *, tc):
    i = pl.program_id(1)

    @pl.when(i == 0)
    def _():
        st[...] = jnp.zeros(st.shape, F32)

    L = GLA_CHUNK
    pg = pg_ref[...]
    log_a = _log_sigmoid(_bdot(pg[:, 768:896], gate2_ref[...]) + bgate_ref[...]) / GLA_TAU
    tri, causal = tri_ref[...], causal_ref[...]
    kmask, vmask, smask = kmask_ref[...], vmask_ref[...], smask_ref[...]
    rows = [slice(c * L, (c + 1) * L) for c in range(tc // L)]
    b = [_cumsum_rows(tri, log_a[r]) for r in rows]
    b_last = [x[L - 1 : L, :] for x in b]
    q_dec = [pg[r, 0:128] * (GLA_K ** -0.5) * jnp.exp(x) for r, x in zip(rows, b)]
    k_rep = [_rep_rows(pg[r, 128:256] * jnp.exp(-x), kmask) for r, x in zip(rows, b)]
    v_rep = [_rep_rows(pg[r, 256:512], vmask) for r in rows]
    att = [_bdot_t(qd, kr) * causal for qd, kr in zip(q_dec, k_rep)]
    o_intra = [_bdot(x, vr) for x, vr in zip(att, v_rep)]
    upd = [_bdot_tn(pg[r, 256:512], pg[r, 128:256] * jnp.exp(bl - x)) * smask for r, x, bl in zip(rows, b, b_last)]
    states = []
    s_t = st[...]
    for bl, ud in zip(b_last, upd):
        states.append(s_t)
        s_t = s_t * jnp.exp(bl) + ud
    st[...] = s_t
    o = jnp.concatenate([_bdot_t(qd, s) + oi for qd, s, oi in zip(q_dec, states, o_intra)], axis=0)
    o_ref[...] = _gla_out(o, pg[:, 512:768], ones_ref[...], gnorm_ref[...])

    @pl.when(i == pl.num_programs(1) - 1)
    def _():
        st_ref[0] = st[...]


def _gla_chunk(p_gla, wts, n_b, t_len):
    tc = min(256, t_len)
    nt = t_len // tc
    const = lambda a: pl.BlockSpec(a.shape, lambda b, i: (0,) * a.ndim)
    vmem = 2 * _nbytes((tc, W_GLA + BRANCH)) + 12 * _nbytes((tc, BRANCH)) + (6 << 20)
    return pl.pallas_call(
        functools.partial(_gla_chunk_kernel, tc=tc),
        grid=(n_b, nt),
        in_specs=[pl.BlockSpec((tc, W_GLA), lambda b, i: (b * nt + i, 0))] + [const(a) for a in wts],
        out_specs=[pl.BlockSpec((tc, BRANCH), lambda b, i: (b * nt + i, 0)),
                   pl.BlockSpec((1, BRANCH, N_HEADS * GLA_K), lambda b, i: (b, 0, 0))],
        out_shape=[jax.ShapeDtypeStruct((n_b * t_len, BRANCH), F32),
                   jax.ShapeDtypeStruct((n_b, BRANCH, N_HEADS * GLA_K), F32)],
        scratch_shapes=[pltpu.VMEM((BRANCH, N_HEADS * GLA_K), F32)],
        compiler_params=_params(("parallel", "arbitrary"), vmem),
    )(p_gla, *wts)


def _mlstm_chunk_kernel(pm_ref, ibias_ref, fbias_ref, mnorm_ref, ones_ref, tri_ref,
                        o_ref, c_out, n_out, m_out, cs, ns, ms, *, tc):
    i = pl.program_id(1)

    @pl.when(i == 0)
    def _():
        cs[...] = jnp.zeros(cs.shape, F32)
        ns[...] = jnp.zeros(ns.shape, F32)
        ms[...] = jnp.zeros(ms.shape, F32)

    L = MIX_CHUNK
    pm = pm_ref[...]
    ifp = pm[:, 1024:1152]
    lane = lax.broadcasted_iota(jnp.int32, ifp.shape, 1)
    gates = jnp.where(lane < N_HEADS, ifp + ibias_ref[...], _log_sigmoid(ifp + fbias_ref[...]))
    tri = tri_ref[...]
    causal = lax.broadcasted_iota(jnp.int32, (L, L), 1) <= lax.broadcasted_iota(jnp.int32, (L, L), 0)
    lane_c = lax.broadcasted_iota(jnp.int32, (L, LANES), 1)
    row_t = lax.broadcasted_iota(jnp.int32, (LANES, L), 0)
    last_lane = lax.broadcasted_iota(jnp.int32, (1, L), 1) == L - 1

    def pick_col(x, j):
        return jnp.sum(jnp.where(lane_c == j, x, 0.0), axis=1, keepdims=True)

    def pick_row(x_t, j):
        return jnp.sum(jnp.where(row_t == j, x_t, 0.0), axis=0, keepdims=True)

    n_ch = tc // L
    rows = [slice(c * L, (c + 1) * L) for c in range(n_ch)]
    units = [(c, h) for c in range(n_ch) for h in range(N_HEADS)]
    z = [jnp.where(lane_c < N_HEADS, gates[r], _cumsum_rows(tri, gates[r])) for r in rows]
    z_t = [x.T for x in z]
    i_col = [pick_col(z[c], h) for c, h in units]
    b_col = [pick_col(z[c], N_HEADS + h) for c, h in units]
    i_row = [pick_row(z_t[c], h) for c, h in units]
    b_row = [pick_row(z_t[c], N_HEADS + h) for c, h in units]
    b_last = [jnp.sum(jnp.where(last_lane, x, 0.0), axis=1, keepdims=True) for x in b_row]
    g_max = [jnp.max(bl - br + ir, axis=1, keepdims=True) for bl, br, ir in zip(b_last, b_row, i_row)]
    m0, m1 = [None] * len(units), [None] * len(units)
    for h in range(N_HEADS):
        m = jnp.max(ms[h : h + 1, :], axis=1, keepdims=True)
        for c in range(n_ch):
            u = c * N_HEADS + h
            m0[u] = m
            m = jnp.maximum(b_last[u] + m, g_max[u])
            m1[u] = m
        ms[h : h + 1, :] = jnp.broadcast_to(m, (1, LANES))
    q = [pm[rows[c], h * HEAD : (h + 1) * HEAD] * (HEAD ** -0.5) for c, h in units]
    k = [pm[rows[c], 256 + h * HEAD : 256 + (h + 1) * HEAD] for c, h in units]
    v = [pm[rows[c], 512 + h * HEAD : 512 + (h + 1) * HEAD] for c, h in units]
    d = [jnp.where(causal, bc - br + ir, -jnp.inf) for bc, br, ir in zip(b_col, b_row, i_row)]
    inter = [bc + m for bc, m in zip(b_col, m0)]
    m_t = [jnp.maximum(x, jnp.max(dd, axis=1, keepdims=True)) for x, dd in zip(inter, d)]
    w_inter = [jnp.exp(x - mt) for x, mt in zip(inter, m_t)]
    s_qk = [_bdot_t(a_, b_) for a_, b_ in zip(q, k)]
    w_intra = [jnp.exp(dd - mt) * s for dd, mt, s in zip(d, m_t, s_qk)]
    num_intra = [_bdot(w, x) for w, x in zip(w_intra, v)]
    den_intra = [jnp.sum(w, axis=1, keepdims=True) for w in w_intra]
    floor = [jnp.exp(-mt) for mt in m_t]
    scale_old = [jnp.exp(bl + a_ - b_) for bl, a_, b_ in zip(b_last, m0, m1)]
    kw = [x * jnp.exp(bl - bc + ic - m) for x, bl, bc, ic, m in zip(k, b_last, b_col, i_col, m1)]
    c_upd = [_bdot_tn(x, y) for x, y in zip(kw, v)]
    n_upd = [jnp.sum(x, axis=0, keepdims=True) for x in kw]
    chunks = []
    for c in range(n_ch):
        heads = []
        for h in range(N_HEADS):
            u = c * N_HEADS + h
            c_h, n_h = cs[h], ns[h : h + 1, :]
            num = w_inter[u] * _bdot(q[u], c_h) + num_intra[u]
            den = w_inter[u] * jnp.sum(q[u] * n_h, axis=1, keepdims=True) + den_intra[u]
            heads.append(num / jnp.maximum(jnp.abs(den), floor[u]))
            cs[h] = scale_old[u] * c_h + c_upd[u]
            ns[h : h + 1, :] = scale_old[u] * n_h + n_upd[u]
        chunks.append(jnp.concatenate(heads, axis=1))
    hm = jnp.concatenate(chunks, axis=0)
    o_ref[...] = _mlstm_out(hm, pm[:, 768:1024], ones_ref[...], mnorm_ref[...])

    @pl.when(i == pl.num_programs(1) - 1)
    def _():
        c_out[0] = cs[...]
        n_out[0] = ns[...]
        m_out[0] = ms[...]


def _mlstm_chunk(p_mlstm, wts, n_b, t_len):
    tc = min(512, t_len)
    nt = t_len // tc
    const = lambda a: pl.BlockSpec(a.shape, lambda b, i: (0,) * a.ndim)
    vmem = 2 * _nbytes((tc, W_MLSTM + BRANCH)) + 12 * _nbytes((tc, BRANCH)) + (6 << 20)
    return pl.pallas_call(
        functools.partial(_mlstm_chunk_kernel, tc=tc),
        grid=(n_b, nt),
        in_specs=[pl.BlockSpec((tc, W_MLSTM), lambda b, i: (b * nt + i, 0))] + [const(a) for a in wts],
        out_specs=[pl.BlockSpec((tc, BRANCH), lambda b, i: (b * nt + i, 0)),
                   pl.BlockSpec((1, N_HEADS, HEAD, HEAD), lambda b, i: (b, 0, 0, 0)),
                   pl.BlockSpec((1, SUBLANES, HEAD), lambda b, i: (b, 0, 0)),
                   pl.BlockSpec((1, SUBLANES, LANES), lambda b, i: (b, 0, 0))],
        out_shape=[jax.ShapeDtypeStruct((n_b * t_len, BRANCH), F32),
                   jax.ShapeDtypeStruct((n_b, N_HEADS, HEAD, HEAD), F32),
                   jax.ShapeDtypeStruct((n_b, SUBLANES, HEAD), F32),
                   jax.ShapeDtypeStruct((n_b, SUBLANES, LANES), F32)],
        scratch_shapes=[pltpu.VMEM((N_HEADS, HEAD, HEAD), F32), pltpu.VMEM((SUBLANES, HEAD), F32),
                        pltpu.VMEM((SUBLANES, LANES), F32)],
        compiler_params=_params(("parallel", "arbitrary"), vmem),
    )(p_mlstm, *wts)


def _rwkv_mix(pr, prev, mu, w0, w2, a0, a2, g2, kkw, ka, rkw, ones_bd):
    pm = pr + (prev - pr) * mu
    rr, rk, rv = pm[:, 0:256], pm[:, 256:512], pm[:, 512:768]
    wlo, alo, rglo = pm[:, 768:832], pm[:, 832:896], pm[:, 896:1024]
    log_w = -jnp.exp(-_softplus(-(w0 + _bdot(jnp.tanh(wlo), w2))) - 0.5)
    a = jax.nn.sigmoid(a0 + _bdot(alo, a2))
    g = _bdot(jax.nn.sigmoid(rglo), g2)
    kkr = rk * kkw
    kk = kkr / jnp.maximum(jnp.sqrt(_group_sum(kkr * kkr, ones_bd)), 1e-12)
    k_mod = rk * (1.0 + (a - 1.0) * ka)
    bonus = _group_sum(rr * k_mod * rkw, ones_bd) * rv
    return log_w, kk, a, k_mod, rr, rv, g, bonus


def _rwkv_chunk_kernel(pr_ref, mu_ref, w0_ref, w2_ref, a0_ref, a2_ref, g2_ref, kk_ref, ka_ref, rk_ref, ones_ref,
                       lnw_ref, lnb_ref, tri_ref, strict_ref, incl_ref, o_ref, s_out, carry, sv, *, tc):
    i = pl.program_id(1)

    @pl.when(i == 0)
    def _():
        carry[...] = jnp.zeros(carry.shape, F32)
        sv[...] = jnp.zeros(sv.shape, F32)

    L = MIX_CHUNK
    pr = pr_ref[...]
    prev = _shift_rows(pr, carry[...], 1)
    carry[...] = pr[tc - 1 :, :]
    ones_bd = ones_ref[...]
    log_w, kk, a, k_mod, rr, rv, g, bonus = _rwkv_mix(
        pr, prev, mu_ref[...], w0_ref[...], w2_ref[...], a0_ref[...], a2_ref[...], g2_ref[...], kk_ref[...],
        ka_ref[...], rk_ref[...], ones_bd)
    bd = ones_bd.astype(F32)
    tri, strict, incl = tri_ref[...], strict_ref[...], incl_ref[...]
    chunks = range(tc // L)
    rows = [slice(c * L, (c + 1) * L) for c in chunks]
    cum = [_cumsum_rows(tri, log_w[r]) for r in rows]
    w_end = [jnp.exp(cm[L - 1 : L, :]) for cm in cum]
    v_c = [rv[r] for r in rows]
    v_rep = [_rep_rows(v, bd) for v in v_c]
    al, rt, k_end, gm = [], [], [], []
    for c, r in zip(chunks, rows):
        end = jnp.exp(cum[c][L - 1 : L, :] - cum[c])
        w_inv = jnp.exp(-cum[c])
        ka_c = kk[r] * a[r]
        al.append(-kk[r] * jnp.exp(cum[c] - log_w[r]))
        rt.append(rr[r] * jnp.exp(cum[c]))
        k_end.append(jnp.concatenate([ka_c * end, k_mod[r] * end], axis=0))
        gm.append(_bdot_t(jnp.concatenate([al[c], rt[c]], axis=0),
                          jnp.concatenate([_rep_rows(ka_c * w_inv, bd), _rep_rows(k_mod[r] * w_inv, bd)], axis=0)))
    a_ab = [jnp.where(strict > 0, g_[0:L, 0:256], 0.0) for g_ in gm]
    a_ak = [jnp.where(strict > 0, g_[0:L, 256:512], 0.0) for g_ in gm]
    r_b = [jnp.where(incl > 0, g_[L : 2 * L, 0:256], 0.0) for g_ in gm]
    r_k = [jnp.where(incl > 0, g_[L : 2 * L, 256:512], 0.0) for g_ in gm]
    t_cols = [(incl - strict) + x for x in a_ab]
    p_cols = [_bdot(x, _rep_rows(x, bd)) for x in a_ab]
    for _ in range(4):
        prod = [_bdot(jnp.concatenate([t, p], axis=0), _rep_rows(p, bd)) for t, p in zip(t_cols, p_cols)]
        t_cols = [t + pr_[0:L] for t, pr_ in zip(t_cols, prod)]
        p_cols = [pr_[L : 2 * L] for pr_ in prod]
    t_cols = [t + _bdot(t, _rep_rows(p, bd)) for t, p in zip(t_cols, p_cols)]
    al2 = [_bdot(t, _rep_rows(x, bd)) for t, x in zip(t_cols, al)]
    u_fix = [_bdot(t, _rep_rows(_bdot(x, v), bd)) for t, x, v in zip(t_cols, a_ak, v_rep)]
    y_fix = [_bdot(x, v) for x, v in zip(r_k, v_rep)]
    outs = []
    for c in chunks:
        s_vk = sv[...]
        u = _bdot_t(al2[c], s_vk) + u_fix[c]
        outs.append(_bdot_t(rt[c], s_vk) + _bdot(r_b[c], _rep_rows(u, bd)) + y_fix[c])
        sv[...] = s_vk * w_end[c] + _bdot_tn(jnp.concatenate([u, v_c[c]], axis=0), k_end[c]) * bd
    y = jnp.concatenate(outs, axis=0)
    o_ref[...] = _rwkv_out(y, bonus, g, ones_bd, lnw_ref[...], lnb_ref[...])

    @pl.when(i == pl.num_programs(1) - 1)
    def _():
        s_out[0] = sv[...]


def _rwkv_chunk(p_rwkv, wts, n_b, t_len):
    tc = min(512, t_len)
    nt = t_len // tc
    const = lambda a: pl.BlockSpec(a.shape, lambda b, i: (0,) * a.ndim)
    vmem = 2 * _nbytes((tc, W_RWKV + BRANCH)) + 24 * _nbytes((tc, BRANCH)) + 16 * _nbytes((BRANCH, BRANCH)) + (6 << 20)
    return pl.pallas_call(
        functools.partial(_rwkv_chunk_kernel, tc=tc),
        grid=(n_b, nt),
        in_specs=[pl.BlockSpec((tc, W_RWKV), lambda b, i: (b * nt + i, 0))] + [const(a) for a in wts],
        out_specs=[pl.BlockSpec((tc, BRANCH), lambda b, i: (b * nt + i, 0)),
                   pl.BlockSpec((1, BRANCH, BRANCH), lambda b, i: (b, 0, 0))],
        out_shape=[jax.ShapeDtypeStruct((n_b * t_len, BRANCH), F32), jax.ShapeDtypeStruct((n_b, BRANCH, BRANCH), F32)],
        scratch_shapes=[pltpu.VMEM((1, W_RWKV), F32), pltpu.VMEM((BRANCH, BRANCH), F32)],
        compiler_params=_params(("parallel", "arbitrary"), vmem),
    )(p_rwkv, *wts)


def _norm_matmul_kernel(x_ref, g_ref, w_ref, *outs):
    h = _rms(x_ref[...], g_ref[...]).astype(BF16)
    off = 0
    for o_ref in outs:
        width = o_ref.shape[1]
        o_ref[...] = jnp.dot(h, w_ref[:, off : off + width], preferred_element_type=F32)
        off += width


def _norm_matmul(x, g, w, widths, tm):
    n = x.shape[0]
    total = sum(widths)
    vmem = 2 * _nbytes((tm, D_MODEL + total)) + 2 * _nbytes((D_MODEL, total), BF16) + (6 << 20)
    return pl.pallas_call(
        _norm_matmul_kernel,
        grid=(n // tm,),
        in_specs=[pl.BlockSpec((tm, D_MODEL), lambda i: (i, 0)), pl.BlockSpec((1, D_MODEL), lambda i: (0, 0)),
                  pl.BlockSpec((D_MODEL, total), lambda i: (0, 0))],
        out_specs=[pl.BlockSpec((tm, w_), lambda i: (i, 0)) for w_ in widths],
        out_shape=[jax.ShapeDtypeStruct((n, w_), F32) for w_ in widths],
        compiler_params=_params(("parallel",), vmem),
    )(x, g, w)


def _proj_res_kernel(x_ref, a_ref, w_ref, g_ref, o_ref):
    o_ref[...] = x_ref[...] + _rms(_bdot(a_ref[...], w_ref[...]), g_ref[...])


def _proj_res(x, a, w, g, tm):
    n = x.shape[0]
    row = pl.BlockSpec((tm, D_MODEL), lambda i: (i, 0))
    vmem = 6 * _nbytes((tm, D_MODEL)) + 2 * _nbytes((D_MODEL, D_MODEL), BF16) + (6 << 20)
    return pl.pallas_call(
        _proj_res_kernel,
        grid=(n // tm,),
        in_specs=[row, row, pl.BlockSpec((D_MODEL, D_MODEL), lambda i: (0, 0)), pl.BlockSpec((1, D_MODEL), lambda i: (0, 0))],
        out_specs=row,
        out_shape=jax.ShapeDtypeStruct((n, D_MODEL), F32),
        compiler_params=_params(("parallel",), vmem),
    )(x, a, w, g)


def _cross_attn_kernel(q_ref, k_ref, v_ref, o_ref):
    q = q_ref[...]
    outs = []
    for h in range(N_HEADS):
        cols = slice(h * CA_DIM, (h + 1) * CA_DIM)
        s = _bdot_t(q[:, cols], k_ref[:, cols]) * (CA_DIM ** -0.5)
        s = s - jnp.max(s, axis=1, keepdims=True)
        p = jnp.exp(s)
        p = p / jnp.sum(p, axis=1, keepdims=True)
        outs.append(_bdot(p, v_ref[:, cols]))
    o_ref[...] = jnp.concatenate(outs, axis=1)


def _cross_attn(q, mem_k, mem_v, layer, tq):
    n_b, t_len, _ = q.shape
    mem = mem_k.shape[2]
    qspec = pl.BlockSpec((None, tq, D_MODEL), lambda b, i: (b, i, 0))
    mspec = pl.BlockSpec((None, None, mem, D_MODEL), lambda b, i: (layer, b, 0, 0))
    vmem = 4 * _nbytes((tq, D_MODEL)) + 4 * _nbytes((mem, D_MODEL)) + 8 * _nbytes((tq, D_MODEL)) + (6 << 20)
    return pl.pallas_call(
        _cross_attn_kernel,
        grid=(n_b, t_len // tq),
        in_specs=[qspec, mspec, mspec],
        out_specs=qspec,
        out_shape=jax.ShapeDtypeStruct(q.shape, F32),
        compiler_params=_params(("parallel", "parallel"), vmem),
    )(q, mem_k, mem_v)


def _gelu(x):
    return 0.5 * x * (1.0 + lax.erf(x * (2.0 ** -0.5)))


def _ffn_kernel(x_ref, gpre_ref, wa_ref, wg_ref, cw_ref, cb_ref, wd_ref, st_ref, gpost_ref,
                o_ref, cnew_ref, hn, acc, carry, *, stride):
    i = pl.program_id(1)
    c = pl.program_id(2)
    tm = x_ref.shape[0]

    @pl.when(c == 0)
    def _():
        hn[...] = _rms(x_ref[...], gpre_ref[...]).astype(BF16)
        acc[...] = jnp.zeros(acc.shape, F32)

    @pl.when(i == 0)
    def _():
        carry[c] = st_ref[0]

    h = hn[...]
    a = jnp.dot(h, wa_ref[...], preferred_element_type=F32)
    gate = jnp.dot(h, wg_ref[...], preferred_element_type=F32)
    st = carry[c]
    prev1 = _shift_rows(a, st[stride:, :], stride)
    prev2 = _shift_rows(a, st, 2 * stride)
    cw = cw_ref[...]
    conv = cb_ref[...] + prev2 * cw[0:1, :] + prev1 * cw[1:2, :] + a * cw[2:3, :]
    tail = a[tm - 2 * stride :, :]
    carry[c] = tail
    fc = a.shape[1]
    for cc in range(FF_CHUNKS):
        @pl.when((c == cc) & (i == pl.num_programs(1) - 1))
        def _(cc=cc):
            cnew_ref[0, :, cc * fc : (cc + 1) * fc] = tail
    acc[...] += _bdot(_gelu(conv) * gate, wd_ref[...])

    @pl.when(c == pl.num_programs(2) - 1)
    def _():
        o_ref[...] = x_ref[...] + _rms(acc[...], gpost_ref[...])


def _ffn(x, gpre, wa, wg, cw, cb, wd, state, gpost, n_seq, tm, stride):
    n = x.shape[0]
    n_tiles = n // n_seq // tm
    fc = D_FF // FF_CHUNKS
    row = pl.BlockSpec((tm, D_MODEL), lambda q, i, c: (q * n_tiles + i, 0))
    vec = pl.BlockSpec((1, D_MODEL), lambda q, i, c: (0, 0))
    st = pl.BlockSpec((1, 2 * stride, fc), lambda q, i, c: (q, 0, c))
    vmem = (4 * _nbytes((tm, D_MODEL)) + 4 * _nbytes((D_MODEL, fc), BF16) + 2 * _nbytes((fc, D_MODEL), BF16)
            + 2 * _nbytes((tm, D_MODEL)) + 8 * _nbytes((tm, fc)) + 6 * _nbytes((2 * stride, fc)) + (4 << 20))
    return pl.pallas_call(
        functools.partial(_ffn_kernel, stride=stride),
        grid=(n_seq, n_tiles, FF_CHUNKS),
        in_specs=[row, vec,
                  pl.BlockSpec((D_MODEL, fc), lambda q, i, c: (0, c)),
                  pl.BlockSpec((D_MODEL, fc), lambda q, i, c: (0, c)),
                  pl.BlockSpec((3, fc), lambda q, i, c: (0, c)),
                  pl.BlockSpec((1, fc), lambda q, i, c: (0, c)),
                  pl.BlockSpec((fc, D_MODEL), lambda q, i, c: (c, 0)),
                  st, vec],
        out_specs=[row, pl.BlockSpec((1, 2 * stride, D_FF), lambda q, i, c: (q, 0, 0))],
        out_shape=[jax.ShapeDtypeStruct((n, D_MODEL), F32), jax.ShapeDtypeStruct((n_seq, 2 * stride, D_FF), F32)],
        scratch_shapes=[pltpu.VMEM((tm, D_MODEL), BF16), pltpu.VMEM((tm, D_MODEL), F32),
                        pltpu.VMEM((FF_CHUNKS, 2 * stride, fc), F32)],
        compiler_params=_params(("arbitrary", "arbitrary", "arbitrary"), vmem),
    )(x, gpre, wa, wg, cw, cb, wd, state, gpost)


def _pad_cols(w, width):
    return jnp.pad(w, ((0, 0), (0, width - w.shape[1])))


def _row(v):
    return v.reshape(1, -1).astype(F32)


def _block_diag(blocks):
    rows = sum(b.shape[0] for b in blocks)
    cols = sum(b.shape[1] for b in blocks)
    out = jnp.zeros((rows, cols), blocks[0].dtype)
    r = c = 0
    for b in blocks:
        out = lax.dynamic_update_slice(out, b, (r, c))
        r += b.shape[0]
        c += b.shape[1]
    return out


def _layer_weights(p, l):
    w = p["w_in"][l]
    gla, mla, rw, ml = w[:, 0:784], w[:, 784:1200], w[:, 1200:2224], w[:, 2224:3256]
    gate = w[:, 3256:]
    w_gla = jnp.concatenate([gla[:, 0:512], gla[:, 528:784], _pad_cols(gla[:, 512:528], LANES)], axis=1)
    w_mla = _pad_cols(mla, W_MLA)
    w_ml = jnp.concatenate([ml[:, 0:768], ml[:, 776:1032], _pad_cols(ml[:, 768:776], LANES)], axis=1)
    w_all = jnp.concatenate([w_gla, w_mla, rw, w_ml, gate], axis=1).astype(BF16)

    ones_bd = _block_diag([jnp.ones((HEAD, HEAD), BF16)] * N_HEADS)
    ibias = jnp.zeros((LANES,), F32).at[0:N_HEADS].set(p["mlstm_i_bias"][l])
    fbias = jnp.zeros((LANES,), F32).at[N_HEADS : 2 * N_HEADS].set(p["mlstm_f_bias"][l])
    prep = [
        jnp.pad(p["gla_w_gate2"][l], ((0, LANES - 16), (0, 0))).astype(BF16), _row(p["gla_b_gate"][l]),
        _row(p["rwkv_mu"][l]), _row(p["rwkv_w0"][l]), p["rwkv_w2"][l].astype(BF16), _row(p["rwkv_a0"][l]),
        p["rwkv_a2"][l].astype(BF16), p["rwkv_g2"][l].astype(BF16), _row(p["rwkv_k_k"][l]), _row(p["rwkv_k_a"][l]),
        _row(p["rwkv_r_k"][l]), ones_bd, _row(ibias), _row(fbias),
    ]
    wuq = p["mla_w_uq"][l].reshape(256, N_HEADS, MLA_NOPE + MLA_ROPE)
    mla_w = [
        _row(p["mla_norm_q"][l]), _row(p["mla_norm_kv"][l]),
        wuq[:, :, :MLA_NOPE].reshape(256, N_HEADS * MLA_NOPE).astype(BF16),
        wuq[:, :, MLA_NOPE:].reshape(256, N_HEADS * MLA_ROPE).astype(BF16),
        _block_diag([p["mla_w_uk"][l][h].T for h in range(N_HEADS)]).astype(BF16),
    ]
    gnorm = _row(jnp.tile(p["gla_norm"][l], N_HEADS))
    post = [ones_bd, gnorm, _row(p["rwkv_ln_w"][l]), _row(p["rwkv_ln_b"][l]), _row(p["mlstm_norm"][l])]
    merge = [
        _block_diag([p["mla_w_uv"][l][h] for h in range(N_HEADS)]).astype(BF16),
        p["w_branch"][l].astype(BF16), p["w_out"][l].astype(BF16), _row(p["norm_post_mix"][l]),
    ]
    tri = lambda n: jnp.tril(jnp.ones((n, n), F32))
    blocks = lambda r, c: _block_diag([jnp.ones((r, c), F32)] * N_HEADS)
    gla_c = prep[0:2] + [gnorm, ones_bd, tri(GLA_CHUNK).astype(BF16), jnp.tile(tri(GLA_CHUNK), (1, N_HEADS)),
                         blocks(GLA_CHUNK, GLA_K), blocks(GLA_CHUNK, HEAD), blocks(HEAD, GLA_K)]
    mlstm_c = [_row(ibias), _row(fbias), _row(p["mlstm_norm"][l]), ones_bd, tri(MIX_CHUNK).astype(BF16)]
    rwkv_c = prep[2:12] + [_row(p["rwkv_ln_w"][l]), _row(p["rwkv_ln_b"][l]), tri(MIX_CHUNK).astype(BF16),
                           jnp.tile(jnp.tril(jnp.ones((MIX_CHUNK, MIX_CHUNK), F32), -1), (1, N_HEADS)),
                           jnp.tile(tri(MIX_CHUNK), (1, N_HEADS))]
    up = p["ffn_w_up"][l]
    return dict(
        w_all=w_all, prep=prep, mla=mla_w, post=post, merge=merge, gla_c=gla_c, mlstm_c=mlstm_c, rwkv_c=rwkv_c,
        g_pre_mix=_row(p["norm_pre_mix"][l]),
        g_pre_ca=_row(p["norm_pre_ca"][l]), wq=p["ca_w_q"][l].astype(BF16), wo=p["ca_w_o"][l].astype(BF16),
        g_post_ca=_row(p["norm_post_ca"][l]),
        g_mem=_row(p["ca_norm_mem"][l]),
        wkv=jnp.concatenate([p["ca_w_k"][l], p["ca_w_v"][l]], axis=1).astype(BF16),
        g_pre_ffn=_row(p["norm_pre_ffn"][l]), wa=up[:, :D_FF].astype(BF16), wg=up[:, D_FF:].astype(BF16),
        cw=p["ffn_conv_w"][l].astype(F32), cb=_row(p["ffn_conv_b"][l]), wd=p["ffn_w_down"][l].astype(BF16),
        g_post_ffn=_row(p["norm_post_ffn"][l]),
    )


def _rope_tables(pos):
    half = MLA_ROPE // 2
    inv_freq = ROPE_THETA ** (-jnp.arange(half, dtype=F32) / half)
    ang = pos[:, None] * inv_freq[None, :]
    cos, sin = jnp.cos(ang), jnp.sin(ang)
    return jnp.concatenate([cos, cos], axis=1), jnp.concatenate([-sin, sin], axis=1)


class _Group:
    def __init__(self, n_b, t_len, time_major):
        self.n_b, self.t_len, self.time_major = n_b, t_len, time_major
        self.n_seq = 1 if time_major else n_b
        self.stride = n_b if time_major else 1
        self.rows = n_b * t_len
        self.tm = min(512, self.rows // self.n_seq)
        pairs = n_b * N_HEADS
        self.groups = max(1, LANES // pairs)
        assert not time_major or (self.groups in (1, 2) and (pairs * self.groups) % LANES == 0)

    def to_btc(self, x):
        c = x.shape[-1]
        if self.time_major:
            return x.reshape(self.t_len, self.n_b, c).transpose(1, 0, 2)
        return x.reshape(self.n_b, self.t_len, c)

    def from_btc(self, x):
        c = x.shape[-1]
        if self.time_major:
            return x.transpose(1, 0, 2).reshape(self.rows, c)
        return x.reshape(self.rows, c)

    def key_operand(self, x, dim):
        g = self.groups
        x = self.to_btc(x).reshape(self.n_b, self.t_len, N_HEADS, g, dim // g)
        return x.transpose(1, 4, 3, 0, 2).reshape(self.t_len, dim // g, g * self.n_b * N_HEADS)

    def val_operand(self, x, dim):
        x = self.to_btc(x).reshape(self.n_b, self.t_len, N_HEADS, dim)
        x = x.transpose(1, 3, 0, 2).reshape(self.t_len, dim, self.n_b * N_HEADS)
        return jnp.concatenate([x] * self.groups, axis=2)

    def scan_to_rows(self, o):
        pairs = self.n_b * N_HEADS
        o = o[:, :, :pairs].reshape(self.t_len, 3, HEAD, self.n_b, N_HEADS)
        return self.from_btc(o.transpose(3, 0, 1, 4, 2).reshape(self.n_b, self.t_len, 3 * BRANCH))

    def state_kv_in(self, s, key_axis_last):
        g = self.groups
        if key_axis_last:
            s = jnp.swapaxes(s, 2, 3)
        k, v = s.shape[2], s.shape[3]
        s = s.reshape(self.n_b, N_HEADS, g, k // g, v)
        return s.transpose(3, 4, 2, 0, 1).reshape(k // g, v, g * self.n_b * N_HEADS)

    def state_kv_out(self, s, key_axis_last):
        g = self.groups
        kg, v, _ = s.shape
        s = s.reshape(kg, v, g, self.n_b, N_HEADS).transpose(3, 4, 2, 0, 1).reshape(self.n_b, N_HEADS, g * kg, v)
        return jnp.swapaxes(s, 2, 3) if key_axis_last else s

    def state_k_in(self, s):
        g = self.groups
        k = s.shape[2]
        s = s.reshape(self.n_b, N_HEADS, g, k // g)
        return s.transpose(3, 2, 0, 1).reshape(k // g, g * self.n_b * N_HEADS)

    def state_k_out(self, s):
        g = self.groups
        kg = s.shape[0]
        return s.reshape(kg, g, self.n_b, N_HEADS).transpose(2, 3, 1, 0).reshape(self.n_b, N_HEADS, g * kg)

    def state_m_in(self, s):
        row = jnp.concatenate([s.reshape(1, self.n_b * N_HEADS)] * self.groups, axis=1)
        return jnp.concatenate([row, jnp.zeros((SUBLANES - 1, row.shape[1]), F32)], axis=0)

    def state_m_out(self, s):
        return s[0, : self.n_b * N_HEADS].reshape(self.n_b, N_HEADS)


def _diag_blocks(s, rows, cols):
    n_b = s.shape[0]
    s = s.reshape(n_b, N_HEADS, rows, N_HEADS, cols)
    return jnp.stack([s[:, h, :, h, :] for h in range(N_HEADS)], axis=1)


def _mixer(grp, x, lw, states, pos0, mla_attend):
    p_gla, p_mla, p_rwkv, p_mlstm, p_gate = _mix_in(x, lw["g_pre_mix"], lw["w_all"], min(256, grp.tm))

    pos = pos0 + jnp.arange(grp.t_len, dtype=F32)
    cos2, sin2 = _rope_tables(pos)
    if grp.time_major:
        cos2, sin2 = jnp.repeat(cos2, grp.n_b, axis=0), jnp.repeat(sin2, grp.n_b, axis=0)
    q_lat, q_pe, ckv, kpe = _mla_prep(p_mla, cos2, sin2, lw["mla"], grp.n_seq, grp.tm)
    o_lat = mla_attend(q_lat, q_pe, ckv, kpe)

    recur = _recurrent_scan if grp.time_major else _recurrent_chunked
    o_gla, o_rwkv, o_mlstm, rec_state = recur(grp, p_gla, p_rwkv, p_mlstm, lw, states)
    x_new = _merge(x, o_gla, o_lat, o_rwkv, o_mlstm, p_gate, lw["merge"], min(256, grp.tm))
    gla_new, rwkv_new, mc_new, mn_new, mm_new = rec_state
    shift_new = grp.to_btc(p_rwkv)[:, -1]
    return x_new, (ckv, kpe, gla_new, rwkv_new, shift_new, mc_new, mn_new, mm_new)


def _recurrent_chunked(grp, p_gla, p_rwkv, p_mlstm, lw, states):
    del states
    o_gla, s_gla = _gla_chunk(p_gla, lw["gla_c"], grp.n_b, grp.t_len)
    o_ml, c_new, n_new, m_new = _mlstm_chunk(p_mlstm, lw["mlstm_c"], grp.n_b, grp.t_len)
    o_rw, s_rw = _rwkv_chunk(p_rwkv, lw["rwkv_c"], grp.n_b, grp.t_len)
    gla_new = jnp.swapaxes(_diag_blocks(s_gla, HEAD, GLA_K), 2, 3)
    rwkv_new = _diag_blocks(s_rw, HEAD, HEAD)
    return o_gla, o_rw, o_ml, (gla_new, rwkv_new, c_new, n_new[:, :N_HEADS, :], m_new[:, :N_HEADS, 0])


def _recurrent_scan(grp, p_gla, p_rwkv, p_mlstm, lw, states):
    gla_s, rwkv_s, shift_s, mc_s, mn_s, mm_s = states
    shift0 = shift_s.reshape(grp.n_seq, grp.stride, W_RWKV)
    gla_ops, rw_ops, aux, ml_ops = _prep(p_gla, p_rwkv, p_mlstm, shift0, lw["prep"], grp.n_seq, grp.tm, grp.stride)
    g = grp.groups
    gk = jnp.concatenate([grp.key_operand(gla_ops[:, 0:128], GLA_K), grp.key_operand(p_gla[:, 128:256], GLA_K),
                          grp.key_operand(gla_ops[:, 128:256], GLA_K)], axis=1)
    rk = jnp.concatenate([grp.key_operand(rw_ops[:, c * 256 : (c + 1) * 256], HEAD) for c in range(5)], axis=1)
    mk = jnp.concatenate([grp.key_operand(ml_ops[:, 0:256], HEAD), grp.key_operand(p_mlstm[:, 256:512], HEAD)], axis=1)
    gates = grp.to_btc(ml_ops[:, 256 : 256 + 2 * N_HEADS]).reshape(grp.n_b, grp.t_len, 2, N_HEADS)
    gates = gates.transpose(1, 2, 0, 3).reshape(grp.t_len, 2, grp.n_b * N_HEADS)
    gates = jnp.concatenate([gates] * g, axis=2)
    vops = jnp.concatenate([grp.val_operand(p_gla[:, 256:512], HEAD), grp.val_operand(rw_ops[:, 1280:1536], HEAD),
                            grp.val_operand(p_mlstm[:, 512:768], HEAD), gates,
                            jnp.zeros((grp.t_len, SUBLANES - 2, gates.shape[2]), F32)], axis=1)
    st_in = [grp.state_kv_in(gla_s, False), grp.state_kv_in(rwkv_s, True), grp.state_kv_in(mc_s, False),
             grp.state_k_in(mn_s), grp.state_m_in(mm_s)]
    o_scan, sg, sr, sc, sn, sm = _scan(gk, rk, mk, vops, st_in, grp.t_len, g)
    scan_rows = grp.scan_to_rows(o_scan)
    o_gla, o_rwkv, o_mlstm = _post(scan_rows, p_gla, aux, p_mlstm, lw["post"], min(256, grp.tm))
    return o_gla, o_rwkv, o_mlstm, (grp.state_kv_out(sg, False), grp.state_kv_out(sr, True),
                                    grp.state_kv_out(sc, False), grp.state_k_out(sn), grp.state_m_out(sm))


def _cross_and_ffn(grp, x, lw, mem_k, mem_v, layer, conv_state):
    (q,) = _norm_matmul(x, lw["g_pre_ca"], lw["wq"], (D_MODEL,), grp.tm)
    q = grp.to_btc(q)
    ca = _cross_attn(q, mem_k, mem_v, layer, min(512, grp.t_len))
    x = _proj_res(x, grp.from_btc(ca), lw["wo"], lw["g_post_ca"], grp.tm)
    x, conv_new = _ffn(x, lw["g_pre_ffn"], lw["wa"], lw["wg"], lw["cw"], lw["cb"], lw["wd"], conv_state,
                       lw["g_post_ffn"], grp.n_seq, max(min(256, grp.tm), 4 * grp.stride), grp.stride)
    return x, conv_new


def kernel(x_prompt, x_sample, mem_prompt, cache_ckv, cache_kpe, page_table, cache_mem_k, cache_mem_v, state_gla, state_rwkv, state_rwkv_shift, state_mlstm_c, state_mlstm_n, state_mlstm_m, state_conv, norm_pre_mix, norm_post_mix, norm_pre_ca, norm_post_ca, norm_pre_ffn, norm_post_ffn, w_in, gla_w_gate2, gla_b_gate, gla_norm, mla_norm_q, mla_norm_kv, mla_w_uq, mla_w_uk, mla_w_uv, rwkv_mu, rwkv_w0, rwkv_w2, rwkv_a0, rwkv_a2, rwkv_g2, rwkv_k_k, rwkv_k_a, rwkv_r_k, rwkv_ln_w, rwkv_ln_b, mlstm_i_bias, mlstm_f_bias, mlstm_norm, w_branch, w_out, ca_norm_mem, ca_w_q, ca_w_k, ca_w_v, ca_w_o, ffn_w_up, ffn_conv_w, ffn_conv_b, ffn_w_down):
    p = dict(norm_pre_mix=norm_pre_mix, norm_post_mix=norm_post_mix, norm_pre_ca=norm_pre_ca, norm_post_ca=norm_post_ca,
             norm_pre_ffn=norm_pre_ffn, norm_post_ffn=norm_post_ffn, w_in=w_in, gla_w_gate2=gla_w_gate2,
             gla_b_gate=gla_b_gate, gla_norm=gla_norm, mla_norm_q=mla_norm_q, mla_norm_kv=mla_norm_kv,
             mla_w_uq=mla_w_uq, mla_w_uk=mla_w_uk, mla_w_uv=mla_w_uv, rwkv_mu=rwkv_mu, rwkv_w0=rwkv_w0,
             rwkv_w2=rwkv_w2, rwkv_a0=rwkv_a0, rwkv_a2=rwkv_a2, rwkv_g2=rwkv_g2, rwkv_k_k=rwkv_k_k,
             rwkv_k_a=rwkv_k_a, rwkv_r_k=rwkv_r_k, rwkv_ln_w=rwkv_ln_w, rwkv_ln_b=rwkv_ln_b,
             mlstm_i_bias=mlstm_i_bias, mlstm_f_bias=mlstm_f_bias, mlstm_norm=mlstm_norm, w_branch=w_branch,
             w_out=w_out, ca_norm_mem=ca_norm_mem, ca_w_q=ca_w_q, ca_w_k=ca_w_k, ca_w_v=ca_w_v, ca_w_o=ca_w_o,
             ffn_w_up=ffn_w_up, ffn_conv_w=ffn_conv_w, ffn_conv_b=ffn_conv_b, ffn_w_down=ffn_w_down)
    depth = w_in.shape[0]
    b_p, t_p, _ = x_prompt.shape
    b_s, t_s, _ = x_sample.shape
    n_pages, page = page_table.shape[1], cache_ckv.shape[2]
    past_len = n_pages * page
    mem_len = mem_prompt.shape[1]
    gp = _Group(b_p, t_p, time_major=False)
    gs = _Group(b_s, t_s, time_major=True)

    xp = gp.from_btc(x_prompt)
    xs = gs.from_btc(x_sample)
    mem_rows = mem_prompt.reshape(b_p * mem_len, D_MODEL)
    mem_k_c = cache_mem_k.reshape(depth, b_s, mem_len, D_MODEL)
    mem_v_c = cache_mem_v.reshape(depth, b_s, mem_len, D_MODEL)
    cache_kpe_t = jnp.swapaxes(cache_kpe, 2, 3)

    conv0_p = jnp.zeros((b_p, 2, D_FF), F32)

    new_p, new_s, mem_k_new, mem_v_new = [], [], [], []
    for l in range(depth):
        lw = _layer_weights(p, l)

        mk, mv = _norm_matmul(mem_rows, lw["g_mem"], lw["wkv"], (D_MODEL, D_MODEL), min(512, mem_rows.shape[0]))
        attend_p = lambda q_lat, q_pe, ckv, kpe: _mla_prompt(q_lat, q_pe, ckv, kpe, b_p, t_p)
        xp, st = _mixer(gp, xp, lw, None, 0.0, attend_p)
        xp, conv_new = _cross_and_ffn(gp, xp, lw, mk.reshape(1, b_p, mem_len, D_MODEL),
                                      mv.reshape(1, b_p, mem_len, D_MODEL), 0, conv0_p)
        new_p.append(st + (conv_new,))
        mem_k_new.append(mk.reshape(b_p, mem_len, N_HEADS, CA_DIM))
        mem_v_new.append(mv.reshape(b_p, mem_len, N_HEADS, CA_DIM))

        def attend_s(q_lat, q_pe, ckv, kpe, l=l):
            rows = t_s * N_HEADS
            ql = gs.to_btc(q_lat).reshape(b_s, rows, MLA_LORA)
            qp = gs.to_btc(q_pe).reshape(b_s, rows, MLA_ROPE)
            cn = jnp.pad(gs.to_btc(ckv), ((0, 0), (0, page - t_s), (0, 0)))
            pn = jnp.pad(gs.to_btc(kpe), ((0, 0), (0, page - t_s), (0, 0)))
            o = _mla_sample(page_table, ql, qp, cn, pn, cache_ckv, cache_kpe_t, l)
            return gs.from_btc(o.reshape(b_s, t_s, N_HEADS * MLA_LORA))

        st_s = (state_gla[l], state_rwkv[l], state_rwkv_shift[l], state_mlstm_c[l], state_mlstm_n[l], state_mlstm_m[l])
        xs, st = _mixer(gs, xs, lw, st_s, float(past_len), attend_s)
        conv_s = state_conv[l].transpose(1, 0, 2).reshape(1, 2 * b_s, D_FF)
        xs, conv_new = _cross_and_ffn(gs, xs, lw, mem_k_c, mem_v_c, l, conv_s)
        new_s.append(st + (conv_new.reshape(2, b_s, D_FF).transpose(1, 0, 2),))

    def stack(per_layer, i, grp):
        vals = [st[i] for st in per_layer]
        if i in (0, 1):
            vals = [grp.to_btc(v) for v in vals]
        return jnp.stack(vals, axis=0)

    outs = [gp.to_btc(xp), gs.to_btc(xs), stack(new_p, 0, gp), stack(new_p, 1, gp), stack(new_s, 0, gs), stack(new_s, 1, gs),
            jnp.stack(mem_k_new, axis=0), jnp.stack(mem_v_new, axis=0)]
    for i in range(2, 9):
        outs.append(stack(new_p, i, gp))
        outs.append(stack(new_s, i, gs))
    return tuple(outs)
```

```python
import functools
import math

import jax
import jax.numpy as jnp
from jax import lax
from jax.experimental import pallas as pl
from jax.experimental.pallas import tpu as pltpu

F32 = jnp.float32
BF16 = jnp.bfloat16

D_MODEL = 1024
BRANCH = 256
N_HEADS = 4
EPS = 1e-6
GLA_K = 32
GLA_TAU = 16.0
MLA_NOPE = 64
MLA_ROPE = 32
MLA_LORA = 128
ROPE_THETA = 10000.0
RWKV_LN_EPS = 64e-5
HEAD = 64
CA_DIM = 256
D_FF = 2816
FF_CHUNKS = 1

LANES = 128
SUBLANES = 8
VMEM_BUDGET = 56 * 1024 * 1024

W_GLA, W_MLA, W_RWKV, W_MLSTM, W_GATE = 896, 512, 1024, 1152, 4096
PIECES = (W_GLA, W_MLA, W_RWKV, W_MLSTM, W_GATE)


def _params(sem, vmem_bytes):
    return pltpu.CompilerParams(dimension_semantics=sem, vmem_limit_bytes=int(min(max(vmem_bytes, 16 << 20), VMEM_BUDGET)))


def _nbytes(shape, dtype=F32):
    return math.prod(shape) * jnp.dtype(dtype).itemsize


def _rms(x, g):
    return x * lax.rsqrt(jnp.mean(x * x, axis=-1, keepdims=True) + EPS) * g


def _bdot(a, b):
    return jnp.dot(a.astype(BF16), b.astype(BF16), preferred_element_type=F32)


def _bdot_t(a, b):
    return lax.dot_general(a.astype(BF16), b.astype(BF16), (((1,), (1,)), ((), ())), preferred_element_type=F32)


def _group_sum(x, ones_bd):
    hi = x.astype(BF16)
    lo = (x - hi.astype(F32)).astype(BF16)
    return jnp.dot(hi, ones_bd, preferred_element_type=F32) + jnp.dot(lo, ones_bd, preferred_element_type=F32)


def _softplus(x):
    return jnp.maximum(x, 0.0) + jnp.log1p(jnp.exp(-jnp.abs(x)))


def _log_sigmoid(x):
    return -_softplus(-x)


def _shift_rows(x, first, s):
    n = x.shape[0]
    if s % SUBLANES == 0:
        return jnp.concatenate([first, x[: n - s]], axis=0)
    rolled = pltpu.roll(x, s, 0)
    row = lax.broadcasted_iota(jnp.int32, x.shape, 0)
    for r in range(s):
        rolled = jnp.where(row == r, first[r : r + 1, :], rolled)
    return rolled


def _mix_in_kernel(x_ref, g_ref, w_ref, *outs):
    h = _rms(x_ref[...], g_ref[...]).astype(BF16)
    off = 0
    for o_ref, width in zip(outs, PIECES):
        piece = jnp.dot(h, w_ref[:, off : off + width], preferred_element_type=F32)
        if o_ref.dtype == BF16:
            piece = jax.nn.sigmoid(piece)
        o_ref[...] = piece.astype(o_ref.dtype)
        off += width


def _mix_in(x, g, w_all, tm):
    n = x.shape[0]
    total = sum(PIECES)
    vmem = 2 * _nbytes((tm, D_MODEL)) + 2 * _nbytes((D_MODEL, total), BF16) + 2 * _nbytes((tm, total)) + (4 << 20)
    return pl.pallas_call(
        _mix_in_kernel,
        grid=(n // tm,),
        in_specs=[
            pl.BlockSpec((tm, D_MODEL), lambda i: (i, 0)),
            pl.BlockSpec((1, D_MODEL), lambda i: (0, 0)),
            pl.BlockSpec((D_MODEL, total), lambda i: (0, 0)),
        ],
        out_specs=[pl.BlockSpec((tm, w), lambda i: (i, 0)) for w in PIECES],
        out_shape=[jax.ShapeDtypeStruct((n, w), BF16 if w == W_GATE else F32) for w in PIECES],
        compiler_params=_params(("parallel",), vmem),
    )(x, g, w_all)


def _prep_kernel(pg_ref, pr_ref, pm_ref, shift_ref,
                 gate2_ref, bgate_ref, mu_ref, w0_ref, w2_ref, a0_ref, a2_ref, g2_ref, kk_ref, ka_ref, rk_ref,
                 ones_ref, ibias_ref, fbias_ref,
                 gla_ref, rw_ref, aux_ref, ml_ref, carry, *, stride):
    i = pl.program_id(1)
    tm = pr_ref.shape[0]

    @pl.when(i == 0)
    def _():
        carry[...] = shift_ref[0]

    pr = pr_ref[...]
    prev = _shift_rows(pr, carry[...], stride)
    carry[...] = pr[tm - stride :, :]
    log_w, kk, a, k_mod, rr, rv, g, bonus = _rwkv_mix(
        pr, prev, mu_ref[...], w0_ref[...], w2_ref[...], a0_ref[...], a2_ref[...], g2_ref[...], kk_ref[...],
        ka_ref[...], rk_ref[...], ones_ref[...])
    rw_ref[:, 0:256] = jnp.exp(log_w)
    rw_ref[:, 256:512] = kk
    rw_ref[:, 512:768] = kk * a
    rw_ref[:, 768:1024] = k_mod
    rw_ref[:, 1024:1280] = rr
    rw_ref[:, 1280:1536] = rv
    aux_ref[:, 0:256] = g
    aux_ref[:, 256:512] = bonus

    pg = pg_ref[...]
    z = _bdot(pg[:, 768:896], gate2_ref[...]) + bgate_ref[...]
    gla_ref[:, 0:128] = pg[:, 0:128] * (GLA_K ** -0.5)
    gla_ref[:, 128:256] = jnp.exp(_log_sigmoid(z) / GLA_TAU)

    pml = pm_ref[...]
    ifp = pml[:, 1024:1152]
    lane = lax.broadcasted_iota(jnp.int32, ifp.shape, 1)
    ml_ref[:, 0:256] = pml[:, 0:256] * (HEAD ** -0.5)
    ml_ref[:, 256:384] = jnp.where(lane < N_HEADS, ifp + ibias_ref[...], _log_sigmoid(ifp + fbias_ref[...]))


def _prep(p_gla, p_rwkv, p_mlstm, shift0, wts, n_seq, tm, stride):
    n = p_rwkv.shape[0]
    n_tiles = n // n_seq // tm
    row = lambda w: pl.BlockSpec((tm, w), lambda q, i: (q * n_tiles + i, 0))
    const = lambda a: pl.BlockSpec(a.shape, lambda q, i: (0,) * a.ndim)
    out_w = (256, 1536, 512, 384)
    vmem = 2 * _nbytes((tm, W_GLA + W_RWKV + W_MLSTM + sum(out_w))) + 8 * _nbytes((tm, W_RWKV)) + (6 << 20)
    return pl.pallas_call(
        functools.partial(_prep_kernel, stride=stride),
        grid=(n_seq, n_tiles),
        in_specs=[row(W_GLA), row(W_RWKV), row(W_MLSTM),
                  pl.BlockSpec((1, stride, W_RWKV), lambda q, i: (q, 0, 0))] + [const(a) for a in wts],
        out_specs=[row(w) for w in out_w],
        out_shape=[jax.ShapeDtypeStruct((n, w), F32) for w in out_w],
        scratch_shapes=[pltpu.VMEM((stride, W_RWKV), F32)],
        compiler_params=_params(("arbitrary", "arbitrary"), vmem),
    )(p_gla, p_rwkv, p_mlstm, shift0, *wts)


def _rope32(x, cos2, sin2):
    half = MLA_ROPE // 2
    swapped = jnp.concatenate([x[:, half:], x[:, :half]], axis=1)
    return x * cos2 + swapped * sin2


def _mla_prep_kernel(p_ref, cos_ref, sin_ref, nq_ref, nkv_ref, wn_ref, wp_ref, wuk_ref,
                     qlat_ref, qpe_ref, ckv_ref, kpe_ref):
    p = p_ref[...]
    cos2, sin2 = cos_ref[...], sin_ref[...]
    cq = _rms(p[:, 0:256], nq_ref[...])
    q_nope = _bdot(cq, wn_ref[...])
    q_pe = _bdot(cq, wp_ref[...])
    qlat_ref[...] = _bdot(q_nope, wuk_ref[...])
    qpe_ref[...] = jnp.concatenate(
        [_rope32(q_pe[:, h * MLA_ROPE : (h + 1) * MLA_ROPE], cos2, sin2) for h in range(N_HEADS)], axis=1)
    ckv_ref[...] = _rms(p[:, 256:384], nkv_ref[...])
    kpe_ref[...] = _rope32(p[:, 384:416], cos2, sin2)


def _mla_prep(p_mla, cos2, sin2, wts, n_seq, tm):
    n = p_mla.shape[0]
    n_tiles = n // n_seq // tm
    row = lambda w: pl.BlockSpec((tm, w), lambda q, i: (q * n_tiles + i, 0))
    tab = pl.BlockSpec((tm, MLA_ROPE), lambda q, i: (i, 0))
    const = lambda a: pl.BlockSpec(a.shape, lambda q, i: (0,) * a.ndim)
    out_w = (N_HEADS * MLA_LORA, N_HEADS * MLA_ROPE, MLA_LORA, MLA_ROPE)
    vmem = 2 * _nbytes((tm, W_MLA + sum(out_w) + 2 * LANES)) + 6 * _nbytes((tm, 512)) + (4 << 20)
    return pl.pallas_call(
        _mla_prep_kernel,
        grid=(n_seq, n_tiles),
        in_specs=[row(W_MLA), tab, tab] + [const(a) for a in wts],
        out_specs=[row(w) for w in out_w],
        out_shape=[jax.ShapeDtypeStruct((n, w), F32) for w in out_w],
        compiler_params=_params(("parallel", "parallel"), vmem),
    )(p_mla, cos2, sin2, *wts)


MLA_SCALE = (MLA_NOPE + MLA_ROPE) ** -0.5


def _stack_heads(x, width):
    return jnp.concatenate([x[:, h * width : (h + 1) * width] for h in range(N_HEADS)], axis=0)


def _mla_prompt_kernel(ql_ref, qp_ref, ckv_ref, kpe_ref, o_ref, *, tq, tk):
    qi = pl.program_id(1)
    q = (_stack_heads(ql_ref[...], MLA_LORA) * MLA_SCALE).astype(BF16)
    qp = (_stack_heads(qp_ref[...], MLA_ROPE) * MLA_SCALE).astype(BF16)
    rows = N_HEADS * tq

    def block(kb, carry, causal):
        m, l, acc = carry
        k0 = pl.multiple_of(kb * tk, tk)
        kc = ckv_ref[pl.ds(k0, tk), :].astype(BF16)
        kp = kpe_ref[pl.ds(k0, tk), :].astype(BF16)
        s = _bdot_t(q, kc) + _bdot_t(qp, kp)
        if causal:
            tpos = lax.broadcasted_iota(jnp.int32, (tq, tk), 0)
            qpos = jnp.concatenate([tpos] * N_HEADS, axis=0)
            s = jnp.where(lax.broadcasted_iota(jnp.int32, (rows, tk), 1) <= qpos, s, -jnp.inf)
        m_new = jnp.maximum(m, jnp.max(s, axis=1, keepdims=True))
        p = jnp.exp(s - m_new)
        alpha = jnp.exp(m - m_new)
        l = alpha * l + jnp.sum(p, axis=1, keepdims=True)
        acc = alpha * acc + jnp.dot(p.astype(BF16), kc, preferred_element_type=F32)
        return m_new, l, acc

    init = (jnp.full((rows, 1), -jnp.inf, F32), jnp.zeros((rows, 1), F32), jnp.zeros((rows, MLA_LORA), F32))
    carry = lax.fori_loop(0, qi, functools.partial(block, causal=False), init)
    _, l, acc = block(qi, carry, causal=True)
    out = acc / l
    o_ref[...] = jnp.concatenate([out[h * tq : (h + 1) * tq, :] for h in range(N_HEADS)], axis=1)


def _mla_prompt(q_lat, q_pe, ckv, kpe, n_b, t_len):
    tq = tk = min(256, t_len)
    nq = t_len // tq
    vmem = 2 * _nbytes((tq, 1024 + 128)) + 2 * _nbytes((t_len, 256)) + 12 * _nbytes((N_HEADS * tq, tk)) + (4 << 20)
    return pl.pallas_call(
        functools.partial(_mla_prompt_kernel, tq=tq, tk=tk),
        grid=(n_b, nq),
        in_specs=[
            pl.BlockSpec((tq, N_HEADS * MLA_LORA), lambda b, i: (b * nq + i, 0)),
            pl.BlockSpec((tq, N_HEADS * MLA_ROPE), lambda b, i: (b * nq + i, 0)),
            pl.BlockSpec((t_len, MLA_LORA), lambda b, i: (b, 0)),
            pl.BlockSpec((t_len, MLA_ROPE), lambda b, i: (b, 0)),
        ],
        out_specs=pl.BlockSpec((tq, N_HEADS * MLA_LORA), lambda b, i: (b * nq + i, 0)),
        out_shape=jax.ShapeDtypeStruct((n_b * t_len, N_HEADS * MLA_LORA), F32),
        compiler_params=_params(("parallel", "parallel"), vmem),
    )(q_lat, q_pe, ckv, kpe)


def _mla_sample_kernel(pt_ref, ql_ref, qp_ref, cnew_ref, pnew_ref, *rest, n_pg, page):
    ckv_refs = rest[:n_pg]
    kpe_refs = rest[n_pg : 2 * n_pg]
    o_ref = rest[2 * n_pg]
    m_s, l_s, acc_s = rest[2 * n_pg + 1 :]
    g = pl.program_id(1)
    rows = ql_ref.shape[1]

    @pl.when(g == 0)
    def _():
        m_s[...] = jnp.full(m_s.shape, -jnp.inf, F32)
        l_s[...] = jnp.zeros(l_s.shape, F32)
        acc_s[...] = jnp.zeros(acc_s.shape, F32)

    q = (ql_ref[0] * MLA_SCALE).astype(BF16)
    qp = (qp_ref[0] * MLA_SCALE).astype(BF16)

    def update(s, v):
        m = m_s[...]
        m_new = jnp.maximum(m, jnp.max(s, axis=1, keepdims=True))
        alpha = jnp.exp(m - m_new)
        p = jnp.exp(s - m_new)
        m_s[...] = m_new
        l_s[...] = alpha * l_s[...] + jnp.sum(p, axis=1, keepdims=True)
        acc_s[...] = alpha * acc_s[...] + jnp.dot(p.astype(BF16), v, preferred_element_type=F32)

    kc = jnp.concatenate([c_ref[0, 0].astype(BF16) for c_ref in ckv_refs], axis=0)
    kp_t = jnp.concatenate([p_ref[0, 0].astype(BF16) for p_ref in kpe_refs], axis=1)
    update(_bdot_t(q, kc) + jnp.dot(qp, kp_t, preferred_element_type=F32), kc)

    @pl.when(g == pl.num_programs(1) - 1)
    def _():
        kn = cnew_ref[0].astype(BF16)
        kpn = pnew_ref[0].astype(BF16)
        s = _bdot_t(q, kn) + _bdot_t(qp, kpn)
        row = lax.broadcasted_iota(jnp.int32, (rows, page), 0)
        tk = lax.broadcasted_iota(jnp.int32, (rows, page), 1)
        s = jnp.where(tk * N_HEADS <= row, s, -jnp.inf)
        update(s, kn)
        o_ref[0] = acc_s[...] / l_s[...]


def _mla_sample(page_table, q_lat, q_pe, ckv_new, kpe_new, cache_ckv, cache_kpe_t, layer):
    n_b, n_pages = page_table.shape
    page = cache_ckv.shape[2]
    rows = q_lat.shape[1]
    n_pg = math.gcd(n_pages, 32)
    n_groups = n_pages // n_pg
    pt_flat = page_table.reshape(-1)

    def page_spec(shape, p):
        return pl.BlockSpec((1, 1) + shape, lambda b, g, pt: (layer, pt[b * n_pages + g * n_pg + p], 0, 0))

    per_b = lambda shape: pl.BlockSpec((1,) + shape, lambda b, g, pt: (b, 0, 0))
    grid_spec = pltpu.PrefetchScalarGridSpec(
        num_scalar_prefetch=1,
        grid=(n_b, n_groups),
        in_specs=[per_b((rows, MLA_LORA)), per_b((rows, MLA_ROPE)), per_b((page, MLA_LORA)), per_b((page, MLA_ROPE))]
        + [page_spec((page, MLA_LORA), p) for p in range(n_pg)] + [page_spec((MLA_ROPE, page), p) for p in range(n_pg)],
        out_specs=per_b((rows, MLA_LORA)),
        scratch_shapes=[pltpu.VMEM((rows, 1), F32), pltpu.VMEM((rows, 1), F32), pltpu.VMEM((rows, MLA_LORA), F32)],
    )
    vmem = 2 * n_pg * 2 * _nbytes((page, LANES)) + 4 * _nbytes((page, 2 * LANES)) + (8 << 20)
    return pl.pallas_call(
        functools.partial(_mla_sample_kernel, n_pg=n_pg, page=page),
        grid_spec=grid_spec,
        out_shape=jax.ShapeDtypeStruct((n_b, rows, MLA_LORA), F32),
        compiler_params=_params(("parallel", "arbitrary"), vmem),
    )(pt_flat, q_lat, q_pe, ckv_new, kpe_new, *([cache_ckv] * n_pg), *([cache_kpe_t] * n_pg))


V_TILES = HEAD // SUBLANES


def _scan_kernel(gk_ref, rk_ref, mk_ref, v_ref, sg0, sr0, sc0, sn0, sm0,
                 o_ref, sg_o, sr_o, sc_o, sn_o, sm_o, sg, sr, sc, sn, sm, *, tb, ksg, ks, groups):
    i = pl.program_id(1)

    @pl.when(i == 0)
    def _():
        sg[...] = sg0[...]
        sr[...] = sr0[...]
        sc[...] = sc0[...]
        sn[...] = sn0[...]
        sm[...] = sm0[...]

    def fold(x):
        return x + pltpu.roll(x, LANES // 2, 1) if groups == 2 else x

    def vt_slice(vt):
        return slice(vt * SUBLANES, (vt + 1) * SUBLANES)

    def rwkv_sa(t):
        acc = [jnp.zeros((SUBLANES, LANES), F32)] * V_TILES
        for k in range(ks):
            kk = rk_ref[t, ks + k : ks + k + 1, :]
            for vt in range(V_TILES):
                acc[vt] = acc[vt] + sr[k, vt_slice(vt), :] * kk
        return tuple(fold(a) for a in acc)

    def step(t, sa):
        tn = jnp.minimum(t + 1, tb - 1)
        zeros = [jnp.zeros((SUBLANES, LANES), F32)] * V_TILES

        gv = [v_ref[t, vt * SUBLANES : (vt + 1) * SUBLANES, :] for vt in range(V_TILES)]
        o = list(zeros)
        for k in range(ksg):
            q = gk_ref[t, k : k + 1, :]
            kr = gk_ref[t, ksg + k : ksg + k + 1, :]
            a = gk_ref[t, 2 * ksg + k : 2 * ksg + k + 1, :]
            for vt in range(V_TILES):
                new = sg[k, vt_slice(vt), :] * a + gv[vt] * kr
                sg[k, vt_slice(vt), :] = new
                o[vt] = o[vt] + new * q
        for vt in range(V_TILES):
            o_ref[t, vt_slice(vt), :] = fold(o[vt])

        rv = [v_ref[t, HEAD + vt * SUBLANES : HEAD + (vt + 1) * SUBLANES, :] for vt in range(V_TILES)]
        y = list(zeros)
        san = list(zeros)
        for k in range(ks):
            w = rk_ref[t, k : k + 1, :]
            b = rk_ref[t, 2 * ks + k : 2 * ks + k + 1, :]
            kr = rk_ref[t, 3 * ks + k : 3 * ks + k + 1, :]
            r = rk_ref[t, 4 * ks + k : 4 * ks + k + 1, :]
            kkn = rk_ref[tn, ks + k : ks + k + 1, :]
            for vt in range(V_TILES):
                new = sr[k, vt_slice(vt), :] * w - sa[vt] * b + rv[vt] * kr
                sr[k, vt_slice(vt), :] = new
                y[vt] = y[vt] + new * r
                san[vt] = san[vt] + new * kkn
        for vt in range(V_TILES):
            o_ref[t, HEAD + vt * SUBLANES : HEAD + (vt + 1) * SUBLANES, :] = fold(y[vt])

        mv = [v_ref[t, 2 * HEAD + vt * SUBLANES : 2 * HEAD + (vt + 1) * SUBLANES, :] for vt in range(V_TILES)]
        i_t = v_ref[t, 3 * HEAD : 3 * HEAD + 1, :]
        lf = v_ref[t, 3 * HEAD + 1 : 3 * HEAD + 2, :]
        m_old = sm[0:1, :]
        m_new = jnp.maximum(lf + m_old, i_t)
        fs = jnp.exp(lf + m_old - m_new)
        isc = jnp.exp(i_t - m_new)
        sm[0:1, :] = m_new
        q_tile = mk_ref[t, 0:ks, :]
        n_new = fs * sn[...] + isc * mk_ref[t, ks : 2 * ks, :]
        sn[...] = n_new
        den = fold(jnp.sum(n_new * q_tile, axis=0, keepdims=True))
        fs_b = jnp.broadcast_to(fs, (SUBLANES, LANES))
        num = list(zeros)
        for k in range(ks):
            q = mk_ref[t, k : k + 1, :]
            ik = isc * mk_ref[t, ks + k : ks + k + 1, :]
            for vt in range(V_TILES):
                new = sc[k, vt_slice(vt), :] * fs_b + mv[vt] * ik
                sc[k, vt_slice(vt), :] = new
                num[vt] = num[vt] + new * q
        scale = 1.0 / jnp.maximum(jnp.abs(den), jnp.exp(-m_new))
        for vt in range(V_TILES):
            o_ref[t, 2 * HEAD + vt * SUBLANES : 2 * HEAD + (vt + 1) * SUBLANES, :] = fold(num[vt]) * scale
        return tuple(fold(x) for x in san)

    lax.fori_loop(0, tb, step, rwkv_sa(0))

    @pl.when(i == pl.num_programs(1) - 1)
    def _():
        sg_o[...] = sg[...]
        sr_o[...] = sr[...]
        sc_o[...] = sc[...]
        sn_o[...] = sn[...]
        sm_o[...] = sm[...]


def _scan(gk, rk, mk, vops, states, t_len, groups):
    ks = HEAD // groups
    ksg = GLA_K // groups
    n_lanes = gk.shape[2]
    n_lt = n_lanes // LANES
    tb = min(32, t_len)
    op = lambda rows: pl.BlockSpec((tb, rows, LANES), lambda j, i: (i, 0, j))
    st3 = lambda k: pl.BlockSpec((k, HEAD, LANES), lambda j, i: (0, 0, j))
    st2 = lambda k: pl.BlockSpec((k, LANES), lambda j, i: (0, j))
    v_rows = vops.shape[1]
    st_specs = [st3(ksg), st3(ks), st3(ks), st2(ks), st2(SUBLANES)]
    st_shapes = [(ksg, HEAD, n_lanes), (ks, HEAD, n_lanes), (ks, HEAD, n_lanes), (ks, n_lanes), (SUBLANES, n_lanes)]
    blk = _nbytes((tb, 3 * ksg + 5 * ks + 2 * ks + v_rows + 3 * HEAD, LANES))
    st_bytes = _nbytes(((ksg + 2 * ks) * HEAD + ks + SUBLANES, LANES))
    return pl.pallas_call(
        functools.partial(_scan_kernel, tb=tb, ksg=ksg, ks=ks, groups=groups),
        grid=(n_lt, t_len // tb),
        in_specs=[op(3 * ksg), op(5 * ks), op(2 * ks), op(v_rows)] + st_specs,
        out_specs=[op(3 * HEAD)] + st_specs,
        out_shape=[jax.ShapeDtypeStruct((t_len, 3 * HEAD, n_lanes), F32)]
        + [jax.ShapeDtypeStruct(s, F32) for s in st_shapes],
        scratch_shapes=[pltpu.VMEM((ksg, HEAD, LANES), F32), pltpu.VMEM((ks, HEAD, LANES), F32),
                        pltpu.VMEM((ks, HEAD, LANES), F32), pltpu.VMEM((ks, LANES), F32),
                        pltpu.VMEM((SUBLANES, LANES), F32)],
        compiler_params=_params(("parallel", "arbitrary"), 2 * blk + 5 * st_bytes + (4 << 20)),
    )(gk, rk, mk, vops, *states)


def _head_ln(x, ones_bd, eps):
    mean = _group_sum(x, ones_bd) * (1.0 / HEAD)
    xc = x - mean
    var = _group_sum(xc * xc, ones_bd) * (1.0 / HEAD)
    return xc * lax.rsqrt(var + eps)


def _gla_out(o, gate_r, ones_bd, gnorm):
    ms = _group_sum(o * o, ones_bd) * (1.0 / HEAD)
    return o * lax.rsqrt(ms + EPS) * gnorm * jax.nn.silu(gate_r)


def _rwkv_out(y, bonus, g, ones_bd, lnw, lnb):
    return (_head_ln(y, ones_bd, RWKV_LN_EPS) * lnw + lnb + bonus) * g


def _mlstm_out(h, gate_o, ones_bd, mnorm):
    return _head_ln(h, ones_bd, EPS) * mnorm * jax.nn.sigmoid(gate_o)


def _post_kernel(scan_ref, pg_ref, aux_ref, pm_ref, ones_ref, gnorm_ref, lnw_ref, lnb_ref, mnorm_ref,
                 og_ref, or_ref, om_ref):
    ones_bd = ones_ref[...]
    sc = scan_ref[...]
    aux = aux_ref[...]
    og_ref[...] = _gla_out(sc[:, 0:256], pg_ref[:, 512:768], ones_bd, gnorm_ref[...])
    or_ref[...] = _rwkv_out(sc[:, 256:512], aux[:, 256:512], aux[:, 0:256], ones_bd, lnw_ref[...], lnb_ref[...])
    om_ref[...] = _mlstm_out(sc[:, 512:768], pm_ref[:, 768:1024], ones_bd, mnorm_ref[...])


def _post(scan_rows, p_gla, aux, p_mlstm, wts, tm):
    n = scan_rows.shape[0]
    row = lambda w: pl.BlockSpec((tm, w), lambda i: (i, 0))
    const = lambda a: pl.BlockSpec(a.shape, lambda i: (0,) * a.ndim)
    widths = (768, W_GLA, 512, W_MLSTM)
    vmem = 2 * _nbytes((tm, sum(widths) + 768)) + 12 * _nbytes((tm, BRANCH)) + (4 << 20)
    return pl.pallas_call(
        _post_kernel,
        grid=(n // tm,),
        in_specs=[row(w) for w in widths] + [const(a) for a in wts],
        out_specs=[row(BRANCH)] * 3,
        out_shape=[jax.ShapeDtypeStruct((n, BRANCH), F32)] * 3,
        compiler_params=_params(("parallel",), vmem),
    )(scan_rows, p_gla, aux, p_mlstm, *wts)


def _merge_kernel(x_ref, og_ref, olat_ref, or_ref, om_ref, gate_ref, wuv_ref, wbr_ref, wout_ref, gpost_ref, o_ref):
    o_mla = _bdot(olat_ref[...], wuv_ref[...])
    mixed = jnp.zeros((x_ref.shape[0], D_MODEL), F32)
    for n, br in enumerate((og_ref[...], o_mla, or_ref[...], om_ref[...])):
        up = _bdot(br, wbr_ref[n])
        mixed = mixed + gate_ref[:, n * D_MODEL : (n + 1) * D_MODEL].astype(F32) * up
    out = _bdot(mixed, wout_ref[...])
    o_ref[...] = x_ref[...] + _rms(out, gpost_ref[...])


def _merge(x, o_gla, o_lat, o_rwkv, o_mlstm, p_gate, wts, tm):
    n = x.shape[0]
    row = lambda w: pl.BlockSpec((tm, w), lambda i: (i, 0))
    const = lambda a: pl.BlockSpec(a.shape, lambda i: (0,) * a.ndim)
    widths = (D_MODEL, BRANCH, 512, BRANCH, BRANCH, W_GATE)
    vmem = 2 * _nbytes((tm, sum(widths) + D_MODEL)) + 10 * _nbytes((tm, D_MODEL)) + (12 << 20)
    return pl.pallas_call(
        _merge_kernel,
        grid=(n // tm,),
        in_specs=[row(w) for w in widths] + [const(a) for a in wts],
        out_specs=row(D_MODEL),
        out_shape=jax.ShapeDtypeStruct((n, D_MODEL), F32),
        compiler_params=_params(("parallel",), vmem),
    )(x, o_gla, o_lat, o_rwkv, o_mlstm, p_gate, *wts)


GLA_CHUNK = 32
MIX_CHUNK = 64


def _bdot_tn(a, b):
    return lax.dot_general(a.astype(BF16), b.astype(BF16), (((0,), (0,)), ((), ())), preferred_element_type=F32)


def _cumsum_rows(tri, x):
    hi = x.astype(BF16)
    lo = (x - hi.astype(F32)).astype(BF16)
    return jnp.dot(tri, hi, preferred_element_type=F32) + jnp.dot(tri, lo, preferred_element_type=F32)


def _rep_rows(x, mask):
    return jnp.concatenate([x] * N_HEADS, axis=0) * mask


def _gla_chunk_kernel(pg_ref, gate2_ref, bgate_ref, gnorm_ref, ones_ref, tri_ref, causal_ref, kmask_ref, vmask_ref,
                      smask_ref, o_ref, st_ref, st, *, tc):
    i = pl.program_id(1)

    @pl.when(i == 0)
    def _():
        st[...] = jnp.zeros(st.shape, F32)

    L = GLA_CHUNK
    pg = pg_ref[...]
    log_a = _log_sigmoid(_bdot(pg[:, 768:896], gate2_ref[...]) + bgate_ref[...]) / GLA_TAU
    tri, causal = tri_ref[...], causal_ref[...]
    kmask, vmask, smask = kmask_ref[...], vmask_ref[...], smask_ref[...]
    rows = [slice(c * L, (c + 1) * L) for c in range(tc // L)]
    b = [_cumsum_rows(tri, log_a[r]) for r in rows]
    b_last = [x[L - 1 : L, :] for x in b]
    q_dec = [pg[r, 0:128] * (GLA_K ** -0.5) * jnp.exp(x) for r, x in zip(rows, b)]
    k_rep = [_rep_rows(pg[r, 128:256] * jnp.exp(-x), kmask) for r, x in zip(rows, b)]
    v_rep = [_rep_rows(pg[r, 256:512], vmask) for r in rows]
    att = [_bdot_t(qd, kr) * causal for qd, kr in zip(q_dec, k_rep)]
    o_intra = [_bdot(x, vr) for x, vr in zip(att, v_rep)]
    upd = [_bdot_tn(pg[r, 256:512], pg[r, 128:256] * jnp.exp(bl - x)) * smask for r, x, bl in zip(rows, b, b_last)]
    states = []
    s_t = st[...]
    for bl, ud in zip(b_last, upd):
        states.append(s_t)
        s_t = s_t * jnp.exp(bl) + ud
    st[...] = s_t
    o = jnp.concatenate([_bdot_t(qd, s) + oi for qd, s, oi in zip(q_dec, states, o_intra)], axis=0)
    o_ref[...] = _gla_out(o, pg[:, 512:768], ones_ref[...], gnorm_ref[...])

    @pl.when(i == pl.num_programs(1) - 1)
    def _():
        st_ref[0] = st[...]


def _gla_chunk(p_gla, wts, n_b, t_len):
    tc = min(256, t_len)
    nt = t_len // tc
    const = lambda a: pl.BlockSpec(a.shape, lambda b, i: (0,) * a.ndim)
    vmem = 2 * _nbytes((tc, W_GLA + BRANCH)) + 12 * _nbytes((tc, BRANCH)) + (6 << 20)
    return pl.pallas_call(
        functools.partial(_gla_chunk_kernel, tc=tc),
        grid=(n_b, nt),
        in_specs=[pl.BlockSpec((tc, W_GLA), lambda b, i: (b * nt + i, 0))] + [const(a) for a in wts],
        out_specs=[pl.BlockSpec((tc, BRANCH), lambda b, i: (b * nt + i, 0)),
                   pl.BlockSpec((1, BRANCH, N_HEADS * GLA_K), lambda b, i: (b, 0, 0))],
        out_shape=[jax.ShapeDtypeStruct((n_b * t_len, BRANCH), F32),
                   jax.ShapeDtypeStruct((n_b, BRANCH, N_HEADS * GLA_K), F32)],
        scratch_shapes=[pltpu.VMEM((BRANCH, N_HEADS * GLA_K), F32)],
        compiler_params=_params(("parallel", "arbitrary"), vmem),
    )(p_gla, *wts)


def _mlstm_chunk_kernel(pm_ref, ibias_ref, fbias_ref, mnorm_ref, ones_ref, tri_ref,
                        o_ref, c_out, n_out, m_out, cs, ns, ms, *, tc):
    i = pl.program_id(1)

    @pl.when(i == 0)
    def _():
        cs[...] = jnp.zeros(cs.shape, F32)
        ns[...] = jnp.zeros(ns.shape, F32)
        ms[...] = jnp.zeros(ms.shape, F32)

    L = MIX_CHUNK
    pm = pm_ref[...]
    ifp = pm[:, 1024:1152]
    lane = lax.broadcasted_iota(jnp.int32, ifp.shape, 1)
    gates = jnp.where(lane < N_HEADS, ifp + ibias_ref[...], _log_sigmoid(ifp + fbias_ref[...]))
    tri = tri_ref[...]
    causal = lax.broadcasted_iota(jnp.int32, (L, L), 1) <= lax.broadcasted_iota(jnp.int32, (L, L), 0)
    lane_c = lax.broadcasted_iota(jnp.int32, (L, LANES), 1)
    row_t = lax.broadcasted_iota(jnp.int32, (LANES, L), 0)
    last_lane = lax.broadcasted_iota(jnp.int32, (1, L), 1) == L - 1

    def pick_col(x, j):
        return jnp.sum(jnp.where(lane_c == j, x, 0.0), axis=1, keepdims=True)

    def pick_row(x_t, j):
        return jnp.sum(jnp.where(row_t == j, x_t, 0.0), axis=0, keepdims=True)

    n_ch = tc // L
    rows = [slice(c * L, (c + 1) * L) for c in range(n_ch)]
    units = [(c, h) for c in range(n_ch) for h in range(N_HEADS)]
    z = [jnp.where(lane_c < N_HEADS, gates[r], _cumsum_rows(tri, gates[r])) for r in rows]
    z_t = [x.T for x in z]
    i_col = [pick_col(z[c], h) for c, h in units]
    b_col = [pick_col(z[c], N_HEADS + h) for c, h in units]
    i_row = [pick_row(z_t[c], h) for c, h in units]
    b_row = [pick_row(z_t[c], N_HEADS + h) for c, h in units]
    b_last = [jnp.sum(jnp.where(last_lane, x, 0.0), axis=1, keepdims=True) for x in b_row]
    g_max = [jnp.max(bl - br + ir, axis=1, keepdims=True) for bl, br, ir in zip(b_last, b_row, i_row)]
    m0, m1 = [None] * len(units), [None] * len(units)
    for h in range(N_HEADS):
        m = jnp.max(ms[h : h + 1, :], axis=1, keepdims=True)
        for c in range(n_ch):
            u = c * N_HEADS + h
            m0[u] = m
            m = jnp.maximum(b_last[u] + m, g_max[u])
            m1[u] = m
        ms[h : h + 1, :] = jnp.broadcast_to(m, (1, LANES))
    q = [pm[rows[c], h * HEAD : (h + 1) * HEAD] * (HEAD ** -0.5) for c, h in units]
    k = [pm[rows[c], 256 + h * HEAD : 256 + (h + 1) * HEAD] for c, h in units]
    v = [pm[rows[c], 512 + h * HEAD : 512 + (h + 1) * HEAD] for c, h in units]
    d = [jnp.where(causal, bc - br + ir, -jnp.inf) for bc, br, ir in zip(b_col, b_row, i_row)]
    inter = [bc + m for bc, m in zip(b_col, m0)]
    m_t = [jnp.maximum(x, jnp.max(dd, axis=1, keepdims=True)) for x, dd in zip(inter, d)]
    w_inter = [jnp.exp(x - mt) for x, mt in zip(inter, m_t)]
    s_qk = [_bdot_t(a_, b_) for a_, b_ in zip(q, k)]
    w_intra = [jnp.exp(dd - mt) * s for dd, mt, s in zip(d, m_t, s_qk)]
    num_intra = [_bdot(w, x) for w, x in zip(w_intra, v)]
    den_intra = [jnp.sum(w, axis=1, keepdims=True) for w in w_intra]
    floor = [jnp.exp(-mt) for mt in m_t]
    scale_old = [jnp.exp(bl + a_ - b_) for bl, a_, b_ in zip(b_last, m0, m1)]
    kw = [x * jnp.exp(bl - bc + ic - m) for x, bl, bc, ic, m in zip(k, b_last, b_col, i_col, m1)]
    c_upd = [_bdot_tn(x, y) for x, y in zip(kw, v)]
    n_upd = [jnp.sum(x, axis=0, keepdims=True) for x in kw]
    chunks = []
    for c in range(n_ch):
        heads = []
        for h in range(N_HEADS):
            u = c * N_HEADS + h
            c_h, n_h = cs[h], ns[h : h + 1, :]
            num = w_inter[u] * _bdot(q[u], c_h) + num_intra[u]
            den = w_inter[u] * jnp.sum(q[u] * n_h, axis=1, keepdims=True) + den_intra[u]
            heads.append(num / jnp.maximum(jnp.abs(den), floor[u]))
            cs[h] = scale_old[u] * c_h + c_upd[u]
            ns[h : h + 1, :] = scale_old[u] * n_h + n_upd[u]
        chunks.append(jnp.concatenate(heads, axis=1))
    hm = jnp.concatenate(chunks, axis=0)
    o_ref[...] = _mlstm_out(hm, pm[:, 768:1024], ones_ref[...], mnorm_ref[...])

    @pl.when(i == pl.num_programs(1) - 1)
    def _():
        c_out[0] = cs[...]
        n_out[0] = ns[...]
        m_out[0] = ms[...]


def _mlstm_chunk(p_mlstm, wts, n_b, t_len):
    tc = min(512, t_len)
    nt = t_len // tc
    const = lambda a: pl.BlockSpec(a.shape, lambda b, i: (0,) * a.ndim)
    vmem = 2 * _nbytes((tc, W_MLSTM + BRANCH)) + 12 * _nbytes((tc, BRANCH)) + (6 << 20)
    return pl.pallas_call(
        functools.partial(_mlstm_chunk_kernel, tc=tc),
        grid=(n_b, nt),
        in_specs=[pl.BlockSpec((tc, W_MLSTM), lambda b, i: (b * nt + i, 0))] + [const(a) for a in wts],
        out_specs=[pl.BlockSpec((tc, BRANCH), lambda b, i: (b * nt + i, 0)),
                   pl.BlockSpec((1, N_HEADS, HEAD, HEAD), lambda b, i: (b, 0, 0, 0)),
                   pl.BlockSpec((1, SUBLANES, HEAD), lambda b, i: (b, 0, 0)),
                   pl.BlockSpec((1, SUBLANES, LANES), lambda b, i: (b, 0, 0))],
        out_shape=[jax.ShapeDtypeStruct((n_b * t_len, BRANCH), F32),
                   jax.ShapeDtypeStruct((n_b, N_HEADS, HEAD, HEAD), F32),
                   jax.ShapeDtypeStruct((n_b, SUBLANES, HEAD), F32),
                   jax.ShapeDtypeStruct((n_b, SUBLANES, LANES), F32)],
        scratch_shapes=[pltpu.VMEM((N_HEADS, HEAD, HEAD), F32), pltpu.VMEM((SUBLANES, HEAD), F32),
                        pltpu.VMEM((SUBLANES, LANES), F32)],
        compiler_params=_params(("parallel", "arbitrary"), vmem),
    )(p_mlstm, *wts)


def _rwkv_mix(pr, prev, mu, w0, w2, a0, a2, g2, kkw, ka, rkw, ones_bd):
    pm = pr + (prev - pr) * mu
    rr, rk, rv = pm[:, 0:256], pm[:, 256:512], pm[:, 512:768]
    wlo, alo, rglo = pm[:, 768:832], pm[:, 832:896], pm[:, 896:1024]
    log_w = -jnp.exp(-_softplus(-(w0 + _bdot(jnp.tanh(wlo), w2))) - 0.5)
    a = jax.nn.sigmoid(a0 + _bdot(alo, a2))
    g = _bdot(jax.nn.sigmoid(rglo), g2)
    kkr = rk * kkw
    kk = kkr / jnp.maximum(jnp.sqrt(_group_sum(kkr * kkr, ones_bd)), 1e-12)
    k_mod = rk * (1.0 + (a - 1.0) * ka)
    bonus = _group_sum(rr * k_mod * rkw, ones_bd) * rv
    return log_w, kk, a, k_mod, rr, rv, g, bonus


def _rwkv_chunk_kernel(pr_ref, mu_ref, w0_ref, w2_ref, a0_ref, a2_ref, g2_ref, kk_ref, ka_ref, rk_ref, ones_ref,
                       lnw_ref, lnb_ref, tri_ref, strict_ref, incl_ref, o_ref, s_out, carry, sv, *, tc):
    i = pl.program_id(1)

    @pl.when(i == 0)
    def _():
        carry[...] = jnp.zeros(carry.shape, F32)
        sv[...] = jnp.zeros(sv.shape, F32)

    L = MIX_CHUNK
    pr = pr_ref[...]
    prev = _shift_rows(pr, carry[...], 1)
    carry[...] = pr[tc - 1 :, :]
    ones_bd = ones_ref[...]
    log_w, kk, a, k_mod, rr, rv, g, bonus = _rwkv_mix(
        pr, prev, mu_ref[...], w0_ref[...], w2_ref[...], a0_ref[...], a2_ref[...], g2_ref[...], kk_ref[...],
        ka_ref[...], rk_ref[...], ones_bd)
    bd = ones_bd.astype(F32)
    tri, strict, incl = tri_ref[...], strict_ref[...], incl_ref[...]
    chunks = range(tc // L)
    rows = [slice(c * L, (c + 1) * L) for c in chunks]
    cum = [_cumsum_rows(tri, log_w[r]) for r in rows]
    w_end = [jnp.exp(cm[L - 1 : L, :]) for cm in cum]
    v_c = [rv[r] for r in rows]
    v_rep = [_rep_rows(v, bd) for v in v_c]
    al, rt, k_end, gm = [], [], [], []
    for c, r in zip(chunks, rows):
        end = jnp.exp(cum[c][L - 1 : L, :] - cum[c])
        w_inv = jnp.exp(-cum[c])
        ka_c = kk[r] * a[r]
        al.append(-kk[r] * jnp.exp(cum[c] - log_w[r]))
        rt.append(rr[r] * jnp.exp(cum[c]))
        k_end.append(jnp.concatenate([ka_c * end, k_mod[r] * end], axis=0))
        gm.append(_bdot_t(jnp.concatenate([al[c], rt[c]], axis=0),
                          jnp.concatenate([_rep_rows(ka_c * w_inv, bd), _rep_rows(k_mod[r] * w_inv, bd)], axis=0)))
    a_ab = [jnp.where(strict > 0, g_[0:L, 0:256], 0.0) for g_ in gm]
    a_ak = [jnp.where(strict > 0, g_[0:L, 256:512], 0.0) for g_ in gm]
    r_b = [jnp.where(incl > 0, g_[L : 2 * L, 0:256], 0.0) for g_ in gm]
    r_k = [jnp.where(incl > 0, g_[L : 2 * L, 256:512], 0.0) for g_ in gm]
    t_cols = [(incl - strict) + x for x in a_ab]
    p_cols = [_bdot(x, _rep_rows(x, bd)) for x in a_ab]
    for _ in range(4):
        prod = [_bdot(jnp.concatenate([t, p], axis=0), _rep_rows(p, bd)) for t, p in zip(t_cols, p_cols)]
        t_cols = [t + pr_[0:L] for t, pr_ in zip(t_cols, prod)]
        p_cols = [pr_[L : 2 * L] for pr_ in prod]
    t_cols = [t + _bdot(t, _rep_rows(p, bd)) for t, p in zip(t_cols, p_cols)]
    al2 = [_bdot(t, _rep_rows(x, bd)) for t, x in zip(t_cols, al)]
    u_fix = [_bdot(t, _rep_rows(_bdot(x, v), bd)) for t, x, v in zip(t_cols, a_ak, v_rep)]
    y_fix = [_bdot(x, v) for x, v in zip(r_k, v_rep)]
    outs = []
    for c in chunks:
        s_vk = sv[...]
        u = _bdot_t(al2[c], s_vk) + u_fix[c]
        outs.append(_bdot_t(rt[c], s_vk) + _bdot(r_b[c], _rep_rows(u, bd)) + y_fix[c])
        sv[...] = s_vk * w_end[c] + _bdot_tn(jnp.concatenate([u, v_c[c]], axis=0), k_end[c]) * bd
    y = jnp.concatenate(outs, axis=0)
    o_ref[...] = _rwkv_out(y, bonus, g, ones_bd, lnw_ref[...], lnb_ref[...])

    @pl.when(i == pl.num_programs(1) - 1)
    def _():
        s_out[0] = sv[...]


def _rwkv_chunk(p_rwkv, wts, n_b, t_len):
    tc = min(512, t_len)
    nt = t_len // tc
    const = lambda a: pl.BlockSpec(a.shape, lambda b, i: (0,) * a.ndim)
    vmem = 2 * _nbytes((tc, W_RWKV + BRANCH)) + 24 * _nbytes((tc, BRANCH)) + 16 * _nbytes((BRANCH, BRANCH)) + (6 << 20)
    return pl.pallas_call(
        functools.partial(_rwkv_chunk_kernel, tc=tc),
        grid=(n_b, nt),
        in_specs=[pl.BlockSpec((tc, W_RWKV), lambda b, i: (b * nt + i, 0))] + [const(a) for a in wts],
        out_specs=[pl.BlockSpec((tc, BRANCH), lambda b, i: (b * nt + i, 0)),
                   pl.BlockSpec((1, BRANCH, BRANCH), lambda b, i: (b, 0, 0))],
        out_shape=[jax.ShapeDtypeStruct((n_b * t_len, BRANCH), F32), jax.ShapeDtypeStruct((n_b, BRANCH, BRANCH), F32)],
        scratch_shapes=[pltpu.VMEM((1, W_RWKV), F32), pltpu.VMEM((BRANCH, BRANCH), F32)],
        compiler_params=_params(("parallel", "arbitrary"), vmem),
    )(p_rwkv, *wts)


def _norm_matmul_kernel(x_ref, g_ref, w_ref, *outs):
    h = _rms(x_ref[...], g_ref[...]).astype(BF16)
    off = 0
    for o_ref in outs:
        width = o_ref.shape[1]
        o_ref[...] = jnp.dot(h, w_ref[:, off : off + width], preferred_element_type=F32)
        off += width


def _norm_matmul(x, g, w, widths, tm):
    n = x.shape[0]
    total = sum(widths)
    vmem = 2 * _nbytes((tm, D_MODEL + total)) + 2 * _nbytes((D_MODEL, total), BF16) + (6 << 20)
    return pl.pallas_call(
        _norm_matmul_kernel,
        grid=(n // tm,),
        in_specs=[pl.BlockSpec((tm, D_MODEL), lambda i: (i, 0)), pl.BlockSpec((1, D_MODEL), lambda i: (0, 0)),
                  pl.BlockSpec((D_MODEL, total), lambda i: (0, 0))],
        out_specs=[pl.BlockSpec((tm, w_), lambda i: (i, 0)) for w_ in widths],
        out_shape=[jax.ShapeDtypeStruct((n, w_), F32) for w_ in widths],
        compiler_params=_params(("parallel",), vmem),
    )(x, g, w)


def _proj_res_kernel(x_ref, a_ref, w_ref, g_ref, o_ref):
    o_ref[...] = x_ref[...] + _rms(_bdot(a_ref[...], w_ref[...]), g_ref[...])


def _proj_res(x, a, w, g, tm):
    n = x.shape[0]
    row = pl.BlockSpec((tm, D_MODEL), lambda i: (i, 0))
    vmem = 6 * _nbytes((tm, D_MODEL)) + 2 * _nbytes((D_MODEL, D_MODEL), BF16) + (6 << 20)
    return pl.pallas_call(
        _proj_res_kernel,
        grid=(n // tm,),
        in_specs=[row, row, pl.BlockSpec((D_MODEL, D_MODEL), lambda i: (0, 0)), pl.BlockSpec((1, D_MODEL), lambda i: (0, 0))],
        out_specs=row,
        out_shape=jax.ShapeDtypeStruct((n, D_MODEL), F32),
        compiler_params=_params(("parallel",), vmem),
    )(x, a, w, g)


def _cross_attn_kernel(q_ref, k_ref, v_ref, o_ref):
    o_ref[...] = _attend_memory(q_ref[...], k_ref, v_ref)


def _cross_fused_kernel(x_ref, gpre_ref, wq_ref, k_ref, v_ref, wo_ref, gpost_ref, o_ref):
    x = x_ref[...]
    ca = _attend_memory(_bdot(_rms(x, gpre_ref[...]), wq_ref[...]), k_ref, v_ref)
    o_ref[...] = x + _rms(_bdot(ca, wo_ref[...]), gpost_ref[...])


def _cross_fused(x, gpre, wq, mem_k, mem_v, wo, gpost, layer, n_b, tq):
    n = x.shape[0]
    nt = n // n_b // tq
    mem = mem_k.shape[2]
    row = pl.BlockSpec((tq, D_MODEL), lambda b, i: (b * nt + i, 0))
    vec = pl.BlockSpec((1, D_MODEL), lambda b, i: (0, 0))
    wsp = pl.BlockSpec((D_MODEL, D_MODEL), lambda b, i: (0, 0))
    mspec = pl.BlockSpec((None, None, mem, D_MODEL), lambda b, i: (layer, b, 0, 0))
    vmem = 4 * _nbytes((tq, D_MODEL)) + 4 * _nbytes((mem, D_MODEL)) + 4 * _nbytes((D_MODEL, D_MODEL), BF16) \
        + 10 * _nbytes((tq, D_MODEL)) + (6 << 20)
    return pl.pallas_call(
        _cross_fused_kernel,
        grid=(n_b, nt),
        in_specs=[row, vec, wsp, mspec, mspec, wsp, vec],
        out_specs=row,
        out_shape=jax.ShapeDtypeStruct((n, D_MODEL), F32),
        compiler_params=_params(("parallel", "parallel"), vmem),
    )(x, gpre, wq, mem_k, mem_v, wo, gpost)


def _attend_memory(q, k_ref, v_ref):
    outs = []
    for h in range(N_HEADS):
        cols = slice(h * CA_DIM, (h + 1) * CA_DIM)
        s = _bdot_t(q[:, cols], k_ref[:, cols]) * (CA_DIM ** -0.5)
        s = s - jnp.max(s, axis=1, keepdims=True)
        p = jnp.exp(s)
        p = p / jnp.sum(p, axis=1, keepdims=True)
        outs.append(_bdot(p, v_ref[:, cols]))
    return jnp.concatenate(outs, axis=1)


def _cross_attn(q, mem_k, mem_v, layer, tq):
    n_b, t_len, _ = q.shape
    mem = mem_k.shape[2]
    qspec = pl.BlockSpec((None, tq, D_MODEL), lambda b, i: (b, i, 0))
    mspec = pl.BlockSpec((None, None, mem, D_MODEL), lambda b, i: (layer, b, 0, 0))
    vmem = 4 * _nbytes((tq, D_MODEL)) + 4 * _nbytes((mem, D_MODEL)) + 8 * _nbytes((tq, D_MODEL)) + (6 << 20)
    return pl.pallas_call(
        _cross_attn_kernel,
        grid=(n_b, t_len // tq),
        in_specs=[qspec, mspec, mspec],
        out_specs=qspec,
        out_shape=jax.ShapeDtypeStruct(q.shape, F32),
        compiler_params=_params(("parallel", "parallel"), vmem),
    )(q, mem_k, mem_v)


def _gelu(x):
    return 0.5 * x * (1.0 + lax.erf(x * (2.0 ** -0.5)))


def _ffn_kernel(x_ref, gpre_ref, wa_ref, wg_ref, cw_ref, cb_ref, wd_ref, st_ref, gpost_ref,
                o_ref, cnew_ref, hn, acc, carry, *, stride):
    i = pl.program_id(1)
    c = pl.program_id(2)
    tm = x_ref.shape[0]

    @pl.when(c == 0)
    def _():
        hn[...] = _rms(x_ref[...], gpre_ref[...]).astype(BF16)
        acc[...] = jnp.zeros(acc.shape, F32)

    @pl.when(i == 0)
    def _():
        carry[c] = st_ref[0]

    h = hn[...]
    a = jnp.dot(h, wa_ref[...], preferred_element_type=F32)
    gate = jnp.dot(h, wg_ref[...], preferred_element_type=F32)
    st = carry[c]
    prev1 = _shift_rows(a, st[stride:, :], stride)
    prev2 = _shift_rows(a, st, 2 * stride)
    cw = cw_ref[...]
    conv = cb_ref[...] + prev2 * cw[0:1, :] + prev1 * cw[1:2, :] + a * cw[2:3, :]
    tail = a[tm - 2 * stride :, :]
    carry[c] = tail
    fc = a.shape[1]
    for cc in range(FF_CHUNKS):
        @pl.when((c == cc) & (i == pl.num_programs(1) - 1))
        def _(cc=cc):
            cnew_ref[0, :, cc * fc : (cc + 1) * fc] = tail
    acc[...] += _bdot(_gelu(conv) * gate, wd_ref[...])

    @pl.when(c == pl.num_programs(2) - 1)
    def _():
        o_ref[...] = x_ref[...] + _rms(acc[...], gpost_ref[...])


def _ffn(x, gpre, wa, wg, cw, cb, wd, state, gpost, n_seq, tm, stride):
    n = x.shape[0]
    n_tiles = n // n_seq // tm
    fc = D_FF // FF_CHUNKS
    row = pl.BlockSpec((tm, D_MODEL), lambda q, i, c: (q * n_tiles + i, 0))
    vec = pl.BlockSpec((1, D_MODEL), lambda q, i, c: (0, 0))
    st = pl.BlockSpec((1, 2 * stride, fc), lambda q, i, c: (q, 0, c))
    vmem = (4 * _nbytes((tm, D_MODEL)) + 4 * _nbytes((D_MODEL, fc), BF16) + 2 * _nbytes((fc, D_MODEL), BF16)
            + 2 * _nbytes((tm, D_MODEL)) + 8 * _nbytes((tm, fc)) + 6 * _nbytes((2 * stride, fc)) + (4 << 20))
    return pl.pallas_call(
        functools.partial(_ffn_kernel, stride=stride),
        grid=(n_seq, n_tiles, FF_CHUNKS),
        in_specs=[row, vec,
                  pl.BlockSpec((D_MODEL, fc), lambda q, i, c: (0, c)),
                  pl.BlockSpec((D_MODEL, fc), lambda q, i, c: (0, c)),
                  pl.BlockSpec((3, fc), lambda q, i, c: (0, c)),
                  pl.BlockSpec((1, fc), lambda q, i, c: (0, c)),
                  pl.BlockSpec((fc, D_MODEL), lambda q, i, c: (c, 0)),
                  st, vec],
        out_specs=[row, pl.BlockSpec((1, 2 * stride, D_FF), lambda q, i, c: (q, 0, 0))],
        out_shape=[jax.ShapeDtypeStruct((n, D_MODEL), F32), jax.ShapeDtypeStruct((n_seq, 2 * stride, D_FF), F32)],
        scratch_shapes=[pltpu.VMEM((tm, D_MODEL), BF16), pltpu.VMEM((tm, D_MODEL), F32),
                        pltpu.VMEM((FF_CHUNKS, 2 * stride, fc), F32)],
        compiler_params=_params(("arbitrary", "arbitrary", "arbitrary"), vmem),
    )(x, gpre, wa, wg, cw, cb, wd, state, gpost)


def _pad_cols(w, width):
    return jnp.pad(w, ((0, 0), (0, width - w.shape[1])))


def _row(v):
    return v.reshape(1, -1).astype(F32)


def _block_diag(blocks):
    rows = sum(b.shape[0] for b in blocks)
    cols = sum(b.shape[1] for b in blocks)
    out = jnp.zeros((rows, cols), blocks[0].dtype)
    r = c = 0
    for b in blocks:
        out = lax.dynamic_update_slice(out, b, (r, c))
        r += b.shape[0]
        c += b.shape[1]
    return out


def _layer_weights(p, l):
    w = p["w_in"][l]
    gla, mla, rw, ml = w[:, 0:784], w[:, 784:1200], w[:, 1200:2224], w[:, 2224:3256]
    gate = w[:, 3256:]
    w_gla = jnp.concatenate([gla[:, 0:512], gla[:, 528:784], _pad_cols(gla[:, 512:528], LANES)], axis=1)
    w_mla = _pad_cols(mla, W_MLA)
    w_ml = jnp.concatenate([ml[:, 0:768], ml[:, 776:1032], _pad_cols(ml[:, 768:776], LANES)], axis=1)
    w_all = jnp.concatenate([w_gla, w_mla, rw, w_ml, gate], axis=1).astype(BF16)

    ones_bd = _block_diag([jnp.ones((HEAD, HEAD), BF16)] * N_HEADS)
    ibias = jnp.zeros((LANES,), F32).at[0:N_HEADS].set(p["mlstm_i_bias"][l])
    fbias = jnp.zeros((LANES,), F32).at[N_HEADS : 2 * N_HEADS].set(p["mlstm_f_bias"][l])
    prep = [
        jnp.pad(p["gla_w_gate2"][l], ((0, LANES - 16), (0, 0))).astype(BF16), _row(p["gla_b_gate"][l]),
        _row(p["rwkv_mu"][l]), _row(p["rwkv_w0"][l]), p["rwkv_w2"][l].astype(BF16), _row(p["rwkv_a0"][l]),
        p["rwkv_a2"][l].astype(BF16), p["rwkv_g2"][l].astype(BF16), _row(p["rwkv_k_k"][l]), _row(p["rwkv_k_a"][l]),
        _row(p["rwkv_r_k"][l]), ones_bd, _row(ibias), _row(fbias),
    ]
    wuq = p["mla_w_uq"][l].reshape(256, N_HEADS, MLA_NOPE + MLA_ROPE)
    mla_w = [
        _row(p["mla_norm_q"][l]), _row(p["mla_norm_kv"][l]),
        wuq[:, :, :MLA_NOPE].reshape(256, N_HEADS * MLA_NOPE).astype(BF16),
        wuq[:, :, MLA_NOPE:].reshape(256, N_HEADS * MLA_ROPE).astype(BF16),
        _block_diag([p["mla_w_uk"][l][h].T for h in range(N_HEADS)]).astype(BF16),
    ]
    gnorm = _row(jnp.tile(p["gla_norm"][l], N_HEADS))
    post = [ones_bd, gnorm, _row(p["rwkv_ln_w"][l]), _row(p["rwkv_ln_b"][l]), _row(p["mlstm_norm"][l])]
    merge = [
        _block_diag([p["mla_w_uv"][l][h] for h in range(N_HEADS)]).astype(BF16),
        p["w_branch"][l].astype(BF16), p["w_out"][l].astype(BF16), _row(p["norm_post_mix"][l]),
    ]
    tri = lambda n: jnp.tril(jnp.ones((n, n), F32))
    blocks = lambda r, c: _block_diag([jnp.ones((r, c), F32)] * N_HEADS)
    gla_c = prep[0:2] + [gnorm, ones_bd, tri(GLA_CHUNK).astype(BF16), jnp.tile(tri(GLA_CHUNK), (1, N_HEADS)),
                         blocks(GLA_CHUNK, GLA_K), blocks(GLA_CHUNK, HEAD), blocks(HEAD, GLA_K)]
    mlstm_c = [_row(ibias), _row(fbias), _row(p["mlstm_norm"][l]), ones_bd, tri(MIX_CHUNK).astype(BF16)]
    rwkv_c = prep[2:12] + [_row(p["rwkv_ln_w"][l]), _row(p["rwkv_ln_b"][l]), tri(MIX_CHUNK).astype(BF16),
                           jnp.tile(jnp.tril(jnp.ones((MIX_CHUNK, MIX_CHUNK), F32), -1), (1, N_HEADS)),
                           jnp.tile(tri(MIX_CHUNK), (1, N_HEADS))]
    up = p["ffn_w_up"][l]
    return dict(
        w_all=w_all, prep=prep, mla=mla_w, post=post, merge=merge, gla_c=gla_c, mlstm_c=mlstm_c, rwkv_c=rwkv_c,
        g_pre_mix=_row(p["norm_pre_mix"][l]),
        g_pre_ca=_row(p["norm_pre_ca"][l]), wq=p["ca_w_q"][l].astype(BF16), wo=p["ca_w_o"][l].astype(BF16),
        g_post_ca=_row(p["norm_post_ca"][l]),
        g_mem=_row(p["ca_norm_mem"][l]),
        wkv=jnp.concatenate([p["ca_w_k"][l], p["ca_w_v"][l]], axis=1).astype(BF16),
        g_pre_ffn=_row(p["norm_pre_ffn"][l]), wa=up[:, :D_FF].astype(BF16), wg=up[:, D_FF:].astype(BF16),
        cw=p["ffn_conv_w"][l].astype(F32), cb=_row(p["ffn_conv_b"][l]), wd=p["ffn_w_down"][l].astype(BF16),
        g_post_ffn=_row(p["norm_post_ffn"][l]),
    )


def _rope_tables(pos):
    half = MLA_ROPE // 2
    inv_freq = ROPE_THETA ** (-jnp.arange(half, dtype=F32) / half)
    ang = pos[:, None] * inv_freq[None, :]
    cos, sin = jnp.cos(ang), jnp.sin(ang)
    return jnp.concatenate([cos, cos], axis=1), jnp.concatenate([-sin, sin], axis=1)


class _Group:
    def __init__(self, n_b, t_len, time_major):
        self.n_b, self.t_len, self.time_major = n_b, t_len, time_major
        self.n_seq = 1 if time_major else n_b
        self.stride = n_b if time_major else 1
        self.rows = n_b * t_len
        self.tm = min(512, self.rows // self.n_seq)
        pairs = n_b * N_HEADS
        self.groups = max(1, LANES // pairs)
        assert not time_major or (self.groups in (1, 2) and (pairs * self.groups) % LANES == 0)

    def to_btc(self, x):
        c = x.shape[-1]
        if self.time_major:
            return x.reshape(self.t_len, self.n_b, c).transpose(1, 0, 2)
        return x.reshape(self.n_b, self.t_len, c)

    def from_btc(self, x):
        c = x.shape[-1]
        if self.time_major:
            return x.transpose(1, 0, 2).reshape(self.rows, c)
        return x.reshape(self.rows, c)

    def key_operand(self, x, dim):
        g = self.groups
        x = self.to_btc(x).reshape(self.n_b, self.t_len, N_HEADS, g, dim // g)
        return x.transpose(1, 4, 3, 0, 2).reshape(self.t_len, dim // g, g * self.n_b * N_HEADS)

    def val_operand(self, x, dim):
        x = self.to_btc(x).reshape(self.n_b, self.t_len, N_HEADS, dim)
        x = x.transpose(1, 3, 0, 2).reshape(self.t_len, dim, self.n_b * N_HEADS)
        return jnp.concatenate([x] * self.groups, axis=2)

    def scan_to_rows(self, o):
        pairs = self.n_b * N_HEADS
        o = o[:, :, :pairs].reshape(self.t_len, 3, HEAD, self.n_b, N_HEADS)
        return self.from_btc(o.transpose(3, 0, 1, 4, 2).reshape(self.n_b, self.t_len, 3 * BRANCH))

    def state_kv_in(self, s, key_axis_last):
        g = self.groups
        if key_axis_last:
            s = jnp.swapaxes(s, 2, 3)
        k, v = s.shape[2], s.shape[3]
        s = s.reshape(self.n_b, N_HEADS, g, k // g, v)
        return s.transpose(3, 4, 2, 0, 1).reshape(k // g, v, g * self.n_b * N_HEADS)

    def state_kv_out(self, s, key_axis_last):
        g = self.groups
        kg, v, _ = s.shape
        s = s.reshape(kg, v, g, self.n_b, N_HEADS).transpose(3, 4, 2, 0, 1).reshape(self.n_b, N_HEADS, g * kg, v)
        return jnp.swapaxes(s, 2, 3) if key_axis_last else s

    def state_k_in(self, s):
        g = self.groups
        k = s.shape[2]
        s = s.reshape(self.n_b, N_HEADS, g, k // g)
        return s.transpose(3, 2, 0, 1).reshape(k // g, g * self.n_b * N_HEADS)

    def state_k_out(self, s):
        g = self.groups
        kg = s.shape[0]
        return s.reshape(kg, g, self.n_b, N_HEADS).transpose(2, 3, 1, 0).reshape(self.n_b, N_HEADS, g * kg)

    def state_m_in(self, s):
        row = jnp.concatenate([s.reshape(1, self.n_b * N_HEADS)] * self.groups, axis=1)
        return jnp.concatenate([row, jnp.zeros((SUBLANES - 1, row.shape[1]), F32)], axis=0)

    def state_m_out(self, s):
        return s[0, : self.n_b * N_HEADS].reshape(self.n_b, N_HEADS)


def _diag_blocks(s, rows, cols):
    n_b = s.shape[0]
    s = s.reshape(n_b, N_HEADS, rows, N_HEADS, cols)
    return jnp.stack([s[:, h, :, h, :] for h in range(N_HEADS)], axis=1)


def _mixer(grp, x, lw, states, pos0, mla_attend):
    p_gla, p_mla, p_rwkv, p_mlstm, p_gate = _mix_in(x, lw["g_pre_mix"], lw["w_all"], min(256, grp.tm))

    pos = pos0 + jnp.arange(grp.t_len, dtype=F32)
    cos2, sin2 = _rope_tables(pos)
    if grp.time_major:
        cos2, sin2 = jnp.repeat(cos2, grp.n_b, axis=0), jnp.repeat(sin2, grp.n_b, axis=0)
    q_lat, q_pe, ckv, kpe = _mla_prep(p_mla, cos2, sin2, lw["mla"], grp.n_seq, grp.tm)
    o_lat = mla_attend(q_lat, q_pe, ckv, kpe)

    recur = _recurrent_scan if grp.time_major else _recurrent_chunked
    o_gla, o_rwkv, o_mlstm, rec_state = recur(grp, p_gla, p_rwkv, p_mlstm, lw, states)
    x_new = _merge(x, o_gla, o_lat, o_rwkv, o_mlstm, p_gate, lw["merge"], min(256, grp.tm))
    gla_new, rwkv_new, mc_new, mn_new, mm_new = rec_state
    shift_new = grp.to_btc(p_rwkv)[:, -1]
    return x_new, (ckv, kpe, gla_new, rwkv_new, shift_new, mc_new, mn_new, mm_new)


def _recurrent_chunked(grp, p_gla, p_rwkv, p_mlstm, lw, states):
    del states
    o_gla, s_gla = _gla_chunk(p_gla, lw["gla_c"], grp.n_b, grp.t_len)
    o_ml, c_new, n_new, m_new = _mlstm_chunk(p_mlstm, lw["mlstm_c"], grp.n_b, grp.t_len)
    o_rw, s_rw = _rwkv_chunk(p_rwkv, lw["rwkv_c"], grp.n_b, grp.t_len)
    gla_new = jnp.swapaxes(_diag_blocks(s_gla, HEAD, GLA_K), 2, 3)
    rwkv_new = _diag_blocks(s_rw, HEAD, HEAD)
    return o_gla, o_rw, o_ml, (gla_new, rwkv_new, c_new, n_new[:, :N_HEADS, :], m_new[:, :N_HEADS, 0])


def _recurrent_scan(grp, p_gla, p_rwkv, p_mlstm, lw, states):
    gla_s, rwkv_s, shift_s, mc_s, mn_s, mm_s = states
    shift0 = shift_s.reshape(grp.n_seq, grp.stride, W_RWKV)
    gla_ops, rw_ops, aux, ml_ops = _prep(p_gla, p_rwkv, p_mlstm, shift0, lw["prep"], grp.n_seq, grp.tm, grp.stride)
    g = grp.groups
    gk = jnp.concatenate([grp.key_operand(gla_ops[:, 0:128], GLA_K), grp.key_operand(p_gla[:, 128:256], GLA_K),
                          grp.key_operand(gla_ops[:, 128:256], GLA_K)], axis=1)
    rk = jnp.concatenate([grp.key_operand(rw_ops[:, c * 256 : (c + 1) * 256], HEAD) for c in range(5)], axis=1)
    mk = jnp.concatenate([grp.key_operand(ml_ops[:, 0:256], HEAD), grp.key_operand(p_mlstm[:, 256:512], HEAD)], axis=1)
    gates = grp.to_btc(ml_ops[:, 256 : 256 + 2 * N_HEADS]).reshape(grp.n_b, grp.t_len, 2, N_HEADS)
    gates = gates.transpose(1, 2, 0, 3).reshape(grp.t_len, 2, grp.n_b * N_HEADS)
    gates = jnp.concatenate([gates] * g, axis=2)
    vops = jnp.concatenate([grp.val_operand(p_gla[:, 256:512], HEAD), grp.val_operand(rw_ops[:, 1280:1536], HEAD),
                            grp.val_operand(p_mlstm[:, 512:768], HEAD), gates,
                            jnp.zeros((grp.t_len, SUBLANES - 2, gates.shape[2]), F32)], axis=1)
    st_in = [grp.state_kv_in(gla_s, False), grp.state_kv_in(rwkv_s, True), grp.state_kv_in(mc_s, False),
             grp.state_k_in(mn_s), grp.state_m_in(mm_s)]
    o_scan, sg, sr, sc, sn, sm = _scan(gk, rk, mk, vops, st_in, grp.t_len, g)
    scan_rows = grp.scan_to_rows(o_scan)
    o_gla, o_rwkv, o_mlstm = _post(scan_rows, p_gla, aux, p_mlstm, lw["post"], min(256, grp.tm))
    return o_gla, o_rwkv, o_mlstm, (grp.state_kv_out(sg, False), grp.state_kv_out(sr, True),
                                    grp.state_kv_out(sc, False), grp.state_k_out(sn), grp.state_m_out(sm))


def _cross_and_ffn(grp, x, lw, mem_k, mem_v, layer, conv_state):
    if grp.time_major:
        (q,) = _norm_matmul(x, lw["g_pre_ca"], lw["wq"], (D_MODEL,), grp.tm)
        ca = _cross_attn(grp.to_btc(q), mem_k, mem_v, layer, grp.t_len)
        x = _proj_res(x, grp.from_btc(ca), lw["wo"], lw["g_post_ca"], grp.tm)
    else:
        x = _cross_fused(x, lw["g_pre_ca"], lw["wq"], mem_k, mem_v, lw["wo"], lw["g_post_ca"], layer, grp.n_b,
                         min(512, grp.t_len))
    x, conv_new = _ffn(x, lw["g_pre_ffn"], lw["wa"], lw["wg"], lw["cw"], lw["cb"], lw["wd"], conv_state,
                       lw["g_post_ffn"], grp.n_seq, max(min(256, grp.tm), 4 * grp.stride), grp.stride)
    return x, conv_new


def kernel(x_prompt, x_sample, mem_prompt, cache_ckv, cache_kpe, page_table, cache_mem_k, cache_mem_v, state_gla, state_rwkv, state_rwkv_shift, state_mlstm_c, state_mlstm_n, state_mlstm_m, state_conv, norm_pre_mix, norm_post_mix, norm_pre_ca, norm_post_ca, norm_pre_ffn, norm_post_ffn, w_in, gla_w_gate2, gla_b_gate, gla_norm, mla_norm_q, mla_norm_kv, mla_w_uq, mla_w_uk, mla_w_uv, rwkv_mu, rwkv_w0, rwkv_w2, rwkv_a0, rwkv_a2, rwkv_g2, rwkv_k_k, rwkv_k_a, rwkv_r_k, rwkv_ln_w, rwkv_ln_b, mlstm_i_bias, mlstm_f_bias, mlstm_norm, w_branch, w_out, ca_norm_mem, ca_w_q, ca_w_k, ca_w_v, ca_w_o, ffn_w_up, ffn_conv_w, ffn_conv_b, ffn_w_down):
    p = dict(norm_pre_mix=norm_pre_mix, norm_post_mix=norm_post_mix, norm_pre_ca=norm_pre_ca, norm_post_ca=norm_post_ca,
             norm_pre_ffn=norm_pre_ffn, norm_post_ffn=norm_post_ffn, w_in=w_in, gla_w_gate2=gla_w_gate2,
             gla_b_gate=gla_b_gate, gla_norm=gla_norm, mla_norm_q=mla_norm_q, mla_norm_kv=mla_norm_kv,
             mla_w_uq=mla_w_uq, mla_w_uk=mla_w_uk, mla_w_uv=mla_w_uv, rwkv_mu=rwkv_mu, rwkv_w0=rwkv_w0,
             rwkv_w2=rwkv_w2, rwkv_a0=rwkv_a0, rwkv_a2=rwkv_a2, rwkv_g2=rwkv_g2, rwkv_k_k=rwkv_k_k,
             rwkv_k_a=rwkv_k_a, rwkv_r_k=rwkv_r_k, rwkv_ln_w=rwkv_ln_w, rwkv_ln_b=rwkv_ln_b,
             mlstm_i_bias=mlstm_i_bias, mlstm_f_bias=mlstm_f_bias, mlstm_norm=mlstm_norm, w_branch=w_branch,
             w_out=w_out, ca_norm_mem=ca_norm_mem, ca_w_q=ca_w_q, ca_w_k=ca_w_k, ca_w_v=ca_w_v, ca_w_o=ca_w_o,
             ffn_w_up=ffn_w_up, ffn_conv_w=ffn_conv_w, ffn_conv_b=ffn_conv_b, ffn_w_down=ffn_w_down)
    depth = w_in.shape[0]
    b_p, t_p, _ = x_prompt.shape
    b_s, t_s, _ = x_sample.shape
    n_pages, page = page_table.shape[1], cache_ckv.shape[2]
    past_len = n_pages * page
    mem_len = mem_prompt.shape[1]
    gp = _Group(b_p, t_p, time_major=False)
    gs = _Group(b_s, t_s, time_major=True)

    xp = gp.from_btc(x_prompt)
    xs = gs.from_btc(x_sample)
    mem_rows = mem_prompt.reshape(b_p * mem_len, D_MODEL)
    mem_k_c = cache_mem_k.reshape(depth, b_s, mem_len, D_MODEL)
    mem_v_c = cache_mem_v.reshape(depth, b_s, mem_len, D_MODEL)
    cache_kpe_t = jnp.swapaxes(cache_kpe, 2, 3)

    conv0_p = jnp.zeros((b_p, 2, D_FF), F32)

    new_p, new_s, mem_k_new, mem_v_new = [], [], [], []
    for l in range(depth):
        lw = _layer_weights(p, l)

        mk, mv = _norm_matmul(mem_rows, lw["g_mem"], lw["wkv"], (D_MODEL, D_MODEL), min(512, mem_rows.shape[0]))
        attend_p = lambda q_lat, q_pe, ckv, kpe: _mla_prompt(q_lat, q_pe, ckv, kpe, b_p, t_p)
        xp, st = _mixer(gp, xp, lw, None, 0.0, attend_p)
        xp, conv_new = _cross_and_ffn(gp, xp, lw, mk.reshape(1, b_p, mem_len, D_MODEL),
                                      mv.reshape(1, b_p, mem_len, D_MODEL), 0, conv0_p)
        new_p.append(st + (conv_new,))
        mem_k_new.append(mk.reshape(b_p, mem_len, N_HEADS, CA_DIM))
        mem_v_new.append(mv.reshape(b_p, mem_len, N_HEADS, CA_DIM))

        def attend_s(q_lat, q_pe, ckv, kpe, l=l):
            rows = t_s * N_HEADS
            ql = gs.to_btc(q_lat).reshape(b_s, rows, MLA_LORA)
            qp = gs.to_btc(q_pe).reshape(b_s, rows, MLA_ROPE)
            cn = jnp.pad(gs.to_btc(ckv), ((0, 0), (0, page - t_s), (0, 0)))
            pn = jnp.pad(gs.to_btc(kpe), ((0, 0), (0, page - t_s), (0, 0)))
            o = _mla_sample(page_table, ql, qp, cn, pn, cache_ckv, cache_kpe_t, l)
            return gs.from_btc(o.reshape(b_s, t_s, N_HEADS * MLA_LORA))

        st_s = (state_gla[l], state_rwkv[l], state_rwkv_shift[l], state_mlstm_c[l], state_mlstm_n[l], state_mlstm_m[l])
        xs, st = _mixer(gs, xs, lw, st_s, float(past_len), attend_s)
        conv_s = state_conv[l].transpose(1, 0, 2).reshape(1, 2 * b_s, D_FF)
        xs, conv_new = _cross_and_ffn(gs, xs, lw, mem_k_c, mem_v_c, l, conv_s)
        new_s.append(st + (conv_new.reshape(2, b_s, D_FF).transpose(1, 0, 2),))

    def stack(per_layer, i, grp):
        vals = [st[i] for st in per_layer]
        if i in (0, 1):
            vals = [grp.to_btc(v) for v in vals]
        return jnp.stack(vals, axis=0)

    outs = [gp.to_btc(xp), gs.to_btc(xs), stack(new_p, 0, gp), stack(new_p, 1, gp), stack(new_s, 0, gs), stack(new_s, 1, gs),
            jnp.stack(mem_k_new, axis=0), jnp.stack(mem_v_new, axis=0)]
    for i in range(2, 9):
        outs.append(stack(new_p, i, gp))
        outs.append(stack(new_s, i, gs))
    return tuple(outs)
```
